```python
import math
import jax, jax.numpy as jnp
from jax import lax
import numpy as np

D_MODEL = 2048
BATCH = 1
SEQ = 8192
DEPTH = 1

CHUNK = 64
NORM_EPS = 1e-6
GDN_HEADS = 8
GDN_HEAD_DIM = 128
GDN_WIDTH = GDN_HEADS * GDN_HEAD_DIM
GDN_CONV = 4
ATT_HEADS = 8
ATT_HEAD_DIM = 128
ATT_WIDTH = ATT_HEADS * ATT_HEAD_DIM
ATT_LEFT_CHUNKS = 8
BAND = (ATT_LEFT_CHUNKS + 1) * CHUNK
REL_MAX = 256
REL_SIZE = REL_MAX + CHUNK
D_FF = 5504
IN_SPLIT = (GDN_WIDTH, GDN_WIDTH, GDN_WIDTH, GDN_WIDTH, GDN_HEADS, GDN_HEADS,
            ATT_WIDTH, ATT_WIDTH, ATT_WIDTH, D_MODEL, D_MODEL)
IN_WIDTH = sum(IN_SPLIT)

kernel_name = "hybrid_gdn_chunkattn_macaron_block"


def rms_norm(x, w):
    xf = x.astype(jnp.float32)
    y = xf * lax.rsqrt(jnp.mean(xf * xf, axis=-1, keepdims=True) + NORM_EPS)
    return (y * w.astype(jnp.float32)).astype(x.dtype)


def l2_norm(x):
    return x * lax.rsqrt(jnp.sum(x * x, axis=-1, keepdims=True) + NORM_EPS)


def swiglu(h, w_gate, w_up, w_down):
    return (jax.nn.silu(h @ w_gate) * (h @ w_up)) @ w_down


def causal_depthwise_conv_silu(x, w):
    K, C = w.shape
    y = lax.conv_general_dilated(
        x, w[:, None, :].astype(x.dtype), window_strides=(1,), padding=[(K - 1, 0)],
        dimension_numbers=("NWC", "WIO", "NWC"), feature_group_count=C)
    return jax.nn.silu(y)


def gated_delta_rule(q, k, v, g, beta):
    B, H, T, Dk = k.shape
    Dv = v.shape[-1]
    n = T // CHUNK
    q = q.reshape(B, H, n, CHUNK, Dk)
    k = k.reshape(B, H, n, CHUNK, Dk)
    v = v.reshape(B, H, n, CHUNK, Dv)
    g = g.reshape(B, H, n, CHUNK)
    beta = beta.reshape(B, H, n, CHUNK)

    G = jnp.cumsum(g, axis=-1)
    causal = jnp.tril(jnp.ones((CHUNK, CHUNK), dtype=bool))
    strict = jnp.tril(jnp.ones((CHUNK, CHUNK), dtype=bool), k=-1)
    decay = jnp.exp(jnp.where(causal, G[..., :, None] - G[..., None, :], -jnp.inf))

    kb = k * beta[..., None]
    A = jnp.where(strict, jnp.einsum('bhnid,bhnjd->bhnij', kb, k) * decay, 0.0)
    eye = jnp.eye(CHUNK, dtype=A.dtype)
    IA = A + eye
    u = lax.linalg.triangular_solve(IA, v * beta[..., None], left_side=True, lower=True)
    w = lax.linalg.triangular_solve(IA, kb * jnp.exp(G)[..., None], left_side=True, lower=True)

    Aqk = jnp.where(causal, jnp.einsum('bhnid,bhnjd->bhnij', q, k) * decay, 0.0)
    qg = q * jnp.exp(G)[..., None]
    kdec = k * jnp.exp(G[..., -1:] - G)[..., None]
    chunk_decay = jnp.exp(G[..., -1])

    def step(S, inp):
        qg_c, kdec_c, u_c, w_c, aqk_c, cd_c = inp
        v_new = u_c - jnp.einsum('bhid,bhde->bhie', w_c, S)
        o_c = jnp.einsum('bhid,bhde->bhie', qg_c, S) + jnp.einsum('bhij,bhje->bhie', aqk_c, v_new)
        S = S * cd_c[..., None, None] + jnp.einsum('bhid,bhie->bhde', kdec_c, v_new)
        return S, o_c

    xs = tuple(jnp.moveaxis(a, 2, 0) for a in (qg, kdec, u, w, Aqk, chunk_decay))
    S0 = jnp.zeros((B, H, Dk, Dv), dtype=q.dtype)
    _, o = lax.scan(step, S0, xs)
    return jnp.moveaxis(o, 0, 2).reshape(B, H, T, Dv)


def gdn_branch(q, k, v, z, a, b, conv_w, A_log, dt_bias, out_norm_w):
    B, T, _ = q.shape
    qkv = causal_depthwise_conv_silu(jnp.concatenate([q, k, v], axis=-1), conv_w)
    q, k, v = jnp.split(qkv, 3, axis=-1)
    heads = lambda t: t.reshape(B, T, GDN_HEADS, GDN_HEAD_DIM).transpose(0, 2, 1, 3).astype(jnp.float32)
    qh = l2_norm(heads(q)) * (GDN_HEAD_DIM ** -0.5)
    kh = l2_norm(heads(k))
    vh = heads(v)
    g = -jnp.exp(A_log.astype(jnp.float32)) * jax.nn.softplus(a.astype(jnp.float32) + dt_bias.astype(jnp.float32))
    beta = jax.nn.sigmoid(b.astype(jnp.float32))
    o = gated_delta_rule(qh, kh, vh, g.transpose(0, 2, 1), beta.transpose(0, 2, 1))
    o = o.transpose(0, 2, 1, 3)
    zf = z.reshape(B, T, GDN_HEADS, GDN_HEAD_DIM).astype(jnp.float32)
    o = rms_norm(o, out_norm_w) * jax.nn.silu(zf)
    return o.reshape(B, T, GDN_WIDTH).astype(z.dtype)


def chunk_attention_branch(q, k, v, q_norm_w, k_norm_w, rel_bias):
    B, T, _ = q.shape
    n = T // CHUNK
    pad = ATT_LEFT_CHUNKS * CHUNK
    heads = lambda t: t.reshape(B, T, ATT_HEADS, ATT_HEAD_DIM).transpose(0, 2, 1, 3)
    qh = rms_norm(heads(q), q_norm_w).astype(jnp.float32)
    kh = rms_norm(heads(k), k_norm_w).astype(jnp.float32)
    vh = heads(v).astype(jnp.float32)

    def band(t):
        tp = jnp.pad(t, ((0, 0), (0, 0), (pad, 0), (0, 0))).reshape(B, ATT_HEADS, n + ATT_LEFT_CHUNKS, CHUNK, ATT_HEAD_DIM)
        return jnp.stack([tp[:, :, s:s + n] for s in range(ATT_LEFT_CHUNKS + 1)], axis=3).reshape(
            B, ATT_HEADS, n, BAND, ATT_HEAD_DIM)

    k_band = band(kh)
    v_band = band(vh)
    qc = qh.reshape(B, ATT_HEADS, n, CHUNK, ATT_HEAD_DIM)
    s = jnp.einsum('bhnqd,bhnkd->bhnqk', qc, k_band) * (ATT_HEAD_DIM ** -0.5)

    i = jnp.arange(CHUNK)[:, None]
    j = jnp.arange(BAND)[None, :]
    rel_idx = jnp.clip(i - j + pad, -(CHUNK - 1), REL_MAX) + (CHUNK - 1)
    bias = rel_bias.astype(jnp.float32)[:, rel_idx]
    key_pos = (jnp.arange(n)[:, None] - ATT_LEFT_CHUNKS) * CHUNK + jnp.arange(BAND)[None, :]
    valid = key_pos >= 0
    s = jnp.where(valid[:, None, :], s + bias[:, None], -jnp.inf)
    p = jax.nn.softmax(s, axis=-1)
    o = jnp.einsum('bhnqk,bhnkd->bhnqd', p, v_band).reshape(B, ATT_HEADS, T, ATT_HEAD_DIM)
    return o.transpose(0, 2, 1, 3).reshape(B, T, ATT_WIDTH).astype(q.dtype)


def setup_inputs(seed: int = 0) -> dict:
    key = jax.random.key(seed)
    ks = jax.random.split(key, 24)
    f32 = jnp.float32
    L = DEPTH
    nrm = lambda k, shape, fan_in: jax.random.normal(k, shape, f32) * (fan_in ** -0.5)
    gain = lambda k, shape: 1.0 + 0.02 * jax.random.normal(k, shape, f32)
    dt = jnp.exp(jax.random.uniform(ks[8], (L, GDN_HEADS), f32, math.log(1e-3), math.log(1e-1)))
    return {
        "x": jax.random.normal(ks[0], (BATCH, SEQ, D_MODEL), f32),
        "ffn1_norm": gain(ks[1], (L, D_MODEL)),
        "ffn1_w_gate": nrm(ks[2], (L, D_MODEL, D_FF), D_MODEL),
        "ffn1_w_up": nrm(ks[3], (L, D_MODEL, D_FF), D_MODEL),
        "ffn1_w_down": nrm(ks[4], (L, D_FF, D_MODEL), D_FF),
        "mix_norm": gain(ks[5], (L, D_MODEL)),
        "w_in": nrm(ks[6], (L, D_MODEL, IN_WIDTH), D_MODEL),
        "gdn_conv": nrm(ks[7], (L, GDN_CONV, 3 * GDN_WIDTH), GDN_CONV),
        "gdn_A_log": jnp.log(jax.random.uniform(ks[9], (L, GDN_HEADS), f32, 1.0, 16.0)),
        "gdn_dt_bias": dt + jnp.log(-jnp.expm1(-dt)),
        "gdn_out_norm": gain(ks[10], (L, GDN_HEAD_DIM)),
        "att_q_norm": gain(ks[11], (L, ATT_HEAD_DIM)),
        "att_k_norm": gain(ks[12], (L, ATT_HEAD_DIM)),
        "att_rel_bias": 0.1 * jax.random.normal(ks[13], (L, ATT_HEADS, REL_SIZE), f32),
        "w_branch_gdn": nrm(ks[14], (L, GDN_WIDTH, D_MODEL), GDN_WIDTH),
        "w_branch_att": nrm(ks[15], (L, ATT_WIDTH, D_MODEL), ATT_WIDTH),
        "w_out": nrm(ks[16], (L, D_MODEL, D_MODEL), D_MODEL),
        "ffn2_norm": gain(ks[17], (L, D_MODEL)),
        "ffn2_w_gate": nrm(ks[18], (L, D_MODEL, D_FF), D_MODEL),
        "ffn2_w_up": nrm(ks[19], (L, D_MODEL, D_FF), D_MODEL),
        "ffn2_w_down": nrm(ks[20], (L, D_FF, D_MODEL), D_FF),
    }


def reference(x, ffn1_norm, ffn1_w_gate, ffn1_w_up, ffn1_w_down, mix_norm, w_in, gdn_conv,
              gdn_A_log, gdn_dt_bias, gdn_out_norm, att_q_norm, att_k_norm, att_rel_bias,
              w_branch_gdn, w_branch_att, w_out, ffn2_norm, ffn2_w_gate, ffn2_w_up, ffn2_w_down):
    split_at = np.cumsum(IN_SPLIT)[:-1].tolist()
    for l in range(DEPTH):
        h = rms_norm(x, ffn1_norm[l])
        x = x + 0.5 * swiglu(h, ffn1_w_gate[l], ffn1_w_up[l], ffn1_w_down[l])

        h = rms_norm(x, mix_norm[l])
        proj = h @ w_in[l]
        (gq, gk, gv, gz, ga, gb, aq, ak, av, gate_gdn, gate_att) = jnp.split(proj, split_at, axis=-1)
        o_gdn = gdn_branch(gq, gk, gv, gz, ga, gb, gdn_conv[l], gdn_A_log[l], gdn_dt_bias[l], gdn_out_norm[l])
        o_att = chunk_attention_branch(aq, ak, av, att_q_norm[l], att_k_norm[l], att_rel_bias[l])
        merged = (jax.nn.sigmoid(gate_gdn) * (o_gdn @ w_branch_gdn[l])
                  + jax.nn.sigmoid(gate_att) * (o_att @ w_branch_att[l]))
        x = x + merged @ w_out[l]

        h = rms_norm(x, ffn2_norm[l])
        x = x + 0.5 * swiglu(h, ffn2_w_gate[l], ffn2_w_up[l], ffn2_w_down[l])
    return x
```

```python
import functools

import jax
import jax.numpy as jnp
from jax import lax
from jax.experimental import pallas as pl
from jax.experimental.pallas import tpu as pltpu

F32 = jnp.float32
BF16 = jnp.bfloat16

NORM_EPS = 1e-6
CHUNK = 64
HEADS = 8
HEAD_DIM = 128
LANES = 128
GDN_CONV = 4
ATT_LEFT_CHUNKS = 8
BAND = (ATT_LEFT_CHUNKS + 1) * CHUNK
REL_MAX = 256
NEG = -1e30

VMEM_LIMIT = 56 * 1024 * 1024

FFN_TM = 512
FFN_TF = 512
PROJ_TM = 512
PROJ_TN = 1024
MERGE_TM = 512
SEQ_BLOCK = 512
ATT_QG = 256
ATT_WIN = ATT_QG + ATT_LEFT_CHUNKS * CHUNK


def _rms(x, w):
    return x * lax.rsqrt(jnp.mean(x * x, axis=-1, keepdims=True) + NORM_EPS) * w


def _dot(a, b):
    return jnp.dot(a, b, preferred_element_type=F32)


def _dot_nt(a, b):
    return lax.dot_general(a, b, (((1,), (1,)), ((), ())), preferred_element_type=F32)


def _dot_tn(a, b):
    return lax.dot_general(a, b, (((0,), (0,)), ((), ())), preferred_element_type=F32)


def _ffn_body(x_ref, nw_ref, wg_ref, wu_ref, wd_ref, o_ref, h_ref):
    j = pl.program_id(1)

    @pl.when(j == 0)
    def _():
        h_ref[...] = _rms(x_ref[...], nw_ref[...]).astype(BF16)
        o_ref[...] = jnp.zeros_like(o_ref)

    h = h_ref[...]
    g = _dot(h, wg_ref[...])
    u = _dot(h, wu_ref[...])
    act = (g * jax.nn.sigmoid(g) * u).astype(BF16)
    o_ref[...] += _dot(act, wd_ref[...])

    @pl.when(j == pl.num_programs(1) - 1)
    def _():
        o_ref[...] = x_ref[...] + 0.5 * o_ref[...]


def _ffn(x, norm_w, wg, wu, wd):
    t, d = x.shape
    fp = wg.shape[1]
    tm, tf = min(FFN_TM, t), FFN_TF
    return pl.pallas_call(
        _ffn_body,
        grid=(t // tm, fp // tf),
        in_specs=[
            pl.BlockSpec((tm, d), lambda i, j: (i, 0)),
            pl.BlockSpec((1, d), lambda i, j: (0, 0)),
            pl.BlockSpec((d, tf), lambda i, j: (0, j)),
            pl.BlockSpec((d, tf), lambda i, j: (0, j)),
            pl.BlockSpec((tf, d), lambda i, j: (j, 0)),
        ],
        out_specs=pl.BlockSpec((tm, d), lambda i, j: (i, 0)),
        out_shape=jax.ShapeDtypeStruct((t, d), F32),
        scratch_shapes=[pltpu.VMEM((tm, d), BF16)],
        compiler_params=pltpu.CompilerParams(
            dimension_semantics=("parallel", "arbitrary"), vmem_limit_bytes=VMEM_LIMIT),
        name="ffn",
    )(x, norm_w.reshape(1, d), wg, wu, wd)


def _proj_body(x_ref, nw_ref, w_ref, wab_ref, p_ref, ab_ref, h_ref):
    j = pl.program_id(1)

    @pl.when(j == 0)
    def _():
        h = _rms(x_ref[...], nw_ref[...]).astype(BF16)
        h_ref[...] = h
        ab_ref[...] = _dot(h, wab_ref[...])

    r = _dot(h_ref[...], w_ref[...])
    for c in range(p_ref.shape[0]):
        p_ref[c] = r[:, c * LANES:(c + 1) * LANES].astype(BF16)


def _proj(x, norm_w, w_big, w_ab):
    t, d = x.shape
    n = w_big.shape[1]
    tm, tn = min(PROJ_TM, t), PROJ_TN
    spb = tn // LANES
    return pl.pallas_call(
        _proj_body,
        grid=(t // tm, n // tn),
        in_specs=[
            pl.BlockSpec((tm, d), lambda i, j: (i, 0)),
            pl.BlockSpec((1, d), lambda i, j: (0, 0)),
            pl.BlockSpec((d, tn), lambda i, j: (0, j)),
            pl.BlockSpec((d, LANES), lambda i, j: (0, 0)),
        ],
        out_specs=[
            pl.BlockSpec((spb, tm, LANES), lambda i, j: (j, i, 0)),
            pl.BlockSpec((tm, LANES), lambda i, j: (i, 0)),
        ],
        out_shape=[
            jax.ShapeDtypeStruct((n // LANES, t, LANES), BF16),
            jax.ShapeDtypeStruct((t, LANES), F32),
        ],
        scratch_shapes=[pltpu.VMEM((tm, d), BF16)],
        compiler_params=pltpu.CompilerParams(
            dimension_semantics=("parallel", "arbitrary"), vmem_limit_bytes=VMEM_LIMIT),
        name="proj",
    )(x, norm_w.reshape(1, d), w_big, w_ab)


def _merge_body(og_ref, oa_ref, gg_ref, ga_ref, x_ref, wa_ref, wb_ref, wo_ref, o_ref):
    def slabs(ref):
        return jnp.concatenate([ref[c] for c in range(ref.shape[0])], axis=-1)

    ya = _dot(slabs(og_ref), wa_ref[...])
    yb = _dot(slabs(oa_ref), wb_ref[...])
    m = (jax.nn.sigmoid(slabs(gg_ref).astype(F32)) * ya
         + jax.nn.sigmoid(slabs(ga_ref).astype(F32)) * yb)
    o_ref[...] = x_ref[...] + _dot(m.astype(BF16), wo_ref[...])


def _merge(og3, oa3, p3, x, wa, wb, wo):
    t, d = x.shape
    tm = min(MERGE_TM, t)
    gs = d // LANES
    const = lambda i: (0, 0)
    return pl.pallas_call(
        _merge_body,
        grid=(t // tm,),
        in_specs=[
            pl.BlockSpec((HEADS, tm, LANES), lambda i: (0, i, 0)),
            pl.BlockSpec((HEADS, tm, LANES), lambda i: (0, i, 0)),
            pl.BlockSpec((gs, tm, LANES), lambda i: (0, i, 0)),
            pl.BlockSpec((gs, tm, LANES), lambda i: (1, i, 0)),
            pl.BlockSpec((tm, d), lambda i: (i, 0)),
            pl.BlockSpec(wa.shape, const, pipeline_mode=pl.Buffered(1)),
            pl.BlockSpec(wb.shape, const, pipeline_mode=pl.Buffered(1)),
            pl.BlockSpec(wo.shape, const, pipeline_mode=pl.Buffered(1)),
        ],
        out_specs=pl.BlockSpec((tm, d), lambda i: (i, 0)),
        out_shape=jax.ShapeDtypeStruct((t, d), F32),
        compiler_params=pltpu.CompilerParams(
            dimension_semantics=("parallel",), vmem_limit_bytes=VMEM_LIMIT),
        name="merge",
    )(og3, oa3, p3, p3, x, wa, wb, wo)


def _att_body(q_ref, k_ref, v_ref, qw_ref, kw_ref, bias_ref, o_ref, kbuf, vbuf):
    blk = pl.program_id(0)
    rb = q_ref.shape[1]

    @pl.when(blk == 0)
    def _():
        kbuf[:, :rb, :] = jnp.zeros((HEADS, rb, LANES), BF16)
        vbuf[:, :rb, :] = jnp.zeros((HEADS, rb, LANES), BF16)

    first_neg = jnp.where(blk == 0, NEG, 0.0).astype(F32)
    col = lax.broadcasted_iota(jnp.int32, (1, ATT_WIN), 1)
    qw = qw_ref[...] * (HEAD_DIM ** -0.5)
    kw = kw_ref[...]

    def head(h, carry):
        kbuf[h, rb:, :] = _rms(k_ref[h].astype(F32), kw).astype(BF16)
        vbuf[h, rb:, :] = v_ref[h]
        for g in range(rb // ATT_QG):
            r0 = g * ATT_QG
            qn = _rms(q_ref[h, r0:r0 + ATT_QG, :].astype(F32), qw).astype(BF16)
            s = _dot_nt(qn, kbuf[h, r0:r0 + ATT_WIN, :])
            s = s + bias_ref[h] + jnp.where(col < rb - r0, first_neg, 0.0)
            m = jnp.max(s, axis=-1, keepdims=True)
            p = jnp.exp(s - m)
            l = jnp.sum(p, axis=-1, keepdims=True)
            o = _dot(p.astype(BF16), vbuf[h, r0:r0 + ATT_WIN, :])
            o_ref[h, r0:r0 + ATT_QG, :] = (o / l).astype(BF16)
        return carry

    lax.fori_loop(0, HEADS, head, 0)
    kbuf[:, :rb, :] = kbuf[:, rb:, :]
    vbuf[:, :rb, :] = vbuf[:, rb:, :]


def _att_bias(rel_bias):
    r = jnp.arange(ATT_QG)[:, None]
    w = jnp.arange(ATT_WIN)[None, :]
    c, i, kc = r // CHUNK, r % CHUNK, w // CHUNK
    valid = (kc >= c) & (kc <= c + ATT_LEFT_CHUNKS)
    dist = i - (w - CHUNK * c) + ATT_LEFT_CHUNKS * CHUNK
    idx = jnp.clip(dist, -(CHUNK - 1), REL_MAX) + (CHUNK - 1)
    return jnp.where(valid[None], rel_bias.astype(F32)[:, idx], NEG)


def _att(p3, q_norm_w, k_norm_w, rel_bias, q_blk, k_blk, v_blk):
    t = p3.shape[1]
    rb = SEQ_BLOCK
    assert rb == ATT_LEFT_CHUNKS * CHUNK and t % rb == 0
    bias = _att_bias(rel_bias)
    return pl.pallas_call(
        _att_body,
        grid=(t // rb,),
        in_specs=[
            pl.BlockSpec((HEADS, rb, LANES), lambda b: (q_blk, b, 0)),
            pl.BlockSpec((HEADS, rb, LANES), lambda b: (k_blk, b, 0)),
            pl.BlockSpec((HEADS, rb, LANES), lambda b: (v_blk, b, 0)),
            pl.BlockSpec((1, LANES), lambda b: (0, 0)),
            pl.BlockSpec((1, LANES), lambda b: (0, 0)),
            pl.BlockSpec(bias.shape, lambda b: (0, 0, 0), pipeline_mode=pl.Buffered(1)),
        ],
        out_specs=pl.BlockSpec((HEADS, rb, LANES), lambda b: (0, b, 0)),
        out_shape=jax.ShapeDtypeStruct((HEADS, t, LANES), BF16),
        scratch_shapes=[pltpu.VMEM((HEADS, 2 * rb, LANES), BF16),
                        pltpu.VMEM((HEADS, 2 * rb, LANES), BF16)],
        compiler_params=pltpu.CompilerParams(
            dimension_semantics=("arbitrary",), vmem_limit_bytes=VMEM_LIMIT),
        name="att",
    )(p3, p3, p3, q_norm_w.reshape(1, LANES), k_norm_w.reshape(1, LANES), bias)


def _split3(x):
    hi = x.astype(BF16)
    r1 = x - hi.astype(F32)
    mid = r1.astype(BF16)
    lo = (r1 - mid.astype(F32)).astype(BF16)
    return hi, mid, lo


def _gdn_body(q_ref, k_ref, v_ref, z_ref, ab_ref, cw_ref, alog_ref, dtb_ref, onw_ref, ls_ref, mx_ref,
              o_ref,
              s_scr, tail_scr, qn_scr, kn_scr, vn_scr, gb_scr, bb_scr, u_scr, wq_scr, aqk_scr, kd_scr,
              cd_scr, oacc_scr):
    blk = pl.program_id(0)
    rows = q_ref.shape[1]
    nchunk = rows // CHUNK

    @pl.when(blk == 0)
    def _():
        s_scr[...] = jnp.zeros_like(s_scr)
        tail_scr[...] = jnp.zeros_like(tail_scr)

    ab = ab_ref[...]
    xa = ab + dtb_ref[...]
    softplus = jnp.maximum(xa, 0.0) + jnp.log1p(jnp.exp(-jnp.abs(xa)))
    g_all = -jnp.exp(alog_ref[...]) * softplus
    beta_all = jax.nn.sigmoid(ab)

    for h in range(HEADS):
        gb_scr[h] = jnp.broadcast_to(g_all[:, h:h + 1], (rows, LANES))
        bb_scr[h] = jnp.broadcast_to(beta_all[:, HEADS + h:HEADS + h + 1], (rows, LANES))

        outs = []
        for part, ref in enumerate((q_ref, k_ref, v_ref)):
            s = part * HEADS + h
            x = ref[h].astype(F32)
            xf = jnp.concatenate([tail_scr[s], x], axis=0)
            w = cw_ref[s]
            y = w[0:1] * xf[5:5 + rows]
            for kk in range(1, GDN_CONV):
                y = y + w[kk:kk + 1] * xf[5 + kk:5 + kk + rows]
            tail_scr[s] = x[rows - 8:rows]
            outs.append(y * jax.nn.sigmoid(y))
        qc, kc, vc = outs
        qn_scr[h] = qc * lax.rsqrt(jnp.sum(qc * qc, axis=-1, keepdims=True) + NORM_EPS) * (HEAD_DIM ** -0.5)
        kn_scr[h] = kc * lax.rsqrt(jnp.sum(kc * kc, axis=-1, keepdims=True) + NORM_EPS)
        vn_scr[h] = vc

    ls = ls_ref[...]
    mx = mx_ref[...]
    ri = lax.broadcasted_iota(jnp.int32, (CHUNK, CHUNK), 0)
    ci = lax.broadcasted_iota(jnp.int32, (CHUNK, CHUNK), 1)
    lower, strict = ri >= ci, ri > ci
    eye = (ri == ci).astype(F32)

    def pre(c, carry):
        r0 = pl.multiple_of(c * CHUNK, CHUNK)
        sl = pl.ds(r0, CHUNK)
        for h in range(HEADS):
            k, q, v = kn_scr[h, sl, :], qn_scr[h, sl, :], vn_scr[h, sl, :]
            gb, bb = gb_scr[h, sl, :], bb_scr[h, sl, :]
            hi, mid, lo = _split3(jnp.concatenate([gb, gb], axis=1) * mx)
            gm = _dot(ls, hi) + _dot(ls, mid) + _dot(ls, lo)
            decay = jnp.exp(gm[:CHUNK, :CHUNK])
            eg = jnp.exp(gm[:CHUNK, LANES:])
            er = jnp.exp(gm[CHUNK:, LANES:])
            kb = k * bb
            kk = _dot_nt(jnp.concatenate([kb, q], axis=0).astype(BF16), k.astype(BF16))
            a = jnp.where(strict, kk[:CHUNK] * decay, 0.0)
            aqk = jnp.where(lower, kk[CHUNK:] * decay, 0.0)
            n = -a
            inv = eye + n
            nb = n.astype(BF16)
            for _ in range(5):
                nb = _dot(nb, nb).astype(BF16)
                inv = inv + _dot(inv.astype(BF16), nb)
            uw = _dot(inv.astype(BF16), jnp.concatenate([v * bb, kb * eg], axis=1).astype(BF16))
            u_scr[h, sl, :] = uw[:, :LANES]
            wq_scr[h, c] = jnp.concatenate([uw[:, LANES:], q * eg], axis=0).astype(BF16)
            aqk_scr[h, c] = aqk.astype(BF16)
            kd_scr[h, sl, :] = (k * er).astype(BF16)
            cd_scr[h, c] = jnp.broadcast_to(eg[CHUNK - 1:CHUNK, :], (8, LANES))
        return carry

    lax.fori_loop(0, nchunk, pre, 0)

    def scan(c, carry):
        r0 = pl.multiple_of(c * CHUNK, CHUNK)
        sl = pl.ds(r0, CHUNK)
        for h in range(HEADS):
            st = s_scr[h]
            wqs = _dot(wq_scr[h, c], st.astype(BF16))
            vnew = (u_scr[h, sl, :] - wqs[:CHUNK]).astype(BF16)
            oacc_scr[h, sl, :] = wqs[CHUNK:] + _dot(aqk_scr[h, c], vnew)
            s_scr[h] = st * cd_scr[h, c][0:1, :] + _dot_tn(kd_scr[h, sl, :], vnew)
        return carry

    lax.fori_loop(0, nchunk, scan, 0)

    onw = onw_ref[...]
    for h in range(HEADS):
        z = z_ref[h].astype(F32)
        o_ref[h] = (_rms(oacc_scr[h], onw) * (z * jax.nn.sigmoid(z))).astype(BF16)


def _gdn(p3, ab, conv_w, a_log, dt_bias, out_norm_w, q_blk):
    t = p3.shape[1]
    rows = min(SEQ_BLOCK, t)
    nchunk = rows // CHUNK
    cw = conv_w.astype(F32).reshape(GDN_CONV, 3 * HEADS, LANES).transpose(1, 0, 2)
    pad = lambda v: jnp.zeros((1, LANES), F32).at[0, :HEADS].set(v.astype(F32))
    ti = jnp.arange(CHUNK)
    ls = jnp.concatenate([ti[None, :] <= ti[:, None], ti[None, :] > ti[:, None]], axis=0).astype(BF16)
    mx = jnp.concatenate([(ti[:, None] > ti[None, :]).astype(F32), jnp.zeros((CHUNK, CHUNK), F32),
                          jnp.ones((CHUNK, LANES), F32)], axis=1)
    slab = lambda off: pl.BlockSpec((HEADS, rows, LANES), lambda b: (q_blk + off, b, 0))
    const2 = lambda b: (0, 0)
    hr = (HEADS, rows, LANES)
    return pl.pallas_call(
        _gdn_body,
        grid=(t // rows,),
        in_specs=[
            slab(0), slab(1), slab(2), slab(3),
            pl.BlockSpec((rows, LANES), lambda b: (b, 0)),
            pl.BlockSpec(cw.shape, lambda b: (0, 0, 0)),
            pl.BlockSpec((1, LANES), const2),
            pl.BlockSpec((1, LANES), const2),
            pl.BlockSpec((1, LANES), const2),
            pl.BlockSpec(ls.shape, const2),
            pl.BlockSpec(mx.shape, const2),
        ],
        out_specs=pl.BlockSpec(hr, lambda b: (0, b, 0)),
        out_shape=jax.ShapeDtypeStruct((HEADS, t, LANES), BF16),
        scratch_shapes=[
            pltpu.VMEM((HEADS, HEAD_DIM, HEAD_DIM), F32),
            pltpu.VMEM((3 * HEADS, 8, LANES), F32),
            pltpu.VMEM(hr, F32), pltpu.VMEM(hr, F32), pltpu.VMEM(hr, F32),
            pltpu.VMEM(hr, F32), pltpu.VMEM(hr, F32),
            pltpu.VMEM(hr, F32),
            pltpu.VMEM((HEADS, nchunk, 2 * CHUNK, LANES), BF16),
            pltpu.VMEM((HEADS, nchunk, CHUNK, CHUNK), BF16),
            pltpu.VMEM(hr, BF16),
            pltpu.VMEM((HEADS, nchunk, 8, LANES), F32),
            pltpu.VMEM(hr, F32),
        ],
        compiler_params=pltpu.CompilerParams(
            dimension_semantics=("arbitrary",), vmem_limit_bytes=VMEM_LIMIT),
        name="gdn",
    )(p3, p3, p3, p3, ab, cw, pad(a_log), pad(dt_bias), out_norm_w.astype(F32).reshape(1, LANES), ls, mx)


def _pad_to(w, axis, mult):
    n = w.shape[axis]
    extra = (-n) % mult
    if extra == 0:
        return w
    widths = [(0, 0)] * w.ndim
    widths[axis] = (0, extra)
    return jnp.pad(w, widths)


def _ffn_weights(wg, wu, wd):
    return (_pad_to(wg.astype(BF16), 1, FFN_TF), _pad_to(wu.astype(BF16), 1, FFN_TF),
            _pad_to(wd.astype(BF16), 0, FFN_TF))


@jax.jit
def _forward(x, ffn1_norm, ffn1_w_gate, ffn1_w_up, ffn1_w_down, mix_norm, w_in, gdn_conv,
             gdn_A_log, gdn_dt_bias, gdn_out_norm, att_q_norm, att_k_norm, att_rel_bias,
             w_branch_gdn, w_branch_att, w_out, ffn2_norm, ffn2_w_gate, ffn2_w_up, ffn2_w_down):
    b, t, d = x.shape
    gw = HEADS * HEAD_DIM
    outs = []
    for bi in range(b):
        xb = x[bi]
        for l in range(ffn1_norm.shape[0]):
            xb = _ffn(xb, ffn1_norm[l], *_ffn_weights(ffn1_w_gate[l], ffn1_w_up[l], ffn1_w_down[l]))

            wi = w_in[l]
            o_ab = 4 * gw
            o_att = o_ab + 2 * HEADS
            o_gate = o_att + 3 * gw
            w_big = jnp.concatenate([wi[:, o_gate:], wi[:, :o_ab], wi[:, o_att:o_gate]], axis=1).astype(BF16)
            w_ab = _pad_to(wi[:, o_ab:o_att].astype(BF16), 1, LANES)
            p3, ab = _proj(xb, mix_norm[l], w_big, w_ab)
            gate_blocks = 2 * d // gw
            og3 = _gdn(p3, ab, gdn_conv[l], gdn_A_log[l], gdn_dt_bias[l], gdn_out_norm[l], gate_blocks)
            oa3 = _att(p3, att_q_norm[l], att_k_norm[l], att_rel_bias[l],
                       gate_blocks + 4, gate_blocks + 5, gate_blocks + 6)
            xb = _merge(og3, oa3, p3, xb, w_branch_gdn[l].astype(BF16), w_branch_att[l].astype(BF16),
                        w_out[l].astype(BF16))

            xb = _ffn(xb, ffn2_norm[l], *_ffn_weights(ffn2_w_gate[l], ffn2_w_up[l], ffn2_w_down[l]))
        outs.append(xb)
    return jnp.stack(outs, axis=0)


def kernel(x, ffn1_norm, ffn1_w_gate, ffn1_w_up, ffn1_w_down, mix_norm, w_in, gdn_conv, gdn_A_log, gdn_dt_bias, gdn_out_norm, att_q_norm, att_k_norm, att_rel_bias, w_branch_gdn, w_branch_att, w_out, ffn2_norm, ffn2_w_gate, ffn2_w_up, ffn2_w_down):
    return _forward(x, ffn1_norm, ffn1_w_gate, ffn1_w_up, ffn1_w_down, mix_norm, w_in, gdn_conv,
                    gdn_A_log, gdn_dt_bias, gdn_out_norm, att_q_norm, att_k_norm, att_rel_bias,
                    w_branch_gdn, w_branch_att, w_out, ffn2_norm, ffn2_w_gate, ffn2_w_up, ffn2_w_down)
```

```python
import jax
import jax.numpy as jnp
from jax import lax
from jax.experimental import pallas as pl
from jax.experimental.pallas import tpu as pltpu

F32 = jnp.float32
BF16 = jnp.bfloat16

NORM_EPS = 1e-6
CHUNK = 64
HEADS = 8
HEAD_DIM = 128
LANES = 128
GDN_CONV = 4
ATT_LEFT_CHUNKS = 8
REL_MAX = 256
NEG = -1e30

VMEM_LIMIT = 56 * 1024 * 1024

FFN_TM = 512
FFN_TF = 512
PROJ_TM = 512
PROJ_TN = 1024
MERGE_TM = 512
SEQ_BLOCK = 512
ATT_QG = 256
ATT_WIN = ATT_QG + ATT_LEFT_CHUNKS * CHUNK
ATT_TBL = 1024
PAIR = 2 * CHUNK


def _rms(x, w):
    return x * lax.rsqrt(jnp.mean(x * x, axis=-1, keepdims=True) + NORM_EPS) * w


def _dot(a, b):
    return jnp.dot(a, b, preferred_element_type=F32)


def _dot_nt(a, b):
    return lax.dot_general(a, b, (((1,), (1,)), ((), ())), preferred_element_type=F32)


def _dot_tn(a, b):
    return lax.dot_general(a, b, (((0,), (0,)), ((), ())), preferred_element_type=F32)


def _ffn_body(x_ref, nw_ref, wg_ref, wu_ref, wd_ref, o_ref, h_ref):
    j = pl.program_id(1)

    @pl.when(j == 0)
    def _():
        h_ref[...] = _rms(x_ref[...], nw_ref[...]).astype(BF16)
        o_ref[...] = jnp.zeros_like(o_ref)

    h = h_ref[...]
    g = _dot(h, wg_ref[...])
    u = _dot(h, wu_ref[...])
    act = (g * jax.nn.sigmoid(g) * u).astype(BF16)
    o_ref[...] += _dot(act, wd_ref[...])

    @pl.when(j == pl.num_programs(1) - 1)
    def _():
        o_ref[...] = x_ref[...] + 0.5 * o_ref[...]


def _ffn(x, norm_w, wg, wu, wd):
    t, d = x.shape
    fp = wg.shape[1]
    tm, tf = min(FFN_TM, t), FFN_TF
    return pl.pallas_call(
        _ffn_body,
        grid=(t // tm, fp // tf),
        in_specs=[
            pl.BlockSpec((tm, d), lambda i, j: (i, 0)),
            pl.BlockSpec((1, d), lambda i, j: (0, 0)),
            pl.BlockSpec((d, tf), lambda i, j: (0, j)),
            pl.BlockSpec((d, tf), lambda i, j: (0, j)),
            pl.BlockSpec((tf, d), lambda i, j: (j, 0)),
        ],
        out_specs=pl.BlockSpec((tm, d), lambda i, j: (i, 0)),
        out_shape=jax.ShapeDtypeStruct((t, d), F32),
        scratch_shapes=[pltpu.VMEM((tm, d), BF16)],
        compiler_params=pltpu.CompilerParams(
            dimension_semantics=("parallel", "arbitrary"), vmem_limit_bytes=VMEM_LIMIT),
        name="ffn",
    )(x, norm_w.reshape(1, d), wg, wu, wd)


def _proj_body(x_ref, nw_ref, w_ref, wab_ref, p_ref, ab_ref, h_ref):
    j = pl.program_id(1)

    @pl.when(j == 0)
    def _():
        h = _rms(x_ref[...], nw_ref[...]).astype(BF16)
        h_ref[...] = h
        ab_ref[...] = _dot(h, wab_ref[...])

    r = _dot(h_ref[...], w_ref[...])
    for c in range(p_ref.shape[0]):
        p_ref[c] = r[:, c * LANES:(c + 1) * LANES].astype(BF16)


def _proj(x, norm_w, w_big, w_ab):
    t, d = x.shape
    n = w_big.shape[1]
    tm, tn = min(PROJ_TM, t), PROJ_TN
    spb = tn // LANES
    return pl.pallas_call(
        _proj_body,
        grid=(t // tm, n // tn),
        in_specs=[
            pl.BlockSpec((tm, d), lambda i, j: (i, 0)),
            pl.BlockSpec((1, d), lambda i, j: (0, 0)),
            pl.BlockSpec((d, tn), lambda i, j: (0, j)),
            pl.BlockSpec((d, LANES), lambda i, j: (0, 0)),
        ],
        out_specs=[
            pl.BlockSpec((spb, tm, LANES), lambda i, j: (j, i, 0)),
            pl.BlockSpec((tm, LANES), lambda i, j: (i, 0)),
        ],
        out_shape=[
            jax.ShapeDtypeStruct((n // LANES, t, LANES), BF16),
            jax.ShapeDtypeStruct((t, LANES), F32),
        ],
        scratch_shapes=[pltpu.VMEM((tm, d), BF16)],
        compiler_params=pltpu.CompilerParams(
            dimension_semantics=("parallel", "arbitrary"), vmem_limit_bytes=VMEM_LIMIT),
        name="proj",
    )(x, norm_w.reshape(1, d), w_big, w_ab)


def _merge_body(og_ref, oa_ref, gg_ref, ga_ref, x_ref, wa_ref, wb_ref, wo_ref, o_ref):
    def slabs(ref):
        return jnp.concatenate([ref[c] for c in range(ref.shape[0])], axis=-1)

    ya = _dot(slabs(og_ref), wa_ref[...])
    yb = _dot(slabs(oa_ref), wb_ref[...])
    m = (jax.nn.sigmoid(slabs(gg_ref).astype(F32)) * ya
         + jax.nn.sigmoid(slabs(ga_ref).astype(F32)) * yb)
    o_ref[...] = x_ref[...] + _dot(m.astype(BF16), wo_ref[...])


def _merge(og3, oa3, p3, x, wa, wb, wo):
    t, d = x.shape
    tm = min(MERGE_TM, t)
    gs = d // LANES
    const = lambda i: (0, 0)
    return pl.pallas_call(
        _merge_body,
        grid=(t // tm,),
        in_specs=[
            pl.BlockSpec((HEADS, tm, LANES), lambda i: (0, i, 0)),
            pl.BlockSpec((HEADS, tm, LANES), lambda i: (0, i, 0)),
            pl.BlockSpec((gs, tm, LANES), lambda i: (0, i, 0)),
            pl.BlockSpec((gs, tm, LANES), lambda i: (1, i, 0)),
            pl.BlockSpec((tm, d), lambda i: (i, 0)),
            pl.BlockSpec(wa.shape, const, pipeline_mode=pl.Buffered(1)),
            pl.BlockSpec(wb.shape, const, pipeline_mode=pl.Buffered(1)),
            pl.BlockSpec(wo.shape, const, pipeline_mode=pl.Buffered(1)),
        ],
        out_specs=pl.BlockSpec((tm, d), lambda i: (i, 0)),
        out_shape=jax.ShapeDtypeStruct((t, d), F32),
        compiler_params=pltpu.CompilerParams(
            dimension_semantics=("parallel",), vmem_limit_bytes=VMEM_LIMIT),
        name="merge",
    )(og3, oa3, p3, p3, x, wa, wb, wo)


def _att_body(q_ref, k_ref, v_ref, qw_ref, kw_ref, bvec_ref, o_ref, kbuf, vbuf, bias_scr):
    blk = pl.program_id(0)
    rb = q_ref.shape[1]

    @pl.when(blk == 0)
    def _():
        kbuf[:, :rb, :] = jnp.zeros((HEADS, rb, LANES), BF16)
        vbuf[:, :rb, :] = jnp.zeros((HEADS, rb, LANES), BF16)
        qc = lax.broadcasted_iota(jnp.int32, (ATT_QG, ATT_WIN), 0) // CHUNK
        kc = lax.broadcasted_iota(jnp.int32, (ATT_QG, ATT_WIN), 1) // CHUNK
        off = kc - qc
        band = jnp.where(off >= 0, jnp.where(off <= ATT_LEFT_CHUNKS, 0.0, NEG), NEG).astype(F32)
        for h in range(HEADS):
            tbl = jnp.broadcast_to(bvec_ref[h], (ATT_QG, ATT_TBL))
            tbl = pltpu.roll(tbl, 0, 1, stride=1, stride_axis=0)
            bias_scr[h] = tbl[:, :ATT_WIN] + band

    first_neg = jnp.where(blk == 0, NEG, 0.0).astype(F32)
    col = lax.broadcasted_iota(jnp.int32, (1, ATT_WIN), 1)
    qw = qw_ref[...] * (HEAD_DIM ** -0.5)
    kw = kw_ref[...]

    def head(h, carry):
        kbuf[h, rb:, :] = _rms(k_ref[h].astype(F32), kw).astype(BF16)
        vbuf[h, rb:, :] = v_ref[h]
        for g in range(rb // ATT_QG):
            r0 = g * ATT_QG
            qn = _rms(q_ref[h, r0:r0 + ATT_QG, :].astype(F32), qw).astype(BF16)
            s = _dot_nt(qn, kbuf[h, r0:r0 + ATT_WIN, :])
            s = s + bias_scr[h] + jnp.where(col < rb - r0, first_neg, 0.0)
            m = jnp.max(s, axis=-1, keepdims=True)
            p = jnp.exp(s - m)
            l = jnp.sum(p, axis=-1, keepdims=True)
            o = _dot(p.astype(BF16), vbuf[h, r0:r0 + ATT_WIN, :])
            o_ref[h, r0:r0 + ATT_QG, :] = (o / l).astype(BF16)
        return carry

    lax.fori_loop(0, HEADS, head, 0)
    kbuf[:, :rb, :] = kbuf[:, rb:, :]
    vbuf[:, :rb, :] = vbuf[:, rb:, :]


def _att_bias_table(rel_bias):
    tbl = rel_bias.astype(F32)
    n_tbl = tbl.shape[1]
    far = ATT_LEFT_CHUNKS * CHUNK - REL_MAX
    rep = lambda c, n: jnp.broadcast_to(tbl[:, c:c + 1], (tbl.shape[0], n))
    vec = jnp.concatenate([rep(n_tbl - 1, far), tbl[:, ::-1],
                           rep(0, ATT_TBL - far - n_tbl - (ATT_QG - 1)), rep(n_tbl - 1, ATT_QG - 1)], axis=1)
    return vec[:, None, :]


def _att(p3, q_norm_w, k_norm_w, rel_bias, q_blk, k_blk, v_blk):
    t = p3.shape[1]
    rb = SEQ_BLOCK
    assert rb == ATT_LEFT_CHUNKS * CHUNK and t % rb == 0
    bvec = _att_bias_table(rel_bias)
    return pl.pallas_call(
        _att_body,
        grid=(t // rb,),
        in_specs=[
            pl.BlockSpec((HEADS, rb, LANES), lambda b: (q_blk, b, 0)),
            pl.BlockSpec((HEADS, rb, LANES), lambda b: (k_blk, b, 0)),
            pl.BlockSpec((HEADS, rb, LANES), lambda b: (v_blk, b, 0)),
            pl.BlockSpec((1, LANES), lambda b: (0, 0)),
            pl.BlockSpec((1, LANES), lambda b: (0, 0)),
            pl.BlockSpec(bvec.shape, lambda b: (0, 0, 0)),
        ],
        out_specs=pl.BlockSpec((HEADS, rb, LANES), lambda b: (0, b, 0)),
        out_shape=jax.ShapeDtypeStruct((HEADS, t, LANES), BF16),
        scratch_shapes=[pltpu.VMEM((HEADS, 2 * rb, LANES), BF16),
                        pltpu.VMEM((HEADS, 2 * rb, LANES), BF16),
                        pltpu.VMEM((HEADS, ATT_QG, ATT_WIN), F32)],
        compiler_params=pltpu.CompilerParams(
            dimension_semantics=("arbitrary",), vmem_limit_bytes=VMEM_LIMIT),
        name="att",
    )(p3, p3, p3, q_norm_w.reshape(1, LANES), k_norm_w.reshape(1, LANES), bvec)


def _split2(x):
    hi = x.astype(BF16)
    return hi, (x - hi.astype(F32)).astype(BF16)


def _gdn_body(q_ref, k_ref, v_ref, z_ref, ab_ref, cw_ref, alog_ref, dtb_ref, onw_ref, ls_ref, mx_ref,
              o_ref,
              s_scr, xp_scr, qn_scr, kn_scr, vn_scr, gb_scr, bb_scr, u_scr, wq_scr, aqk_scr, kw_scr,
              bm_scr, cd_scr, oacc_scr):
    blk = pl.program_id(0)
    rows = q_ref.shape[1]
    npair = rows // PAIR
    hs = range(HEADS)

    @pl.when(blk == 0)
    def _():
        s_scr[...] = jnp.zeros_like(s_scr)
        xp_scr[:, 0:8, :] = jnp.zeros((3 * HEADS, 8, LANES), F32)

    ab = ab_ref[...]
    xa = ab + dtb_ref[...]
    softplus = jnp.maximum(xa, 0.0) + jnp.log(1.0 + jnp.exp(-jnp.abs(xa)))
    g_all = -jnp.exp(alog_ref[...]) * softplus
    beta_all = jax.nn.sigmoid(ab)

    for h in hs:
        gb_scr[h] = jnp.broadcast_to(g_all[:, h:h + 1], (rows, LANES))
        bb_scr[h] = jnp.broadcast_to(beta_all[:, HEADS + h:HEADS + h + 1], (rows, LANES))

        outs = []
        for part, ref in enumerate((q_ref, k_ref, v_ref)):
            s = part * HEADS + h
            xp_scr[s, 8:8 + rows, :] = ref[h].astype(F32)
            w = cw_ref[s]
            y = w[0:1] * xp_scr[s, 5:5 + rows, :]
            for tap in range(1, GDN_CONV):
                y = y + w[tap:tap + 1] * xp_scr[s, 5 + tap:5 + tap + rows, :]
            outs.append(y * jax.nn.sigmoid(y))
        qc, kc, vc = outs
        qn_scr[h] = qc * lax.rsqrt(jnp.sum(qc * qc, axis=-1, keepdims=True) + NORM_EPS) * (HEAD_DIM ** -0.5)
        kn_scr[h] = kc * lax.rsqrt(jnp.sum(kc * kc, axis=-1, keepdims=True) + NORM_EPS)
        vn_scr[h] = vc

    for s in range(3 * HEADS):
        xp_scr[s, 0:8, :] = xp_scr[s, rows:rows + 8, :]

    ls = ls_ref[...]
    mx = mx_ref[...]
    ri = lax.broadcasted_iota(jnp.int32, (PAIR, PAIR), 0)
    ci = lax.broadcasted_iota(jnp.int32, (PAIR, PAIR), 1)
    same = (ri // CHUNK) == (ci // CHUNK)
    lower_f = jnp.where(same, jnp.where(ri >= ci, 1.0, 0.0), 0.0).astype(F32)
    strict_f = jnp.where(same, jnp.where(ri > ci, 1.0, 0.0), 0.0).astype(F32)
    eye = jnp.where(ri == ci, 1.0, 0.0).astype(F32)

    def pre(p, carry):
        sl = pl.ds(pl.multiple_of(p * PAIR, PAIR), PAIR)
        k = [kn_scr[h, sl, :] for h in hs]
        q = [qn_scr[h, sl, :] for h in hs]
        v = [vn_scr[h, sl, :] for h in hs]
        gb = [gb_scr[h, sl, :] for h in hs]
        bb = [bb_scr[h, sl, :] for h in hs]
        parts = [_split2(jnp.concatenate([gb[h], gb[h]], axis=1) * mx) for h in hs]
        gm = [_dot(ls, pt[0]) + _dot(ls, pt[1]) for pt in parts]
        decay = [jnp.exp(g[:, :PAIR]) for g in gm]
        gi = [g[:, PAIR:] for g in gm]
        glast = [jnp.concatenate([jnp.broadcast_to(g[c * CHUNK - 1:c * CHUNK, :], (CHUNK, LANES))
                                  for c in (1, 2)], axis=0) for g in gi]
        eg = [jnp.exp(g) for g in gi]
        er = [jnp.exp(glast[h] - gi[h]) for h in hs]
        kb = [k[h] * bb[h] for h in hs]
        kk = [_dot_nt(jnp.concatenate([kb[h], q[h]], axis=0).astype(BF16), k[h].astype(BF16)) for h in hs]
        n = [-(kk[h][:PAIR] * decay[h] * strict_f) for h in hs]
        aqk = [kk[h][PAIR:] * decay[h] * lower_f for h in hs]
        inv = [eye + n[h] for h in hs]
        nb = [n[h].astype(BF16) for h in hs]
        nb = [_dot(nb[h], nb[h]).astype(BF16) for h in hs]
        for it in range(5):
            if it < 4:
                m = [_dot(jnp.concatenate([inv[h].astype(BF16), nb[h]], axis=0), nb[h]) for h in hs]
                inv = [inv[h] + m[h][:PAIR] for h in hs]
                nb = [m[h][PAIR:].astype(BF16) for h in hs]
            else:
                inv = [inv[h] + _dot(inv[h].astype(BF16), nb[h]) for h in hs]
        uw = [_dot(inv[h].astype(BF16),
                   jnp.concatenate([v[h] * bb[h], kb[h] * eg[h]], axis=1).astype(BF16)) for h in hs]
        kd = [(k[h] * er[h]).astype(BF16) for h in hs]
        qg = [q[h] * eg[h] for h in hs]
        for cc in range(2):
            c = 2 * p + cc
            rs = slice(cc * CHUNK, (cc + 1) * CHUNK)
            kwb = [_dot_tn(kd[h][rs], jnp.concatenate([uw[h][rs, LANES:], uw[h][rs, :LANES]], axis=1).astype(BF16))
                   for h in hs]
            for h in hs:
                wq_scr[h, c] = jnp.concatenate([uw[h][rs, LANES:], qg[h][rs]], axis=0).astype(BF16)
                kw_scr[h, c] = kwb[h][:, :LANES].astype(BF16)
                bm_scr[h, c] = kwb[h][:, LANES:]
                cd_scr[h, c] = jnp.broadcast_to(eg[h][(cc + 1) * CHUNK - 1:(cc + 1) * CHUNK, :], (8, LANES))
        for h in hs:
            u_scr[h, sl, :] = uw[h][:, :LANES]
            aqk_scr[h, p] = aqk[h].astype(BF16)
        return carry

    lax.fori_loop(0, npair, pre, 0)

    def chain(p, carry):
        sl = pl.ds(pl.multiple_of(p * PAIR, PAIR), PAIR)
        c0, c1 = 2 * p, 2 * p + 1
        s0 = [s_scr[h] for h in hs]
        sb0 = [s0[h].astype(BF16) for h in hs]
        s1 = [s0[h] * cd_scr[h, c0][0:1, :] + (bm_scr[h, c0] - _dot(kw_scr[h, c0], sb0[h])) for h in hs]
        sb1 = [s1[h].astype(BF16) for h in hs]
        s2 = [s1[h] * cd_scr[h, c1][0:1, :] + (bm_scr[h, c1] - _dot(kw_scr[h, c1], sb1[h])) for h in hs]
        for h in hs:
            s_scr[h] = s2[h]
        a0 = [_dot(wq_scr[h, c0], sb0[h]) for h in hs]
        a1 = [_dot(wq_scr[h, c1], sb1[h]) for h in hs]
        vnew = [(u_scr[h, sl, :] - jnp.concatenate([a0[h][:CHUNK], a1[h][:CHUNK]], axis=0)).astype(BF16)
                for h in hs]
        for h in hs:
            oacc_scr[h, sl, :] = (jnp.concatenate([a0[h][CHUNK:], a1[h][CHUNK:]], axis=0)
                                  + _dot(aqk_scr[h, p], vnew[h]))
        return carry

    lax.fori_loop(0, npair, chain, 0)

    onw = onw_ref[...]
    for h in hs:
        z = z_ref[h].astype(F32)
        o_ref[h] = (_rms(oacc_scr[h], onw) * (z * jax.nn.sigmoid(z))).astype(BF16)


def _gdn(p3, ab, conv_w, a_log, dt_bias, out_norm_w, q_blk):
    t = p3.shape[1]
    rows = min(SEQ_BLOCK, t)
    nchunk, npair = rows // CHUNK, rows // PAIR
    cw = conv_w.astype(F32).reshape(GDN_CONV, 3 * HEADS, LANES).transpose(1, 0, 2)
    pad = lambda v: jnp.zeros((1, LANES), F32).at[0, :HEADS].set(v.astype(F32))
    ti = jnp.arange(PAIR)
    same = (ti[:, None] // CHUNK) == (ti[None, :] // CHUNK)
    le = same & (ti[None, :] <= ti[:, None])
    gt = same & (ti[None, :] > ti[:, None])
    ls = le.astype(BF16)
    mx = jnp.concatenate([gt.T.astype(F32), jnp.ones((PAIR, LANES), F32)], axis=1)
    slab = lambda off: pl.BlockSpec((HEADS, rows, LANES), lambda b: (q_blk + off, b, 0))
    const2 = lambda b: (0, 0)
    hr = (HEADS, rows, LANES)
    return pl.pallas_call(
        _gdn_body,
        grid=(t // rows,),
        in_specs=[
            slab(0), slab(1), slab(2), slab(3),
            pl.BlockSpec((rows, LANES), lambda b: (b, 0)),
            pl.BlockSpec(cw.shape, lambda b: (0, 0, 0)),
            pl.BlockSpec((1, LANES), const2),
            pl.BlockSpec((1, LANES), const2),
            pl.BlockSpec((1, LANES), const2),
            pl.BlockSpec(ls.shape, const2),
            pl.BlockSpec(mx.shape, const2),
        ],
        out_specs=pl.BlockSpec(hr, lambda b: (0, b, 0)),
        out_shape=jax.ShapeDtypeStruct((HEADS, t, LANES), BF16),
        scratch_shapes=[
            pltpu.VMEM((HEADS, HEAD_DIM, HEAD_DIM), F32),
            pltpu.VMEM((3 * HEADS, rows + 8, LANES), F32),
            pltpu.VMEM(hr, F32), pltpu.VMEM(hr, F32), pltpu.VMEM(hr, F32),
            pltpu.VMEM(hr, F32), pltpu.VMEM(hr, F32),
            pltpu.VMEM(hr, F32),
            pltpu.VMEM((HEADS, nchunk, 2 * CHUNK, LANES), BF16),
            pltpu.VMEM((HEADS, npair, PAIR, PAIR), BF16),
            pltpu.VMEM((HEADS, nchunk, HEAD_DIM, HEAD_DIM), BF16),
            pltpu.VMEM((HEADS, nchunk, HEAD_DIM, HEAD_DIM), F32),
            pltpu.VMEM((HEADS, nchunk, 8, LANES), F32),
            pltpu.VMEM(hr, F32),
        ],
        compiler_params=pltpu.CompilerParams(
            dimension_semantics=("arbitrary",), vmem_limit_bytes=VMEM_LIMIT),
        name="gdn",
    )(p3, p3, p3, p3, ab, cw, pad(a_log), pad(dt_bias), out_norm_w.astype(F32).reshape(1, LANES), ls, mx)


def _pad_to(w, axis, mult):
    n = w.shape[axis]
    extra = (-n) % mult
    if extra == 0:
        return w
    widths = [(0, 0)] * w.ndim
    widths[axis] = (0, extra)
    return jnp.pad(w, widths)


def _ffn_weights(wg, wu, wd):
    return (_pad_to(wg.astype(BF16), 1, FFN_TF), _pad_to(wu.astype(BF16), 1, FFN_TF),
            _pad_to(wd.astype(BF16), 0, FFN_TF))


@jax.jit
def _forward(x, ffn1_norm, ffn1_w_gate, ffn1_w_up, ffn1_w_down, mix_norm, w_in, gdn_conv,
             gdn_A_log, gdn_dt_bias, gdn_out_norm, att_q_norm, att_k_norm, att_rel_bias,
             w_branch_gdn, w_branch_att, w_out, ffn2_norm, ffn2_w_gate, ffn2_w_up, ffn2_w_down):
    b, t, d = x.shape
    gw = HEADS * HEAD_DIM
    outs = []
    for bi in range(b):
        xb = x[bi]
        for l in range(ffn1_norm.shape[0]):
            xb = _ffn(xb, ffn1_norm[l], *_ffn_weights(ffn1_w_gate[l], ffn1_w_up[l], ffn1_w_down[l]))

            wi = w_in[l]
            o_ab = 4 * gw
            o_att = o_ab + 2 * HEADS
            o_gate = o_att + 3 * gw
            w_big = jnp.concatenate([wi[:, o_gate:], wi[:, :o_ab], wi[:, o_att:o_gate]], axis=1).astype(BF16)
            w_ab = _pad_to(wi[:, o_ab:o_att].astype(BF16), 1, LANES)
            p3, ab = _proj(xb, mix_norm[l], w_big, w_ab)
            gate_blocks = 2 * d // gw
            og3 = _gdn(p3, ab, gdn_conv[l], gdn_A_log[l], gdn_dt_bias[l], gdn_out_norm[l], gate_blocks)
            oa3 = _att(p3, att_q_norm[l], att_k_norm[l], att_rel_bias[l],
                       gate_blocks + 4, gate_blocks + 5, gate_blocks + 6)
            xb = _merge(og3, oa3, p3, xb, w_branch_gdn[l].astype(BF16), w_branch_att[l].astype(BF16),
                        w_out[l].astype(BF16))

            xb = _ffn(xb, ffn2_norm[l], *_ffn_weights(ffn2_w_gate[l], ffn2_w_up[l], ffn2_w_down[l]))
        outs.append(xb)
    return jnp.stack(outs, axis=0)


def kernel(x, ffn1_norm, ffn1_w_gate, ffn1_w_up, ffn1_w_down, mix_norm, w_in, gdn_conv, gdn_A_log, gdn_dt_bias, gdn_out_norm, att_q_norm, att_k_norm, att_rel_bias, w_branch_gdn, w_branch_att, w_out, ffn2_norm, ffn2_w_gate, ffn2_w_up, ffn2_w_down):
    return _forward(x, ffn1_norm, ffn1_w_gate, ffn1_w_up, ffn1_w_down, mix_norm, w_in, gdn_conv,
                    gdn_A_log, gdn_dt_bias, gdn_out_norm, att_q_norm, att_k_norm, att_rel_bias,
                    w_branch_gdn, w_branch_att, w_out, ffn2_norm, ffn2_w_gate, ffn2_w_up, ffn2_w_down)
```

```python
import jax
import jax.numpy as jnp
from jax import lax
from jax.experimental import pallas as pl
from jax.experimental.pallas import tpu as pltpu

F32 = jnp.float32
BF16 = jnp.bfloat16

NORM_EPS = 1e-6
CHUNK = 64
HEADS = 8
HEAD_DIM = 128
LANES = 128
GDN_CONV = 4
ATT_LEFT_CHUNKS = 8
REL_MAX = 256
NEG = -1e30

VMEM_LIMIT = 56 * 1024 * 1024

NORM_TM = 512
FFN_UP_TM = 1024
FFN_TF = 512
FFN_DOWN_TM = 256
PROJ_TM = 1024
PROJ_TN = 1024
MERGE_TM = 512
SEQ_BLOCK = 512
ATT_QG = 256
ATT_WIN = ATT_QG + ATT_LEFT_CHUNKS * CHUNK
ATT_TBL = 1024
PAIR = 2 * CHUNK


def _rms(x, w):
    return x * lax.rsqrt(jnp.mean(x * x, axis=-1, keepdims=True) + NORM_EPS) * w


def _dot(a, b):
    return jnp.dot(a, b, preferred_element_type=F32)


def _dot_nt(a, b):
    return lax.dot_general(a, b, (((1,), (1,)), ((), ())), preferred_element_type=F32)


def _dot_tn(a, b):
    return lax.dot_general(a, b, (((0,), (0,)), ((), ())), preferred_element_type=F32)


def _norm_body(x_ref, nw_ref, o_ref):
    o_ref[...] = _rms(x_ref[...], nw_ref[...]).astype(BF16)


def _norm(x, norm_w):
    t, d = x.shape
    tm = min(NORM_TM, t)
    return pl.pallas_call(
        _norm_body,
        grid=(t // tm,),
        in_specs=[pl.BlockSpec((tm, d), lambda i: (i, 0)), pl.BlockSpec((1, d), lambda i: (0, 0))],
        out_specs=pl.BlockSpec((tm, d), lambda i: (i, 0)),
        out_shape=jax.ShapeDtypeStruct((t, d), BF16),
        compiler_params=pltpu.CompilerParams(dimension_semantics=("parallel",), vmem_limit_bytes=VMEM_LIMIT),
        name="norm",
    )(x, norm_w.reshape(1, d))


def _ffn_up_body(h_ref, wg_ref, wu_ref, o_ref, w_scr):
    tf = wg_ref.shape[1]

    @pl.when(pl.program_id(1) == 0)
    def _():
        w_scr[:, :tf] = wg_ref[...].astype(BF16)
        w_scr[:, tf:] = wu_ref[...].astype(BF16)

    gu = _dot(h_ref[...], w_scr[...])
    g, u = gu[:, :tf], gu[:, tf:]
    o_ref[...] = (g * jax.nn.sigmoid(g) * u).astype(BF16)


def _ffn_up(h, wg, wu):
    t, d = h.shape
    f = wg.shape[1]
    tm, tf = min(FFN_UP_TM, t), FFN_TF
    return pl.pallas_call(
        _ffn_up_body,
        grid=(pl.cdiv(f, tf), t // tm),
        in_specs=[
            pl.BlockSpec((tm, d), lambda j, i: (i, 0)),
            pl.BlockSpec((d, tf), lambda j, i: (0, j)),
            pl.BlockSpec((d, tf), lambda j, i: (0, j)),
        ],
        out_specs=pl.BlockSpec((tm, tf), lambda j, i: (i, j)),
        out_shape=jax.ShapeDtypeStruct((t, f), BF16),
        scratch_shapes=[pltpu.VMEM((d, 2 * tf), BF16)],
        compiler_params=pltpu.CompilerParams(
            dimension_semantics=("arbitrary", "arbitrary"), vmem_limit_bytes=VMEM_LIMIT),
        name="ffn_up",
    )(h, wg, wu)


def _ffn_down_body(a_ref, wd_ref, x_ref, o_ref):
    o_ref[...] = x_ref[...] + 0.5 * _dot(a_ref[...], wd_ref[...])


def _ffn_down_norm_body(a_ref, wd_ref, x_ref, nw_ref, o_ref, hn_ref):
    y = x_ref[...] + 0.5 * _dot(a_ref[...], wd_ref[...])
    o_ref[...] = y
    hn_ref[...] = _rms(y, nw_ref[...]).astype(BF16)


def _ffn_down(act, wd, x, next_norm_w=None):
    t, d = x.shape
    f = act.shape[1]
    tm = min(FFN_DOWN_TM, t)
    row = pl.BlockSpec((tm, d), lambda i: (i, 0))
    in_specs = [pl.BlockSpec((tm, f), lambda i: (i, 0)),
                pl.BlockSpec((f, d), lambda i: (0, 0), pipeline_mode=pl.Buffered(1)),
                row]
    params = pltpu.CompilerParams(dimension_semantics=("parallel",), vmem_limit_bytes=VMEM_LIMIT)
    if next_norm_w is None:
        return pl.pallas_call(
            _ffn_down_body, grid=(t // tm,), in_specs=in_specs, out_specs=row,
            out_shape=jax.ShapeDtypeStruct((t, d), F32), compiler_params=params, name="ffn_down",
        )(act, wd, x)
    return pl.pallas_call(
        _ffn_down_norm_body, grid=(t // tm,),
        in_specs=in_specs + [pl.BlockSpec((1, d), lambda i: (0, 0))],
        out_specs=[row, row],
        out_shape=[jax.ShapeDtypeStruct((t, d), F32), jax.ShapeDtypeStruct((t, d), BF16)],
        compiler_params=params, name="ffn_down_norm",
    )(act, wd, x, next_norm_w.reshape(1, d))


def _proj_body(h_ref, w_ref, wab_ref, p_ref, ab_ref):
    h = h_ref[...]

    @pl.when(pl.program_id(1) == 0)
    def _():
        ab_ref[...] = _dot(h, wab_ref[...])

    r = _dot(h, w_ref[...])
    for c in range(p_ref.shape[0]):
        p_ref[c] = r[:, c * LANES:(c + 1) * LANES].astype(BF16)


def _proj(h, w_big, w_ab):
    t, d = h.shape
    n = w_big.shape[1]
    tm, tn = min(PROJ_TM, t), PROJ_TN
    spb = tn // LANES
    return pl.pallas_call(
        _proj_body,
        grid=(t // tm, n // tn),
        in_specs=[
            pl.BlockSpec((tm, d), lambda i, j: (i, 0)),
            pl.BlockSpec((d, tn), lambda i, j: (0, j)),
            pl.BlockSpec((d, LANES), lambda i, j: (0, 0)),
        ],
        out_specs=[
            pl.BlockSpec((spb, tm, LANES), lambda i, j: (j, i, 0)),
            pl.BlockSpec((tm, LANES), lambda i, j: (i, 0)),
        ],
        out_shape=[
            jax.ShapeDtypeStruct((n // LANES, t, LANES), BF16),
            jax.ShapeDtypeStruct((t, LANES), F32),
        ],
        compiler_params=pltpu.CompilerParams(
            dimension_semantics=("parallel", "arbitrary"), vmem_limit_bytes=VMEM_LIMIT),
        name="proj",
    )(h, w_big, w_ab)


def _merge_body(og_ref, oa_ref, gg_ref, ga_ref, x_ref, wa_ref, wb_ref, wo_ref, nw_ref, o_ref, hn_ref):
    def slabs(ref):
        return jnp.concatenate([ref[c] for c in range(ref.shape[0])], axis=-1)

    ya = _dot(slabs(og_ref), wa_ref[...])
    yb = _dot(slabs(oa_ref), wb_ref[...])
    m = (jax.nn.sigmoid(slabs(gg_ref).astype(F32)) * ya
         + jax.nn.sigmoid(slabs(ga_ref).astype(F32)) * yb)
    y = x_ref[...] + _dot(m.astype(BF16), wo_ref[...])
    o_ref[...] = y
    hn_ref[...] = _rms(y, nw_ref[...]).astype(BF16)


def _merge(og3, oa3, p3, x, wa, wb, wo, next_norm_w):
    t, d = x.shape
    tm = min(MERGE_TM, t)
    gs = d // LANES
    const = lambda i: (0, 0)
    row = pl.BlockSpec((tm, d), lambda i: (i, 0))
    return pl.pallas_call(
        _merge_body,
        grid=(t // tm,),
        in_specs=[
            pl.BlockSpec((HEADS, tm, LANES), lambda i: (0, i, 0)),
            pl.BlockSpec((HEADS, tm, LANES), lambda i: (0, i, 0)),
            pl.BlockSpec((gs, tm, LANES), lambda i: (0, i, 0)),
            pl.BlockSpec((gs, tm, LANES), lambda i: (1, i, 0)),
            row,
            pl.BlockSpec(wa.shape, const, pipeline_mode=pl.Buffered(1)),
            pl.BlockSpec(wb.shape, const, pipeline_mode=pl.Buffered(1)),
            pl.BlockSpec(wo.shape, const, pipeline_mode=pl.Buffered(1)),
            pl.BlockSpec((1, d), const),
        ],
        out_specs=[row, row],
        out_shape=[jax.ShapeDtypeStruct((t, d), F32), jax.ShapeDtypeStruct((t, d), BF16)],
        compiler_params=pltpu.CompilerParams(
            dimension_semantics=("parallel",), vmem_limit_bytes=VMEM_LIMIT),
        name="merge",
    )(og3, oa3, p3, p3, x, wa, wb, wo, next_norm_w.reshape(1, d))


def _att_body(q_ref, k_ref, v_ref, qw_ref, kw_ref, bvec_ref, o_ref, kbuf, vbuf, bias_scr):
    blk = pl.program_id(0)
    rb = q_ref.shape[1]

    @pl.when(blk == 0)
    def _():
        kbuf[:, :rb, :] = jnp.zeros((HEADS, rb, LANES), BF16)
        vbuf[:, :rb, :] = jnp.zeros((HEADS, rb, LANES), BF16)
        qc = lax.broadcasted_iota(jnp.int32, (ATT_QG, ATT_WIN), 0) // CHUNK
        kc = lax.broadcasted_iota(jnp.int32, (ATT_QG, ATT_WIN), 1) // CHUNK
        off = kc - qc
        band = jnp.where(off >= 0, jnp.where(off <= ATT_LEFT_CHUNKS, 0.0, NEG), NEG).astype(F32)
        for h in range(HEADS):
            tbl = jnp.broadcast_to(bvec_ref[h], (ATT_QG, ATT_TBL))
            tbl = pltpu.roll(tbl, 0, 1, stride=1, stride_axis=0)
            bias_scr[h] = tbl[:, :ATT_WIN] + band

    first_neg = jnp.where(blk == 0, NEG, 0.0).astype(F32)
    col = lax.broadcasted_iota(jnp.int32, (1, ATT_WIN), 1)
    qw = qw_ref[...] * (HEAD_DIM ** -0.5)
    kw = kw_ref[...]

    def head(h, carry):
        kbuf[h, rb:, :] = _rms(k_ref[h].astype(F32), kw).astype(BF16)
        vbuf[h, rb:, :] = v_ref[h]
        for g in range(rb // ATT_QG):
            r0 = g * ATT_QG
            qn = _rms(q_ref[h, r0:r0 + ATT_QG, :].astype(F32), qw).astype(BF16)
            s = _dot_nt(qn, kbuf[h, r0:r0 + ATT_WIN, :])
            s = s + bias_scr[h] + jnp.where(col < rb - r0, first_neg, 0.0)
            m = jnp.max(s, axis=-1, keepdims=True)
            p = jnp.exp(s - m)
            l = jnp.sum(p, axis=-1, keepdims=True)
            o = _dot(p.astype(BF16), vbuf[h, r0:r0 + ATT_WIN, :])
            o_ref[h, r0:r0 + ATT_QG, :] = (o / l).astype(BF16)
        return carry

    lax.fori_loop(0, HEADS, head, 0)
    kbuf[:, :rb, :] = kbuf[:, rb:, :]
    vbuf[:, :rb, :] = vbuf[:, rb:, :]


def _att_bias_table(rel_bias):
    tbl = rel_bias.astype(F32)
    n_tbl = tbl.shape[1]
    far = ATT_LEFT_CHUNKS * CHUNK - REL_MAX
    rep = lambda c, n: jnp.broadcast_to(tbl[:, c:c + 1], (tbl.shape[0], n))
    vec = jnp.concatenate([rep(n_tbl - 1, far), tbl[:, ::-1],
                           rep(0, ATT_TBL - far - n_tbl - (ATT_QG - 1)), rep(n_tbl - 1, ATT_QG - 1)], axis=1)
    return vec[:, None, :]


def _att(p3, q_norm_w, k_norm_w, rel_bias, q_blk, k_blk, v_blk):
    t = p3.shape[1]
    rb = SEQ_BLOCK
    assert rb == ATT_LEFT_CHUNKS * CHUNK and t % rb == 0
    bvec = _att_bias_table(rel_bias)
    return pl.pallas_call(
        _att_body,
        grid=(t // rb,),
        in_specs=[
            pl.BlockSpec((HEADS, rb, LANES), lambda b: (q_blk, b, 0)),
            pl.BlockSpec((HEADS, rb, LANES), lambda b: (k_blk, b, 0)),
            pl.BlockSpec((HEADS, rb, LANES), lambda b: (v_blk, b, 0)),
            pl.BlockSpec((1, LANES), lambda b: (0, 0)),
            pl.BlockSpec((1, LANES), lambda b: (0, 0)),
            pl.BlockSpec(bvec.shape, lambda b: (0, 0, 0)),
        ],
        out_specs=pl.BlockSpec((HEADS, rb, LANES), lambda b: (0, b, 0)),
        out_shape=jax.ShapeDtypeStruct((HEADS, t, LANES), BF16),
        scratch_shapes=[pltpu.VMEM((HEADS, 2 * rb, LANES), BF16),
                        pltpu.VMEM((HEADS, 2 * rb, LANES), BF16),
                        pltpu.VMEM((HEADS, ATT_QG, ATT_WIN), F32)],
        compiler_params=pltpu.CompilerParams(
            dimension_semantics=("arbitrary",), vmem_limit_bytes=VMEM_LIMIT),
        name="att",
    )(p3, p3, p3, q_norm_w.reshape(1, LANES), k_norm_w.reshape(1, LANES), bvec)


def _split2(x):
    hi = x.astype(BF16)
    return hi, (x - hi.astype(F32)).astype(BF16)


def _gdn_body(q_ref, k_ref, v_ref, z_ref, ab_ref, cw_ref, alog_ref, dtb_ref, onw_ref, ls_ref, mx_ref,
              o_ref,
              s_scr, xp_scr, qn_scr, kn_scr, vn_scr, gb_scr, bb_scr, u_scr, wq_scr, aqk_scr, kw_scr,
              bm_scr, cd_scr, oacc_scr):
    blk = pl.program_id(0)
    rows = q_ref.shape[1]
    npair = rows // PAIR
    hs = range(HEADS)

    @pl.when(blk == 0)
    def _():
        s_scr[...] = jnp.zeros_like(s_scr)
        xp_scr[:, 0:8, :] = jnp.zeros((3 * HEADS, 8, LANES), F32)

    ab = ab_ref[...]
    xa = ab + dtb_ref[...]
    softplus = jnp.maximum(xa, 0.0) + jnp.log(1.0 + jnp.exp(-jnp.abs(xa)))
    g_all = -jnp.exp(alog_ref[...]) * softplus
    beta_all = jax.nn.sigmoid(ab)

    for h in hs:
        gb_scr[h] = jnp.broadcast_to(g_all[:, h:h + 1], (rows, LANES))
        bb_scr[h] = jnp.broadcast_to(beta_all[:, HEADS + h:HEADS + h + 1], (rows, LANES))

        outs = []
        for part, ref in enumerate((q_ref, k_ref, v_ref)):
            s = part * HEADS + h
            xp_scr[s, 8:8 + rows, :] = ref[h].astype(F32)
            w = cw_ref[s]
            y = w[0:1] * xp_scr[s, 5:5 + rows, :]
            for tap in range(1, GDN_CONV):
                y = y + w[tap:tap + 1] * xp_scr[s, 5 + tap:5 + tap + rows, :]
            outs.append(y * jax.nn.sigmoid(y))
        qc, kc, vc = outs
        qn_scr[h] = qc * lax.rsqrt(jnp.sum(qc * qc, axis=-1, keepdims=True) + NORM_EPS) * (HEAD_DIM ** -0.5)
        kn_scr[h] = kc * lax.rsqrt(jnp.sum(kc * kc, axis=-1, keepdims=True) + NORM_EPS)
        vn_scr[h] = vc

    for s in range(3 * HEADS):
        xp_scr[s, 0:8, :] = xp_scr[s, rows:rows + 8, :]

    ls = ls_ref[...]
    mx = mx_ref[...]
    ri = lax.broadcasted_iota(jnp.int32, (PAIR, PAIR), 0)
    ci = lax.broadcasted_iota(jnp.int32, (PAIR, PAIR), 1)
    same = (ri // CHUNK) == (ci // CHUNK)
    lower_f = jnp.where(same, jnp.where(ri >= ci, 1.0, 0.0), 0.0).astype(F32)
    strict_f = jnp.where(same, jnp.where(ri > ci, 1.0, 0.0), 0.0).astype(F32)
    eye = jnp.where(ri == ci, 1.0, 0.0).astype(F32)

    def pre(p, carry):
        sl = pl.ds(pl.multiple_of(p * PAIR, PAIR), PAIR)
        k = [kn_scr[h, sl, :] for h in hs]
        q = [qn_scr[h, sl, :] for h in hs]
        v = [vn_scr[h, sl, :] for h in hs]
        gb = [gb_scr[h, sl, :] for h in hs]
        bb = [bb_scr[h, sl, :] for h in hs]
        parts = [_split2(jnp.concatenate([gb[h], gb[h]], axis=1) * mx) for h in hs]
        gm = [_dot(ls, pt[0]) + _dot(ls, pt[1]) for pt in parts]
        decay = [jnp.exp(g[:, :PAIR]) for g in gm]
        gi = [g[:, PAIR:] for g in gm]
        glast = [jnp.concatenate([jnp.broadcast_to(g[c * CHUNK - 1:c * CHUNK, :], (CHUNK, LANES))
                                  for c in (1, 2)], axis=0) for g in gi]
        eg = [jnp.exp(g) for g in gi]
        er = [jnp.exp(glast[h] - gi[h]) for h in hs]
        kb = [k[h] * bb[h] for h in hs]
        kk = [_dot_nt(jnp.concatenate([kb[h], q[h]], axis=0).astype(BF16), k[h].astype(BF16)) for h in hs]
        n = [-(kk[h][:PAIR] * decay[h] * strict_f) for h in hs]
        aqk = [kk[h][PAIR:] * decay[h] * lower_f for h in hs]
        inv = [eye + n[h] for h in hs]
        nb = [n[h].astype(BF16) for h in hs]
        nb = [_dot(nb[h], nb[h]).astype(BF16) for h in hs]
        for it in range(5):
            if it < 4:
                m = [_dot(jnp.concatenate([inv[h].astype(BF16), nb[h]], axis=0), nb[h]) for h in hs]
                inv = [inv[h] + m[h][:PAIR] for h in hs]
                nb = [m[h][PAIR:].astype(BF16) for h in hs]
            else:
                inv = [inv[h] + _dot(inv[h].astype(BF16), nb[h]) for h in hs]
        uw = [_dot(inv[h].astype(BF16),
                   jnp.concatenate([v[h] * bb[h], kb[h] * eg[h]], axis=1).astype(BF16)) for h in hs]
        kd = [(k[h] * er[h]).astype(BF16) for h in hs]
        qg = [q[h] * eg[h] for h in hs]
        for cc in range(2):
            c = 2 * p + cc
            rs = slice(cc * CHUNK, (cc + 1) * CHUNK)
            kwb = [_dot_tn(kd[h][rs], jnp.concatenate([uw[h][rs, LANES:], uw[h][rs, :LANES]], axis=1).astype(BF16))
                   for h in hs]
            for h in hs:
                wq_scr[h, c] = jnp.concatenate([uw[h][rs, LANES:], qg[h][rs]], axis=0).astype(BF16)
                kw_scr[h, c] = kwb[h][:, :LANES].astype(BF16)
                bm_scr[h, c] = kwb[h][:, LANES:]
                cd_scr[h, c] = jnp.broadcast_to(eg[h][(cc + 1) * CHUNK - 1:(cc + 1) * CHUNK, :], (8, LANES))
        for h in hs:
            u_scr[h, sl, :] = uw[h][:, :LANES]
            aqk_scr[h, p] = aqk[h].astype(BF16)
        return carry

    lax.fori_loop(0, npair, pre, 0)

    def chain(p, carry):
        sl = pl.ds(pl.multiple_of(p * PAIR, PAIR), PAIR)
        c0, c1 = 2 * p, 2 * p + 1
        s0 = [s_scr[h] for h in hs]
        sb0 = [s0[h].astype(BF16) for h in hs]
        s1 = [s0[h] * cd_scr[h, c0][0:1, :] + (bm_scr[h, c0] - _dot(kw_scr[h, c0], sb0[h])) for h in hs]
        sb1 = [s1[h].astype(BF16) for h in hs]
        s2 = [s1[h] * cd_scr[h, c1][0:1, :] + (bm_scr[h, c1] - _dot(kw_scr[h, c1], sb1[h])) for h in hs]
        for h in hs:
            s_scr[h] = s2[h]
        a0 = [_dot(wq_scr[h, c0], sb0[h]) for h in hs]
        a1 = [_dot(wq_scr[h, c1], sb1[h]) for h in hs]
        vnew = [(u_scr[h, sl, :] - jnp.concatenate([a0[h][:CHUNK], a1[h][:CHUNK]], axis=0)).astype(BF16)
                for h in hs]
        for h in hs:
            oacc_scr[h, sl, :] = (jnp.concatenate([a0[h][CHUNK:], a1[h][CHUNK:]], axis=0)
                                  + _dot(aqk_scr[h, p], vnew[h]))
        return carry

    lax.fori_loop(0, npair, chain, 0)

    onw = onw_ref[...]
    for h in hs:
        z = z_ref[h].astype(F32)
        o_ref[h] = (_rms(oacc_scr[h], onw) * (z * jax.nn.sigmoid(z))).astype(BF16)


def _gdn(p3, ab, conv_w, a_log, dt_bias, out_norm_w, q_blk):
    t = p3.shape[1]
    rows = min(SEQ_BLOCK, t)
    nchunk, npair = rows // CHUNK, rows // PAIR
    cw = conv_w.astype(F32).reshape(GDN_CONV, 3 * HEADS, LANES).transpose(1, 0, 2)
    pad = lambda v: jnp.zeros((1, LANES), F32).at[0, :HEADS].set(v.astype(F32))
    ti = jnp.arange(PAIR)
    same = (ti[:, None] // CHUNK) == (ti[None, :] // CHUNK)
    le = same & (ti[None, :] <= ti[:, None])
    gt = same & (ti[None, :] > ti[:, None])
    ls = le.astype(BF16)
    mx = jnp.concatenate([gt.T.astype(F32), jnp.ones((PAIR, LANES), F32)], axis=1)
    slab = lambda off: pl.BlockSpec((HEADS, rows, LANES), lambda b: (q_blk + off, b, 0))
    const2 = lambda b: (0, 0)
    hr = (HEADS, rows, LANES)
    return pl.pallas_call(
        _gdn_body,
        grid=(t // rows,),
        in_specs=[
            slab(0), slab(1), slab(2), slab(3),
            pl.BlockSpec((rows, LANES), lambda b: (b, 0)),
            pl.BlockSpec(cw.shape, lambda b: (0, 0, 0)),
            pl.BlockSpec((1, LANES), const2),
            pl.BlockSpec((1, LANES), const2),
            pl.BlockSpec((1, LANES), const2),
            pl.BlockSpec(ls.shape, const2),
            pl.BlockSpec(mx.shape, const2),
        ],
        out_specs=pl.BlockSpec(hr, lambda b: (0, b, 0)),
        out_shape=jax.ShapeDtypeStruct((HEADS, t, LANES), BF16),
        scratch_shapes=[
            pltpu.VMEM((HEADS, HEAD_DIM, HEAD_DIM), F32),
            pltpu.VMEM((3 * HEADS, rows + 8, LANES), F32),
            pltpu.VMEM(hr, F32), pltpu.VMEM(hr, F32), pltpu.VMEM(hr, F32),
            pltpu.VMEM(hr, F32), pltpu.VMEM(hr, F32),
            pltpu.VMEM(hr, F32),
            pltpu.VMEM((HEADS, nchunk, 2 * CHUNK, LANES), BF16),
            pltpu.VMEM((HEADS, npair, PAIR, PAIR), BF16),
            pltpu.VMEM((HEADS, nchunk, HEAD_DIM, HEAD_DIM), BF16),
            pltpu.VMEM((HEADS, nchunk, HEAD_DIM, HEAD_DIM), F32),
            pltpu.VMEM((HEADS, nchunk, 8, LANES), F32),
            pltpu.VMEM(hr, F32),
        ],
        compiler_params=pltpu.CompilerParams(
            dimension_semantics=("arbitrary",), vmem_limit_bytes=VMEM_LIMIT),
        name="gdn",
    )(p3, p3, p3, p3, ab, cw, pad(a_log), pad(dt_bias), out_norm_w.astype(F32).reshape(1, LANES), ls, mx)


def _pad_to(w, axis, mult):
    n = w.shape[axis]
    extra = (-n) % mult
    if extra == 0:
        return w
    widths = [(0, 0)] * w.ndim
    widths[axis] = (0, extra)
    return jnp.pad(w, widths)


@jax.jit
def _forward(x, ffn1_norm, ffn1_w_gate, ffn1_w_up, ffn1_w_down, mix_norm, w_in, gdn_conv,
             gdn_A_log, gdn_dt_bias, gdn_out_norm, att_q_norm, att_k_norm, att_rel_bias,
             w_branch_gdn, w_branch_att, w_out, ffn2_norm, ffn2_w_gate, ffn2_w_up, ffn2_w_down):
    b, t, d = x.shape
    gw = HEADS * HEAD_DIM
    outs = []
    for bi in range(b):
        xb = x[bi]
        depth = ffn1_norm.shape[0]
        hn = _norm(xb, ffn1_norm[0])
        for l in range(depth):
            act = _ffn_up(hn, ffn1_w_gate[l], ffn1_w_up[l])
            xb, hn = _ffn_down(act, ffn1_w_down[l].astype(BF16), xb, mix_norm[l])

            wi = w_in[l]
            o_ab = 4 * gw
            o_att = o_ab + 2 * HEADS
            o_gate = o_att + 3 * gw
            w_big = jnp.concatenate([wi[:, o_gate:], wi[:, :o_ab], wi[:, o_att:o_gate]], axis=1).astype(BF16)
            w_ab = _pad_to(wi[:, o_ab:o_att].astype(BF16), 1, LANES)
            p3, ab = _proj(hn, w_big, w_ab)
            gate_blocks = 2 * d // gw
            og3 = _gdn(p3, ab, gdn_conv[l], gdn_A_log[l], gdn_dt_bias[l], gdn_out_norm[l], gate_blocks)
            oa3 = _att(p3, att_q_norm[l], att_k_norm[l], att_rel_bias[l],
                       gate_blocks + 4, gate_blocks + 5, gate_blocks + 6)
            xb, hn = _merge(og3, oa3, p3, xb, w_branch_gdn[l].astype(BF16), w_branch_att[l].astype(BF16),
                            w_out[l].astype(BF16), ffn2_norm[l])

            act = _ffn_up(hn, ffn2_w_gate[l], ffn2_w_up[l])
            xb, hn = (_ffn_down(act, ffn2_w_down[l].astype(BF16), xb, ffn1_norm[l + 1]) if l + 1 < depth
                      else (_ffn_down(act, ffn2_w_down[l].astype(BF16), xb), None))
        outs.append(xb)
    return jnp.stack(outs, axis=0)


def kernel(x, ffn1_norm, ffn1_w_gate, ffn1_w_up, ffn1_w_down, mix_norm, w_in, gdn_conv, gdn_A_log, gdn_dt_bias, gdn_out_norm, att_q_norm, att_k_norm, att_rel_bias, w_branch_gdn, w_branch_att, w_out, ffn2_norm, ffn2_w_gate, ffn2_w_up, ffn2_w_down):
    return _forward(x, ffn1_norm, ffn1_w_gate, ffn1_w_up, ffn1_w_down, mix_norm, w_in, gdn_conv,
                    gdn_A_log, gdn_dt_bias, gdn_out_norm, att_q_norm, att_k_norm, att_rel_bias,
                    w_branch_gdn, w_branch_att, w_out, ffn2_norm, ffn2_w_gate, ffn2_w_up, ffn2_w_down)
```

```python
import functools
import math

import jax
import jax.numpy as jnp
from jax import lax
from jax.experimental import pallas as pl
from jax.experimental.pallas import tpu as pltpu

F32 = jnp.float32
BF16 = jnp.bfloat16

NORM_EPS = 1e-6
CHUNK = 64
HEADS = 8
HEAD_DIM = 128
LANES = 128
GDN_CONV = 4
ATT_LEFT_CHUNKS = 8
REL_MAX = 256
NEG = -1e30

VMEM_LIMIT = 56 * 1024 * 1024

LOG2E = math.log2(math.e)

NORM_TM = 512
FFN_UP_TM = 2048
FFN_TF = 512
FFN_DOWN_TM = 256
PROJ_TM = 1024
PROJ_TN = 1024
ATT_HEADS_PER_ITER = 2
MERGE_TM = 512
SEQ_BLOCK = 512
ATT_QG = 256
ATT_WIN = ATT_QG + ATT_LEFT_CHUNKS * CHUNK
ATT_TBL = 1024
PAIR = 2 * CHUNK


def _rms(x, w):
    return x * lax.rsqrt(jnp.mean(x * x, axis=-1, keepdims=True) + NORM_EPS) * w


def _dot(a, b):
    return jnp.dot(a, b, preferred_element_type=F32)


def _dot_nt(a, b):
    return lax.dot_general(a, b, (((1,), (1,)), ((), ())), preferred_element_type=F32)


def _dot_tn(a, b):
    return lax.dot_general(a, b, (((0,), (0,)), ((), ())), preferred_element_type=F32)


def _norm_body(x_ref, nw_ref, o_ref):
    o_ref[...] = _rms(x_ref[...], nw_ref[...]).astype(BF16)


def _norm(x, norm_w):
    t, d = x.shape
    tm = min(NORM_TM, t)
    return pl.pallas_call(
        _norm_body,
        grid=(t // tm,),
        in_specs=[pl.BlockSpec((tm, d), lambda i: (i, 0)), pl.BlockSpec((1, d), lambda i: (0, 0))],
        out_specs=pl.BlockSpec((tm, d), lambda i: (i, 0)),
        out_shape=jax.ShapeDtypeStruct((t, d), BF16),
        compiler_params=pltpu.CompilerParams(dimension_semantics=("parallel",), vmem_limit_bytes=VMEM_LIMIT),
        name="norm",
    )(x, norm_w.reshape(1, d))


def _ffn_up_body(h_ref, wg_ref, wu_ref, o_ref, w_scr):
    tf = wg_ref.shape[1]

    @pl.when(pl.program_id(1) == 0)
    def _():
        w_scr[:, :tf] = wg_ref[...].astype(BF16)
        w_scr[:, tf:] = wu_ref[...].astype(BF16)

    gu = _dot(h_ref[...], w_scr[...])
    g, u = gu[:, :tf], gu[:, tf:]
    o_ref[...] = (g * jax.nn.sigmoid(g) * u).astype(BF16)


def _ffn_up(h, wg, wu):
    t, d = h.shape
    f = wg.shape[1]
    tm, tf = min(FFN_UP_TM, t), FFN_TF
    return pl.pallas_call(
        _ffn_up_body,
        grid=(pl.cdiv(f, tf), t // tm),
        in_specs=[
            pl.BlockSpec((tm, d), lambda j, i: (i, 0)),
            pl.BlockSpec((d, tf), lambda j, i: (0, j)),
            pl.BlockSpec((d, tf), lambda j, i: (0, j)),
        ],
        out_specs=pl.BlockSpec((tm, tf), lambda j, i: (i, j)),
        out_shape=jax.ShapeDtypeStruct((t, f), BF16),
        scratch_shapes=[pltpu.VMEM((d, 2 * tf), BF16)],
        compiler_params=pltpu.CompilerParams(
            dimension_semantics=("arbitrary", "arbitrary"), vmem_limit_bytes=VMEM_LIMIT),
        name="ffn_up",
    )(h, wg, wu)


def _ffn_down_body(a_ref, wd_ref, x_ref, o_ref):
    o_ref[...] = x_ref[...] + 0.5 * _dot(a_ref[...], wd_ref[...])


def _ffn_down_norm_body(a_ref, wd_ref, x_ref, nw_ref, ws_ref, o_ref, hn_ref, side_ref):
    y = x_ref[...] + 0.5 * _dot(a_ref[...], wd_ref[...])
    o_ref[...] = y
    hn = _rms(y, nw_ref[...]).astype(BF16)
    hn_ref[...] = hn
    side_ref[...] = _dot(hn, ws_ref[...])


def _ffn_down(act, wd, x, next_norm_w=None, w_side=None):
    t, d = x.shape
    f = act.shape[1]
    tm = min(FFN_DOWN_TM, t)
    row = pl.BlockSpec((tm, d), lambda i: (i, 0))
    in_specs = [pl.BlockSpec((tm, f), lambda i: (i, 0)),
                pl.BlockSpec((f, d), lambda i: (0, 0), pipeline_mode=pl.Buffered(1)),
                row]
    params = pltpu.CompilerParams(dimension_semantics=("parallel",), vmem_limit_bytes=VMEM_LIMIT)
    if next_norm_w is None:
        assert w_side is None
        return pl.pallas_call(
            _ffn_down_body, grid=(t // tm,), in_specs=in_specs, out_specs=row,
            out_shape=jax.ShapeDtypeStruct((t, d), F32), compiler_params=params, name="ffn_down",
        )(act, wd, x)
    return pl.pallas_call(
        _ffn_down_norm_body, grid=(t // tm,),
        in_specs=in_specs + [pl.BlockSpec((1, d), lambda i: (0, 0)), pl.BlockSpec((d, LANES), lambda i: (0, 0))],
        out_specs=[row, row, pl.BlockSpec((tm, LANES), lambda i: (i, 0))],
        out_shape=[jax.ShapeDtypeStruct((t, d), F32), jax.ShapeDtypeStruct((t, d), BF16),
                   jax.ShapeDtypeStruct((t, LANES), F32)],
        compiler_params=params, name="ffn_down_norm",
    )(act, wd, x, next_norm_w.reshape(1, d), w_side)


def _proj_body(h_ref, wf_ref, wr_ref, p_ref, w_scr, *, n_first):
    j, i = pl.program_id(0), pl.program_id(1)

    @pl.when(jnp.logical_and(i == 0, j < n_first))
    def _():
        w_scr[...] = wf_ref[...].astype(BF16)

    @pl.when(jnp.logical_and(i == 0, j >= n_first))
    def _():
        w_scr[...] = wr_ref[...]

    r = _dot(h_ref[...], w_scr[...])
    for c in range(p_ref.shape[0]):
        p_ref[c] = r[:, c * LANES:(c + 1) * LANES].astype(BF16)


def _proj(h, w_first, n_first, w_rest):
    t, d = h.shape
    tm, tn = min(PROJ_TM, t), PROJ_TN
    n_rest = w_rest.shape[1] // tn
    assert w_rest.shape[1] % tn == 0 and w_first.shape[1] >= n_first * tn
    spb = tn // LANES
    return pl.pallas_call(
        functools.partial(_proj_body, n_first=n_first),
        grid=(n_first + n_rest, t // tm),
        in_specs=[
            pl.BlockSpec((tm, d), lambda j, i: (i, 0)),
            pl.BlockSpec((d, tn), lambda j, i: (0, jnp.minimum(j, n_first - 1))),
            pl.BlockSpec((d, tn), lambda j, i: (0, jnp.maximum(j - n_first, 0))),
        ],
        out_specs=pl.BlockSpec((spb, tm, LANES), lambda j, i: (j, i, 0)),
        out_shape=jax.ShapeDtypeStruct(((n_first + n_rest) * spb, t, LANES), BF16),
        scratch_shapes=[pltpu.VMEM((d, tn), BF16)],
        compiler_params=pltpu.CompilerParams(
            dimension_semantics=("arbitrary", "arbitrary"), vmem_limit_bytes=VMEM_LIMIT),
        name="proj",
    )(h, w_first, w_rest)


def _merge_body(og_ref, oa_ref, gg0_ref, gg1_ref, ga0_ref, ga1_ref, x_ref, wa_ref, wb_ref, wo_ref, nw_ref,
                o_ref, hn_ref):
    def slabs(*refs):
        return jnp.concatenate([ref[c] for ref in refs for c in range(ref.shape[0])], axis=-1)

    ya = _dot(slabs(og_ref), wa_ref[...])
    yb = _dot(slabs(oa_ref), wb_ref[...])
    m = (jax.nn.sigmoid(slabs(gg0_ref, gg1_ref).astype(F32)) * ya
         + jax.nn.sigmoid(slabs(ga0_ref, ga1_ref).astype(F32)) * yb)
    y = x_ref[...] + _dot(m.astype(BF16), wo_ref[...])
    o_ref[...] = y
    hn_ref[...] = _rms(y, nw_ref[...]).astype(BF16)


def _merge(og3, oa3, p3, gate_blk, x, wa, wb, wo, next_norm_w):
    t, d = x.shape
    tm = min(MERGE_TM, t)
    assert d == 2 * HEADS * LANES
    const = lambda i: (0, 0)
    row = pl.BlockSpec((tm, d), lambda i: (i, 0))
    slab = lambda blk: pl.BlockSpec((HEADS, tm, LANES), lambda i: (blk, i, 0))
    return pl.pallas_call(
        _merge_body,
        grid=(t // tm,),
        in_specs=[
            slab(0), slab(0),
            slab(gate_blk), slab(gate_blk + 1), slab(gate_blk + 2), slab(gate_blk + 3),
            row,
            pl.BlockSpec(wa.shape, const, pipeline_mode=pl.Buffered(1)),
            pl.BlockSpec(wb.shape, const, pipeline_mode=pl.Buffered(1)),
            pl.BlockSpec(wo.shape, const, pipeline_mode=pl.Buffered(1)),
            pl.BlockSpec((1, d), const),
        ],
        out_specs=[row, row],
        out_shape=[jax.ShapeDtypeStruct((t, d), F32), jax.ShapeDtypeStruct((t, d), BF16)],
        compiler_params=pltpu.CompilerParams(
            dimension_semantics=("parallel",), vmem_limit_bytes=VMEM_LIMIT),
        name="merge",
    )(og3, oa3, p3, p3, p3, p3, x, wa, wb, wo, next_norm_w.reshape(1, d))


def _att_body(q_ref, k_ref, v_ref, qw_ref, kw_ref, bvec_ref, o_ref, kbuf, vbuf, bias_scr):
    blk = pl.program_id(0)
    rb = q_ref.shape[1]

    @pl.when(blk == 0)
    def _():
        kbuf[:, :rb, :] = jnp.zeros((HEADS, rb, LANES), BF16)
        vbuf[:, :rb, :] = jnp.zeros((HEADS, rb, LANES), BF16)
        qc = lax.broadcasted_iota(jnp.int32, (ATT_QG, ATT_WIN), 0) // CHUNK
        kc = lax.broadcasted_iota(jnp.int32, (ATT_QG, ATT_WIN), 1) // CHUNK
        off = kc - qc
        band = jnp.where(off >= 0, jnp.where(off <= ATT_LEFT_CHUNKS, 0.0, NEG), NEG).astype(F32)
        for h in range(HEADS):
            tbl = jnp.broadcast_to(bvec_ref[h], (ATT_QG, ATT_TBL))
            tbl = pltpu.roll(tbl, 0, 1, stride=1, stride_axis=0)
            bias_scr[h] = tbl[:, :ATT_WIN] * LOG2E + band

    first_neg = jnp.where(blk == 0, NEG, 0.0).astype(F32)
    col = lax.broadcasted_iota(jnp.int32, (1, ATT_WIN), 1)
    qw = qw_ref[...] * (HEAD_DIM ** -0.5 * LOG2E)
    kw = kw_ref[...]
    groups = [(dh, g * ATT_QG) for dh in range(ATT_HEADS_PER_ITER) for g in range(rb // ATT_QG)]

    def heads(hi, carry):
        h0 = hi * ATT_HEADS_PER_ITER
        for dh in range(ATT_HEADS_PER_ITER):
            kbuf[h0 + dh, rb:, :] = _rms(k_ref[h0 + dh].astype(F32), kw).astype(BF16)
            vbuf[h0 + dh, rb:, :] = v_ref[h0 + dh]
        qn = [_rms(q_ref[h0 + dh, r0:r0 + ATT_QG, :].astype(F32), qw).astype(BF16) for dh, r0 in groups]
        s = [_dot_nt(qn[n], kbuf[h0 + dh, r0:r0 + ATT_WIN, :]) for n, (dh, r0) in enumerate(groups)]
        s = [s[n] + bias_scr[h0 + dh] + jnp.where(col < rb - r0, first_neg, 0.0)
             for n, (dh, r0) in enumerate(groups)]
        p = [jnp.exp2(sn - jnp.max(sn, axis=-1, keepdims=True)) for sn in s]
        l = [jnp.sum(pn, axis=-1, keepdims=True) for pn in p]
        o = [_dot(p[n].astype(BF16), vbuf[h0 + dh, r0:r0 + ATT_WIN, :]) for n, (dh, r0) in enumerate(groups)]
        for n, (dh, r0) in enumerate(groups):
            o_ref[h0 + dh, r0:r0 + ATT_QG, :] = (o[n] / l[n]).astype(BF16)
        return carry

    lax.fori_loop(0, HEADS // ATT_HEADS_PER_ITER, heads, 0)
    kbuf[:, :rb, :] = kbuf[:, rb:, :]
    vbuf[:, :rb, :] = vbuf[:, rb:, :]


def _att_bias_table(rel_bias):
    tbl = rel_bias.astype(F32)
    n_tbl = tbl.shape[1]
    far = ATT_LEFT_CHUNKS * CHUNK - REL_MAX
    rep = lambda c, n: jnp.broadcast_to(tbl[:, c:c + 1], (tbl.shape[0], n))
    vec = jnp.concatenate([rep(n_tbl - 1, far), tbl[:, ::-1],
                           rep(0, ATT_TBL - far - n_tbl - (ATT_QG - 1)), rep(n_tbl - 1, ATT_QG - 1)], axis=1)
    return vec[:, None, :]


def _att(p3, q_norm_w, k_norm_w, rel_bias, q_blk, k_blk, v_blk):
    t = p3.shape[1]
    rb = SEQ_BLOCK
    assert rb == ATT_LEFT_CHUNKS * CHUNK and t % rb == 0
    bvec = _att_bias_table(rel_bias)
    return pl.pallas_call(
        _att_body,
        grid=(t // rb,),
        in_specs=[
            pl.BlockSpec((HEADS, rb, LANES), lambda b: (q_blk, b, 0)),
            pl.BlockSpec((HEADS, rb, LANES), lambda b: (k_blk, b, 0)),
            pl.BlockSpec((HEADS, rb, LANES), lambda b: (v_blk, b, 0)),
            pl.BlockSpec((1, LANES), lambda b: (0, 0)),
            pl.BlockSpec((1, LANES), lambda b: (0, 0)),
            pl.BlockSpec(bvec.shape, lambda b: (0, 0, 0)),
        ],
        out_specs=pl.BlockSpec((HEADS, rb, LANES), lambda b: (0, b, 0)),
        out_shape=jax.ShapeDtypeStruct((HEADS, t, LANES), BF16),
        scratch_shapes=[pltpu.VMEM((HEADS, 2 * rb, LANES), BF16),
                        pltpu.VMEM((HEADS, 2 * rb, LANES), BF16),
                        pltpu.VMEM((HEADS, ATT_QG, ATT_WIN), F32)],
        compiler_params=pltpu.CompilerParams(
            dimension_semantics=("arbitrary",), vmem_limit_bytes=VMEM_LIMIT),
        name="att",
    )(p3, p3, p3, q_norm_w.reshape(1, LANES), k_norm_w.reshape(1, LANES), bvec)


def _split2(x):
    hi = x.astype(BF16)
    return hi, (x - hi.astype(F32)).astype(BF16)


def _gdn_body(q_ref, k_ref, v_ref, z_ref, ab_ref, cw_ref, alog_ref, dtb_ref, onw_ref, ls_ref, mx_ref,
              o_ref,
              s_scr, xp_scr, qn_scr, kn_scr, vn_scr, gb_scr, bb_scr, u_scr, wq_scr, aqk_scr, kw_scr,
              bm_scr, cd_scr, oacc_scr):
    blk = pl.program_id(0)
    rows = q_ref.shape[1]
    npair = rows // PAIR
    hs = range(HEADS)

    @pl.when(blk == 0)
    def _():
        s_scr[...] = jnp.zeros_like(s_scr)
        xp_scr[:, 0:8, :] = jnp.zeros((3 * HEADS, 8, LANES), F32)

    ab = ab_ref[...]
    xa = ab + dtb_ref[...]
    softplus = jnp.maximum(xa, 0.0) + jnp.log(1.0 + jnp.exp(-jnp.abs(xa)))
    g_all = -jnp.exp(alog_ref[...]) * softplus
    beta_all = jax.nn.sigmoid(ab)

    for h in hs:
        gb_scr[h] = jnp.broadcast_to(g_all[:, h:h + 1], (rows, LANES))
        bb_scr[h] = jnp.broadcast_to(beta_all[:, HEADS + h:HEADS + h + 1], (rows, LANES))

        outs = []
        for part, ref in enumerate((q_ref, k_ref, v_ref)):
            s = part * HEADS + h
            xp_scr[s, 8:8 + rows, :] = ref[h].astype(F32)
            w = cw_ref[s]
            y = w[0:1] * xp_scr[s, 5:5 + rows, :]
            for tap in range(1, GDN_CONV):
                y = y + w[tap:tap + 1] * xp_scr[s, 5 + tap:5 + tap + rows, :]
            outs.append(y * jax.nn.sigmoid(y))
        qc, kc, vc = outs
        qn_scr[h] = qc * lax.rsqrt(jnp.sum(qc * qc, axis=-1, keepdims=True) + NORM_EPS) * (HEAD_DIM ** -0.5)
        kn_scr[h] = kc * lax.rsqrt(jnp.sum(kc * kc, axis=-1, keepdims=True) + NORM_EPS)
        vn_scr[h] = vc

    for s in range(3 * HEADS):
        xp_scr[s, 0:8, :] = xp_scr[s, rows:rows + 8, :]

    ls = ls_ref[...]
    mx = mx_ref[...]
    ri = lax.broadcasted_iota(jnp.int32, (PAIR, PAIR), 0)
    ci = lax.broadcasted_iota(jnp.int32, (PAIR, PAIR), 1)
    same = (ri // CHUNK) == (ci // CHUNK)
    lower_f = jnp.where(same, jnp.where(ri >= ci, 1.0, 0.0), 0.0).astype(F32)
    strict_f = jnp.where(same, jnp.where(ri > ci, 1.0, 0.0), 0.0).astype(F32)
    eye = jnp.where(ri == ci, 1.0, 0.0).astype(F32)

    def pre(p, carry):
        sl = pl.ds(pl.multiple_of(p * PAIR, PAIR), PAIR)
        k = [kn_scr[h, sl, :] for h in hs]
        q = [qn_scr[h, sl, :] for h in hs]
        v = [vn_scr[h, sl, :] for h in hs]
        gb = [gb_scr[h, sl, :] for h in hs]
        bb = [bb_scr[h, sl, :] for h in hs]
        parts = [_split2(jnp.concatenate([gb[h], gb[h]], axis=1) * mx) for h in hs]
        gm = [_dot(ls, pt[0]) + _dot(ls, pt[1]) for pt in parts]
        decay = [jnp.exp(g[:, :PAIR]) for g in gm]
        gi = [g[:, PAIR:] for g in gm]
        glast = [jnp.concatenate([jnp.broadcast_to(g[c * CHUNK - 1:c * CHUNK, :], (CHUNK, LANES))
                                  for c in (1, 2)], axis=0) for g in gi]
        eg = [jnp.exp(g) for g in gi]
        er = [jnp.exp(glast[h] - gi[h]) for h in hs]
        kb = [k[h] * bb[h] for h in hs]
        kk = [_dot_nt(jnp.concatenate([kb[h], q[h]], axis=0).astype(BF16), k[h].astype(BF16)) for h in hs]
        n = [-(kk[h][:PAIR] * decay[h] * strict_f) for h in hs]
        aqk = [kk[h][PAIR:] * decay[h] * lower_f for h in hs]
        inv = [eye + n[h] for h in hs]
        nb = [n[h].astype(BF16) for h in hs]
        nb = [_dot(nb[h], nb[h]).astype(BF16) for h in hs]
        for it in range(5):
            if it < 4:
                m = [_dot(jnp.concatenate([inv[h].astype(BF16), nb[h]], axis=0), nb[h]) for h in hs]
                inv = [inv[h] + m[h][:PAIR] for h in hs]
                nb = [m[h][PAIR:].astype(BF16) for h in hs]
            else:
                inv = [inv[h] + _dot(inv[h].astype(BF16), nb[h]) for h in hs]
        uw = [_dot(inv[h].astype(BF16),
                   jnp.concatenate([v[h] * bb[h], kb[h] * eg[h]], axis=1).astype(BF16)) for h in hs]
        kd = [(k[h] * er[h]).astype(BF16) for h in hs]
        qg = [q[h] * eg[h] for h in hs]
        for cc in range(2):
            c = 2 * p + cc
            rs = slice(cc * CHUNK, (cc + 1) * CHUNK)
            kwb = [_dot_tn(kd[h][rs], jnp.concatenate([uw[h][rs, LANES:], uw[h][rs, :LANES]], axis=1).astype(BF16))
                   for h in hs]
            for h in hs:
                wq_scr[h, c] = jnp.concatenate([uw[h][rs, LANES:], qg[h][rs]], axis=0).astype(BF16)
                kw_scr[h, c] = kwb[h][:, :LANES].astype(BF16)
                bm_scr[h, c] = kwb[h][:, LANES:]
                cd_scr[h, c] = jnp.broadcast_to(eg[h][(cc + 1) * CHUNK - 1:(cc + 1) * CHUNK, :], (8, LANES))
        for h in hs:
            u_scr[h, sl, :] = uw[h][:, :LANES]
            aqk_scr[h, p] = aqk[h].astype(BF16)
        return carry

    lax.fori_loop(0, npair, pre, 0)

    def chain(p, carry):
        sl = pl.ds(pl.multiple_of(p * PAIR, PAIR), PAIR)
        c0, c1 = 2 * p, 2 * p + 1
        s0 = [s_scr[h] for h in hs]
        sb0 = [s0[h].astype(BF16) for h in hs]
        s1 = [s0[h] * cd_scr[h, c0][0:1, :] + (bm_scr[h, c0] - _dot(kw_scr[h, c0], sb0[h])) for h in hs]
        sb1 = [s1[h].astype(BF16) for h in hs]
        s2 = [s1[h] * cd_scr[h, c1][0:1, :] + (bm_scr[h, c1] - _dot(kw_scr[h, c1], sb1[h])) for h in hs]
        for h in hs:
            s_scr[h] = s2[h]
        a0 = [_dot(wq_scr[h, c0], sb0[h]) for h in hs]
        a1 = [_dot(wq_scr[h, c1], sb1[h]) for h in hs]
        vnew = [(u_scr[h, sl, :] - jnp.concatenate([a0[h][:CHUNK], a1[h][:CHUNK]], axis=0)).astype(BF16)
                for h in hs]
        for h in hs:
            oacc_scr[h, sl, :] = (jnp.concatenate([a0[h][CHUNK:], a1[h][CHUNK:]], axis=0)
                                  + _dot(aqk_scr[h, p], vnew[h]))
        return carry

    lax.fori_loop(0, npair, chain, 0)

    onw = onw_ref[...]
    for h in hs:
        z = z_ref[h].astype(F32)
        o_ref[h] = (_rms(oacc_scr[h], onw) * (z * jax.nn.sigmoid(z))).astype(BF16)


def _gdn(p3, ab, conv_w, a_log, dt_bias, out_norm_w, q_blk):
    t = p3.shape[1]
    rows = min(SEQ_BLOCK, t)
    nchunk, npair = rows // CHUNK, rows // PAIR
    cw = conv_w.astype(F32).reshape(GDN_CONV, 3 * HEADS, LANES).transpose(1, 0, 2)
    pad = lambda v: jnp.zeros((1, LANES), F32).at[0, :HEADS].set(v.astype(F32))
    ti = jnp.arange(PAIR)
    same = (ti[:, None] // CHUNK) == (ti[None, :] // CHUNK)
    le = same & (ti[None, :] <= ti[:, None])
    gt = same & (ti[None, :] > ti[:, None])
    ls = le.astype(BF16)
    mx = jnp.concatenate([gt.T.astype(F32), jnp.ones((PAIR, LANES), F32)], axis=1)
    slab = lambda off: pl.BlockSpec((HEADS, rows, LANES), lambda b: (q_blk + off, b, 0))
    const2 = lambda b: (0, 0)
    hr = (HEADS, rows, LANES)
    return pl.pallas_call(
        _gdn_body,
        grid=(t // rows,),
        in_specs=[
            slab(0), slab(1), slab(2), slab(3),
            pl.BlockSpec((rows, LANES), lambda b: (b, 0)),
            pl.BlockSpec(cw.shape, lambda b: (0, 0, 0)),
            pl.BlockSpec((1, LANES), const2),
            pl.BlockSpec((1, LANES), const2),
            pl.BlockSpec((1, LANES), const2),
            pl.BlockSpec(ls.shape, const2),
            pl.BlockSpec(mx.shape, const2),
        ],
        out_specs=pl.BlockSpec(hr, lambda b: (0, b, 0)),
        out_shape=jax.ShapeDtypeStruct((HEADS, t, LANES), BF16),
        scratch_shapes=[
            pltpu.VMEM((HEADS, HEAD_DIM, HEAD_DIM), F32),
            pltpu.VMEM((3 * HEADS, rows + 8, LANES), F32),
            pltpu.VMEM(hr, F32), pltpu.VMEM(hr, F32), pltpu.VMEM(hr, F32),
            pltpu.VMEM(hr, F32), pltpu.VMEM(hr, F32),
            pltpu.VMEM(hr, F32),
            pltpu.VMEM((HEADS, nchunk, 2 * CHUNK, LANES), BF16),
            pltpu.VMEM((HEADS, npair, PAIR, PAIR), BF16),
            pltpu.VMEM((HEADS, nchunk, HEAD_DIM, HEAD_DIM), BF16),
            pltpu.VMEM((HEADS, nchunk, HEAD_DIM, HEAD_DIM), F32),
            pltpu.VMEM((HEADS, nchunk, 8, LANES), F32),
            pltpu.VMEM(hr, F32),
        ],
        compiler_params=pltpu.CompilerParams(
            dimension_semantics=("arbitrary",), vmem_limit_bytes=VMEM_LIMIT),
        name="gdn",
    )(p3, p3, p3, p3, ab, cw, pad(a_log), pad(dt_bias), out_norm_w.astype(F32).reshape(1, LANES), ls, mx)


def _pad_to(w, axis, mult):
    n = w.shape[axis]
    extra = (-n) % mult
    if extra == 0:
        return w
    widths = [(0, 0)] * w.ndim
    widths[axis] = (0, extra)
    return jnp.pad(w, widths)


@jax.jit
def _forward(x, ffn1_norm, ffn1_w_gate, ffn1_w_up, ffn1_w_down, mix_norm, w_in, gdn_conv,
             gdn_A_log, gdn_dt_bias, gdn_out_norm, att_q_norm, att_k_norm, att_rel_bias,
             w_branch_gdn, w_branch_att, w_out, ffn2_norm, ffn2_w_gate, ffn2_w_up, ffn2_w_down):
    b, t, d = x.shape
    gw = HEADS * HEAD_DIM
    outs = []
    for bi in range(b):
        xb = x[bi]
        depth = ffn1_norm.shape[0]
        for l in range(depth):
            wi = w_in[l]
            o_ab = 4 * gw
            o_att = o_ab + 2 * HEADS
            w_ab = _pad_to(wi[:, o_ab:o_att].astype(BF16), 1, LANES)
            w_rest = wi[:, o_att:].astype(BF16)

            act = _ffn_up(_norm(xb, ffn1_norm[l]), ffn1_w_gate[l], ffn1_w_up[l])
            xb, hn, ab = _ffn_down(act, ffn1_w_down[l].astype(BF16), xb, mix_norm[l], w_ab)
            p3 = _proj(hn, wi, o_ab // PROJ_TN, w_rest)
            og3 = _gdn(p3, ab, gdn_conv[l], gdn_A_log[l], gdn_dt_bias[l], gdn_out_norm[l], 0)
            oa3 = _att(p3, att_q_norm[l], att_k_norm[l], att_rel_bias[l], 4, 5, 6)
            xb, hn = _merge(og3, oa3, p3, 7, xb, w_branch_gdn[l].astype(BF16), w_branch_att[l].astype(BF16),
                            w_out[l].astype(BF16), ffn2_norm[l])

            act = _ffn_up(hn, ffn2_w_gate[l], ffn2_w_up[l])
            xb = _ffn_down(act, ffn2_w_down[l].astype(BF16), xb)
        outs.append(xb)
    return jnp.stack(outs, axis=0)


def kernel(x, ffn1_norm, ffn1_w_gate, ffn1_w_up, ffn1_w_down, mix_norm, w_in, gdn_conv, gdn_A_log, gdn_dt_bias, gdn_out_norm, att_q_norm, att_k_norm, att_rel_bias, w_branch_gdn, w_branch_att, w_out, ffn2_norm, ffn2_w_gate, ffn2_w_up, ffn2_w_down):
    return _forward(x, ffn1_norm, ffn1_w_gate, ffn1_w_up, ffn1_w_down, mix_norm, w_in, gdn_conv,
                    gdn_A_log, gdn_dt_bias, gdn_out_norm, att_q_norm, att_k_norm, att_rel_bias,
                    w_branch_gdn, w_branch_att, w_out, ffn2_norm, ffn2_w_gate, ffn2_w_up, ffn2_w_down)
```

```python
import functools
import math

import jax
import jax.numpy as jnp
from jax import lax
from jax.experimental import pallas as pl
from jax.experimental.pallas import tpu as pltpu

F32 = jnp.float32
BF16 = jnp.bfloat16

NORM_EPS = 1e-6
CHUNK = 64
HEADS = 8
HEAD_DIM = 128
LANES = 128
GDN_CONV = 4
ATT_LEFT_CHUNKS = 8
REL_MAX = 256
NEG = -1e30

VMEM_LIMIT = 56 * 1024 * 1024

LOG2E = math.log2(math.e)

NORM_TM = 512
FFN_UP_TM = 2048
FFN_TF = 512
FFN_DOWN_TM = 256
PROJ_TM = 1024
PROJ_TN = 1024
PROJ_SHIFT_ROWS = 256
ATT_HEADS_PER_ITER = 2
MERGE_TM = 512
SEQ_BLOCK = 512
ATT_QG = 256
ATT_WIN = ATT_QG + ATT_LEFT_CHUNKS * CHUNK
ATT_TBL = 1024
PAIR = 2 * CHUNK


def _rms(x, w):
    return x * lax.rsqrt(jnp.mean(x * x, axis=-1, keepdims=True) + NORM_EPS) * w


def _dot(a, b):
    return jnp.dot(a, b, preferred_element_type=F32)


def _dot_nt(a, b):
    return lax.dot_general(a, b, (((1,), (1,)), ((), ())), preferred_element_type=F32)


def _dot_tn(a, b):
    return lax.dot_general(a, b, (((0,), (0,)), ((), ())), preferred_element_type=F32)


def _norm_body(x_ref, nw_ref, o_ref):
    o_ref[...] = _rms(x_ref[...], nw_ref[...]).astype(BF16)


def _norm(x, norm_w):
    t, d = x.shape
    tm = min(NORM_TM, t)
    return pl.pallas_call(
        _norm_body,
        grid=(t // tm,),
        in_specs=[pl.BlockSpec((tm, d), lambda i: (i, 0)), pl.BlockSpec((1, d), lambda i: (0, 0))],
        out_specs=pl.BlockSpec((tm, d), lambda i: (i, 0)),
        out_shape=jax.ShapeDtypeStruct((t, d), BF16),
        compiler_params=pltpu.CompilerParams(dimension_semantics=("parallel",), vmem_limit_bytes=VMEM_LIMIT),
        name="norm",
    )(x, norm_w.reshape(1, d))


def _ffn_up_body(h_ref, wg_ref, wu_ref, o_ref, w_scr):
    tf = wg_ref.shape[1]

    @pl.when(pl.program_id(1) == 0)
    def _():
        w_scr[:, :tf] = wg_ref[...].astype(BF16)
        w_scr[:, tf:] = wu_ref[...].astype(BF16)

    gu = _dot(h_ref[...], w_scr[...])
    g, u = gu[:, :tf], gu[:, tf:]
    o_ref[...] = (g * jax.nn.sigmoid(g) * u).astype(BF16)


def _ffn_up(h, wg, wu):
    t, d = h.shape
    f = wg.shape[1]
    tm, tf = min(FFN_UP_TM, t), FFN_TF
    return pl.pallas_call(
        _ffn_up_body,
        grid=(pl.cdiv(f, tf), t // tm),
        in_specs=[
            pl.BlockSpec((tm, d), lambda j, i: (i, 0)),
            pl.BlockSpec((d, tf), lambda j, i: (0, j)),
            pl.BlockSpec((d, tf), lambda j, i: (0, j)),
        ],
        out_specs=pl.BlockSpec((tm, tf), lambda j, i: (i, j)),
        out_shape=jax.ShapeDtypeStruct((t, f), BF16),
        scratch_shapes=[pltpu.VMEM((d, 2 * tf), BF16)],
        compiler_params=pltpu.CompilerParams(
            dimension_semantics=("arbitrary", "arbitrary"), vmem_limit_bytes=VMEM_LIMIT),
        name="ffn_up",
    )(h, wg, wu)


def _ffn_down_body(a_ref, wd_ref, x_ref, o_ref):
    o_ref[...] = x_ref[...] + 0.5 * _dot(a_ref[...], wd_ref[...])


def _ffn_down_norm_body(a_ref, wd_ref, x_ref, nw_ref, ws_ref, o_ref, hn_ref, side_ref):
    y = x_ref[...] + 0.5 * _dot(a_ref[...], wd_ref[...])
    o_ref[...] = y
    hn = _rms(y, nw_ref[...]).astype(BF16)
    hn_ref[...] = hn
    side_ref[...] = _dot(hn, ws_ref[...])


def _ffn_down(act, wd, x, next_norm_w=None, w_side=None):
    t, d = x.shape
    f = act.shape[1]
    tm = min(FFN_DOWN_TM, t)
    row = pl.BlockSpec((tm, d), lambda i: (i, 0))
    in_specs = [pl.BlockSpec((tm, f), lambda i: (i, 0)),
                pl.BlockSpec((f, d), lambda i: (0, 0), pipeline_mode=pl.Buffered(1)),
                row]
    params = pltpu.CompilerParams(dimension_semantics=("parallel",), vmem_limit_bytes=VMEM_LIMIT)
    if next_norm_w is None:
        assert w_side is None
        return pl.pallas_call(
            _ffn_down_body, grid=(t // tm,), in_specs=in_specs, out_specs=row,
            out_shape=jax.ShapeDtypeStruct((t, d), F32), compiler_params=params, name="ffn_down",
        )(act, wd, x)
    return pl.pallas_call(
        _ffn_down_norm_body, grid=(t // tm,),
        in_specs=in_specs + [pl.BlockSpec((1, d), lambda i: (0, 0)), pl.BlockSpec((d, LANES), lambda i: (0, 0))],
        out_specs=[row, row, pl.BlockSpec((tm, LANES), lambda i: (i, 0))],
        out_shape=[jax.ShapeDtypeStruct((t, d), F32), jax.ShapeDtypeStruct((t, d), BF16),
                   jax.ShapeDtypeStruct((t, LANES), F32)],
        compiler_params=params, name="ffn_down_norm",
    )(act, wd, x, next_norm_w.reshape(1, d), w_side)


def _proj_body(h_ref, wf_ref, wx_ref, p_ref, w_scr, *, n_aligned, shift):
    j, i = pl.program_id(0), pl.program_id(1)
    d, tn = wf_ref.shape

    @pl.when(jnp.logical_and(i == 0, j < n_aligned))
    def _():
        w_scr[...] = wf_ref[...].astype(BF16)

    @pl.when(jnp.logical_and(i == 0, j >= n_aligned))
    def _():
        for r in range(0, d, PROJ_SHIFT_ROWS):
            w = jnp.concatenate([wf_ref[r:r + PROJ_SHIFT_ROWS, :], wx_ref[r:r + PROJ_SHIFT_ROWS, :]], axis=1)
            w = pltpu.roll(w, tn + LANES - shift, 1)
            w_scr[r:r + PROJ_SHIFT_ROWS, :] = w[:, :tn].astype(BF16)

    r = _dot(h_ref[...], w_scr[...])
    for c in range(p_ref.shape[0]):
        p_ref[c] = r[:, c * LANES:(c + 1) * LANES].astype(BF16)


def _proj(h, w, n_aligned, shift, n_tiles):
    t, d = h.shape
    tm, tn = min(PROJ_TM, t), PROJ_TN
    assert 0 < shift < LANES and w.shape[1] == n_tiles * tn + shift
    spb = tn // LANES
    return pl.pallas_call(
        functools.partial(_proj_body, n_aligned=n_aligned, shift=shift),
        grid=(n_tiles, t // tm),
        in_specs=[
            pl.BlockSpec((tm, d), lambda j, i: (i, 0)),
            pl.BlockSpec((d, tn), lambda j, i: (0, j)),
            pl.BlockSpec((d, LANES), lambda j, i: (0, (j + 1) * spb)),
        ],
        out_specs=pl.BlockSpec((spb, tm, LANES), lambda j, i: (j, i, 0)),
        out_shape=jax.ShapeDtypeStruct((n_tiles * spb, t, LANES), BF16),
        scratch_shapes=[pltpu.VMEM((d, tn), BF16)],
        compiler_params=pltpu.CompilerParams(
            dimension_semantics=("arbitrary", "arbitrary"), vmem_limit_bytes=VMEM_LIMIT),
        name="proj",
    )(h, w, w)


def _merge_body(og_ref, oa_ref, gg0_ref, gg1_ref, ga0_ref, ga1_ref, x_ref, wa_ref, wb_ref, wo_ref, nw_ref,
                o_ref, hn_ref):
    def slabs(*refs):
        return jnp.concatenate([ref[c] for ref in refs for c in range(ref.shape[0])], axis=-1)

    ya = _dot(slabs(og_ref), wa_ref[...])
    yb = _dot(slabs(oa_ref), wb_ref[...])
    m = (jax.nn.sigmoid(slabs(gg0_ref, gg1_ref).astype(F32)) * ya
         + jax.nn.sigmoid(slabs(ga0_ref, ga1_ref).astype(F32)) * yb)
    y = x_ref[...] + _dot(m.astype(BF16), wo_ref[...])
    o_ref[...] = y
    hn_ref[...] = _rms(y, nw_ref[...]).astype(BF16)


def _merge(og3, oa3, p3, gate_blk, x, wa, wb, wo, next_norm_w):
    t, d = x.shape
    tm = min(MERGE_TM, t)
    assert d == 2 * HEADS * LANES
    const = lambda i: (0, 0)
    row = pl.BlockSpec((tm, d), lambda i: (i, 0))
    slab = lambda blk: pl.BlockSpec((HEADS, tm, LANES), lambda i: (blk, i, 0))
    return pl.pallas_call(
        _merge_body,
        grid=(t // tm,),
        in_specs=[
            slab(0), slab(0),
            slab(gate_blk), slab(gate_blk + 1), slab(gate_blk + 2), slab(gate_blk + 3),
            row,
            pl.BlockSpec(wa.shape, const, pipeline_mode=pl.Buffered(1)),
            pl.BlockSpec(wb.shape, const, pipeline_mode=pl.Buffered(1)),
            pl.BlockSpec(wo.shape, const, pipeline_mode=pl.Buffered(1)),
            pl.BlockSpec((1, d), const),
        ],
        out_specs=[row, row],
        out_shape=[jax.ShapeDtypeStruct((t, d), F32), jax.ShapeDtypeStruct((t, d), BF16)],
        compiler_params=pltpu.CompilerParams(
            dimension_semantics=("parallel",), vmem_limit_bytes=VMEM_LIMIT),
        name="merge",
    )(og3, oa3, p3, p3, p3, p3, x, wa, wb, wo, next_norm_w.reshape(1, d))


def _att_body(q_ref, k_ref, v_ref, qw_ref, kw_ref, bvec_ref, o_ref, kbuf, vbuf, bias_scr):
    blk = pl.program_id(0)
    rb = q_ref.shape[1]

    @pl.when(blk == 0)
    def _():
        kbuf[:, :rb, :] = jnp.zeros((HEADS, rb, LANES), BF16)
        vbuf[:, :rb, :] = jnp.zeros((HEADS, rb, LANES), BF16)
        qc = lax.broadcasted_iota(jnp.int32, (ATT_QG, ATT_WIN), 0) // CHUNK
        kc = lax.broadcasted_iota(jnp.int32, (ATT_QG, ATT_WIN), 1) // CHUNK
        off = kc - qc
        band = jnp.where(off >= 0, jnp.where(off <= ATT_LEFT_CHUNKS, 0.0, NEG), NEG).astype(F32)
        for h in range(HEADS):
            tbl = jnp.broadcast_to(bvec_ref[h], (ATT_QG, ATT_TBL))
            tbl = pltpu.roll(tbl, 0, 1, stride=1, stride_axis=0)
            bias_scr[h] = tbl[:, :ATT_WIN] * LOG2E + band

    first_neg = jnp.where(blk == 0, NEG, 0.0).astype(F32)
    col = lax.broadcasted_iota(jnp.int32, (1, ATT_WIN), 1)
    qw = qw_ref[...] * (HEAD_DIM ** -0.5 * LOG2E)
    kw = kw_ref[...]
    groups = [(dh, g * ATT_QG) for dh in range(ATT_HEADS_PER_ITER) for g in range(rb // ATT_QG)]

    def heads(hi, carry):
        h0 = hi * ATT_HEADS_PER_ITER
        for dh in range(ATT_HEADS_PER_ITER):
            kbuf[h0 + dh, rb:, :] = _rms(k_ref[h0 + dh].astype(F32), kw).astype(BF16)
            vbuf[h0 + dh, rb:, :] = v_ref[h0 + dh]
        qn = [_rms(q_ref[h0 + dh, r0:r0 + ATT_QG, :].astype(F32), qw).astype(BF16) for dh, r0 in groups]
        s = [_dot_nt(qn[n], kbuf[h0 + dh, r0:r0 + ATT_WIN, :]) for n, (dh, r0) in enumerate(groups)]
        s = [s[n] + bias_scr[h0 + dh] + jnp.where(col < rb - r0, first_neg, 0.0)
             for n, (dh, r0) in enumerate(groups)]
        p = [jnp.exp2(sn - jnp.max(sn, axis=-1, keepdims=True)) for sn in s]
        l = [jnp.sum(pn, axis=-1, keepdims=True) for pn in p]
        o = [_dot(p[n].astype(BF16), vbuf[h0 + dh, r0:r0 + ATT_WIN, :]) for n, (dh, r0) in enumerate(groups)]
        for n, (dh, r0) in enumerate(groups):
            o_ref[h0 + dh, r0:r0 + ATT_QG, :] = (o[n] / l[n]).astype(BF16)
        return carry

    lax.fori_loop(0, HEADS // ATT_HEADS_PER_ITER, heads, 0)
    kbuf[:, :rb, :] = kbuf[:, rb:, :]
    vbuf[:, :rb, :] = vbuf[:, rb:, :]


def _att_bias_table(rel_bias):
    tbl = rel_bias.astype(F32)
    n_tbl = tbl.shape[1]
    far = ATT_LEFT_CHUNKS * CHUNK - REL_MAX
    rep = lambda c, n: jnp.broadcast_to(tbl[:, c:c + 1], (tbl.shape[0], n))
    vec = jnp.concatenate([rep(n_tbl - 1, far), tbl[:, ::-1],
                           rep(0, ATT_TBL - far - n_tbl - (ATT_QG - 1)), rep(n_tbl - 1, ATT_QG - 1)], axis=1)
    return vec[:, None, :]


def _att(p3, q_norm_w, k_norm_w, rel_bias, q_blk, k_blk, v_blk):
    t = p3.shape[1]
    rb = SEQ_BLOCK
    assert rb == ATT_LEFT_CHUNKS * CHUNK and t % rb == 0
    bvec = _att_bias_table(rel_bias)
    return pl.pallas_call(
        _att_body,
        grid=(t // rb,),
        in_specs=[
            pl.BlockSpec((HEADS, rb, LANES), lambda b: (q_blk, b, 0)),
            pl.BlockSpec((HEADS, rb, LANES), lambda b: (k_blk, b, 0)),
            pl.BlockSpec((HEADS, rb, LANES), lambda b: (v_blk, b, 0)),
            pl.BlockSpec((1, LANES), lambda b: (0, 0)),
            pl.BlockSpec((1, LANES), lambda b: (0, 0)),
            pl.BlockSpec(bvec.shape, lambda b: (0, 0, 0)),
        ],
        out_specs=pl.BlockSpec((HEADS, rb, LANES), lambda b: (0, b, 0)),
        out_shape=jax.ShapeDtypeStruct((HEADS, t, LANES), BF16),
        scratch_shapes=[pltpu.VMEM((HEADS, 2 * rb, LANES), BF16),
                        pltpu.VMEM((HEADS, 2 * rb, LANES), BF16),
                        pltpu.VMEM((HEADS, ATT_QG, ATT_WIN), F32)],
        compiler_params=pltpu.CompilerParams(
            dimension_semantics=("arbitrary",), vmem_limit_bytes=VMEM_LIMIT),
        name="att",
    )(p3, p3, p3, q_norm_w.reshape(1, LANES), k_norm_w.reshape(1, LANES), bvec)


def _split2(x):
    hi = x.astype(BF16)
    return hi, (x - hi.astype(F32)).astype(BF16)


def _gdn_body(q_ref, k_ref, v_ref, z_ref, ab_ref, cw_ref, alog_ref, dtb_ref, onw_ref, ls_ref, mx_ref,
              o_ref,
              s_scr, xp_scr, qn_scr, kn_scr, vn_scr, gb_scr, bb_scr, u_scr, wq_scr, aqk_scr, kw_scr,
              bm_scr, cd_scr):
    blk = pl.program_id(0)
    rows = q_ref.shape[1]
    npair = rows // PAIR
    hs = range(HEADS)

    @pl.when(blk == 0)
    def _():
        s_scr[...] = jnp.zeros_like(s_scr)
        xp_scr[:, 0:8, :] = jnp.zeros((3 * HEADS, 8, LANES), F32)

    ls = ls_ref[...]
    mx = mx_ref[...]
    ri = lax.broadcasted_iota(jnp.int32, (PAIR, PAIR), 0)
    ci = lax.broadcasted_iota(jnp.int32, (PAIR, PAIR), 1)
    same = (ri // CHUNK) == (ci // CHUNK)
    lower_f = jnp.where(same, jnp.where(ri >= ci, 1.0, 0.0), 0.0).astype(F32)
    strict_f = jnp.where(same, jnp.where(ri > ci, 1.0, 0.0), 0.0).astype(F32)
    eye = jnp.where(ri == ci, 1.0, 0.0).astype(F32)
    onw = onw_ref[...]
    alog, dtb = alog_ref[...], dtb_ref[...]

    def stage_a(p):
        r0 = p * PAIR
        sl = slice(r0, r0 + PAIR)
        ab = ab_ref[sl, :]
        xa = ab + dtb
        softplus = jnp.maximum(xa, 0.0) + jnp.log(1.0 + jnp.exp(-jnp.abs(xa)))
        g_all = -jnp.exp(alog) * softplus
        beta_all = jax.nn.sigmoid(ab)
        for h in hs:
            gb_scr[h, sl, :] = jnp.broadcast_to(g_all[:, h:h + 1], (PAIR, LANES))
            bb_scr[h, sl, :] = jnp.broadcast_to(beta_all[:, HEADS + h:HEADS + h + 1], (PAIR, LANES))
            outs = []
            for part, ref in enumerate((q_ref, k_ref, v_ref)):
                s = part * HEADS + h
                xp_scr[s, 8 + r0:8 + r0 + PAIR, :] = ref[h, sl, :].astype(F32)
                w = cw_ref[s]
                y = w[0:1] * xp_scr[s, 5 + r0:5 + r0 + PAIR, :]
                for tap in range(1, GDN_CONV):
                    y = y + w[tap:tap + 1] * xp_scr[s, 5 + tap + r0:5 + tap + r0 + PAIR, :]
                outs.append(y * jax.nn.sigmoid(y))
            qc, kc, vc = outs
            qn_scr[h, sl, :] = (qc * lax.rsqrt(jnp.sum(qc * qc, axis=-1, keepdims=True) + NORM_EPS)
                                * (HEAD_DIM ** -0.5))
            kn_scr[h, sl, :] = kc * lax.rsqrt(jnp.sum(kc * kc, axis=-1, keepdims=True) + NORM_EPS)
            vn_scr[h, sl, :] = vc

    def stage_b(p):
        sl = slice(p * PAIR, (p + 1) * PAIR)
        k = [kn_scr[h, sl, :] for h in hs]
        q = [qn_scr[h, sl, :] for h in hs]
        v = [vn_scr[h, sl, :] for h in hs]
        gb = [gb_scr[h, sl, :] for h in hs]
        bb = [bb_scr[h, sl, :] for h in hs]
        parts = [_split2(jnp.concatenate([gb[h], gb[h]], axis=1) * mx) for h in hs]
        gm = [_dot(ls, pt[0]) + _dot(ls, pt[1]) for pt in parts]
        yield
        decay = [jnp.exp(g[:, :PAIR]) for g in gm]
        gi = [g[:, PAIR:] for g in gm]
        glast = [jnp.concatenate([jnp.broadcast_to(g[c * CHUNK - 1:c * CHUNK, :], (CHUNK, LANES))
                                  for c in (1, 2)], axis=0) for g in gi]
        eg = [jnp.exp(g) for g in gi]
        er = [jnp.exp(glast[h] - gi[h]) for h in hs]
        kb = [k[h] * bb[h] for h in hs]
        kk = [_dot_nt(jnp.concatenate([kb[h], q[h]], axis=0).astype(BF16), k[h].astype(BF16)) for h in hs]
        yield
        n = [-(kk[h][:PAIR] * decay[h] * strict_f) for h in hs]
        aqk = [kk[h][PAIR:] * decay[h] * lower_f for h in hs]
        inv = [eye + n[h] for h in hs]
        nb = [n[h].astype(BF16) for h in hs]
        nb = [_dot(nb[h], nb[h]).astype(BF16) for h in hs]
        yield
        for it in range(5):
            if it < 4:
                m = [_dot(jnp.concatenate([inv[h].astype(BF16), nb[h]], axis=0), nb[h]) for h in hs]
                inv = [inv[h] + m[h][:PAIR] for h in hs]
                nb = [m[h][PAIR:].astype(BF16) for h in hs]
            else:
                inv = [inv[h] + _dot(inv[h].astype(BF16), nb[h]) for h in hs]
            yield
        uw = [_dot(inv[h].astype(BF16),
                   jnp.concatenate([v[h] * bb[h], kb[h] * eg[h]], axis=1).astype(BF16)) for h in hs]
        yield
        kd = [(k[h] * er[h]).astype(BF16) for h in hs]
        qg = [q[h] * eg[h] for h in hs]
        for cc in range(2):
            c = 2 * p + cc
            rs = slice(cc * CHUNK, (cc + 1) * CHUNK)
            kwb = [_dot_tn(kd[h][rs], jnp.concatenate([uw[h][rs, LANES:], uw[h][rs, :LANES]], axis=1).astype(BF16))
                   for h in hs]
            for h in hs:
                wq_scr[h, c] = jnp.concatenate([uw[h][rs, LANES:], qg[h][rs]], axis=0).astype(BF16)
                kw_scr[h, c] = kwb[h][:, :LANES].astype(BF16)
                bm_scr[h, c] = kwb[h][:, LANES:]
                cd_scr[h, c] = jnp.broadcast_to(eg[h][(cc + 1) * CHUNK - 1:(cc + 1) * CHUNK, :], (8, LANES))
            yield
        for h in hs:
            u_scr[h, sl, :] = uw[h][:, :LANES]
            aqk_scr[h, p] = aqk[h].astype(BF16)

    def stage_c(p):
        sl = slice(p * PAIR, (p + 1) * PAIR)
        c0, c1 = 2 * p, 2 * p + 1
        s0 = [s_scr[h] for h in hs]
        sb0 = [s0[h].astype(BF16) for h in hs]
        s1 = [s0[h] * cd_scr[h, c0][0:1, :] + (bm_scr[h, c0] - _dot(kw_scr[h, c0], sb0[h])) for h in hs]
        yield
        sb1 = [s1[h].astype(BF16) for h in hs]
        s2 = [s1[h] * cd_scr[h, c1][0:1, :] + (bm_scr[h, c1] - _dot(kw_scr[h, c1], sb1[h])) for h in hs]
        for h in hs:
            s_scr[h] = s2[h]
        yield
        a0 = [_dot(wq_scr[h, c0], sb0[h]) for h in hs]
        a1 = [_dot(wq_scr[h, c1], sb1[h]) for h in hs]
        yield
        vnew = [(u_scr[h, sl, :] - jnp.concatenate([a0[h][:CHUNK], a1[h][:CHUNK]], axis=0)).astype(BF16)
                for h in hs]
        o = [jnp.concatenate([a0[h][CHUNK:], a1[h][CHUNK:]], axis=0) + _dot(aqk_scr[h, p], vnew[h]) for h in hs]
        yield
        for h in hs:
            z = z_ref[h, sl, :].astype(F32)
            o_ref[h, sl, :] = (_rms(o[h], onw) * (z * jax.nn.sigmoid(z))).astype(BF16)

    def interleave(*gens):
        live = [g for g in gens if g is not None]
        while live:
            for g in list(live):
                try:
                    next(g)
                except StopIteration:
                    live.remove(g)

    stage_a(0)
    for step in range(npair + 1):
        if step + 1 < npair:
            stage_a(step + 1)
        interleave(stage_c(step - 1) if step >= 1 else None, stage_b(step) if step < npair else None)

    for s in range(3 * HEADS):
        xp_scr[s, 0:8, :] = xp_scr[s, rows:rows + 8, :]


def _gdn(p3, ab, conv_w, a_log, dt_bias, out_norm_w, q_blk):
    t = p3.shape[1]
    rows = min(SEQ_BLOCK, t)
    nchunk, npair = rows // CHUNK, rows // PAIR
    cw = conv_w.astype(F32).reshape(GDN_CONV, 3 * HEADS, LANES).transpose(1, 0, 2)
    pad = lambda v: jnp.zeros((1, LANES), F32).at[0, :HEADS].set(v.astype(F32))
    ti = jnp.arange(PAIR)
    same = (ti[:, None] // CHUNK) == (ti[None, :] // CHUNK)
    le = same & (ti[None, :] <= ti[:, None])
    gt = same & (ti[None, :] > ti[:, None])
    ls = le.astype(BF16)
    mx = jnp.concatenate([gt.T.astype(F32), jnp.ones((PAIR, LANES), F32)], axis=1)
    slab = lambda off: pl.BlockSpec((HEADS, rows, LANES), lambda b: (q_blk + off, b, 0))
    const2 = lambda b: (0, 0)
    hr = (HEADS, rows, LANES)
    return pl.pallas_call(
        _gdn_body,
        grid=(t // rows,),
        in_specs=[
            slab(0), slab(1), slab(2), slab(3),
            pl.BlockSpec((rows, LANES), lambda b: (b, 0)),
            pl.BlockSpec(cw.shape, lambda b: (0, 0, 0)),
            pl.BlockSpec((1, LANES), const2),
            pl.BlockSpec((1, LANES), const2),
            pl.BlockSpec((1, LANES), const2),
            pl.BlockSpec(ls.shape, const2),
            pl.BlockSpec(mx.shape, const2),
        ],
        out_specs=pl.BlockSpec(hr, lambda b: (0, b, 0)),
        out_shape=jax.ShapeDtypeStruct((HEADS, t, LANES), BF16),
        scratch_shapes=[
            pltpu.VMEM((HEADS, HEAD_DIM, HEAD_DIM), F32),
            pltpu.VMEM((3 * HEADS, rows + 8, LANES), F32),
            pltpu.VMEM(hr, F32), pltpu.VMEM(hr, F32), pltpu.VMEM(hr, F32),
            pltpu.VMEM(hr, F32), pltpu.VMEM(hr, F32),
            pltpu.VMEM(hr, F32),
            pltpu.VMEM((HEADS, nchunk, 2 * CHUNK, LANES), BF16),
            pltpu.VMEM((HEADS, npair, PAIR, PAIR), BF16),
            pltpu.VMEM((HEADS, nchunk, HEAD_DIM, HEAD_DIM), BF16),
            pltpu.VMEM((HEADS, nchunk, HEAD_DIM, HEAD_DIM), F32),
            pltpu.VMEM((HEADS, nchunk, 8, LANES), F32),
        ],
        compiler_params=pltpu.CompilerParams(
            dimension_semantics=("arbitrary",), vmem_limit_bytes=VMEM_LIMIT),
        name="gdn",
    )(p3, p3, p3, p3, ab, cw, pad(a_log), pad(dt_bias), out_norm_w.astype(F32).reshape(1, LANES), ls, mx)


def _pad_to(w, axis, mult):
    n = w.shape[axis]
    extra = (-n) % mult
    if extra == 0:
        return w
    widths = [(0, 0)] * w.ndim
    widths[axis] = (0, extra)
    return jnp.pad(w, widths)


@jax.jit
def _forward(x, ffn1_norm, ffn1_w_gate, ffn1_w_up, ffn1_w_down, mix_norm, w_in, gdn_conv,
             gdn_A_log, gdn_dt_bias, gdn_out_norm, att_q_norm, att_k_norm, att_rel_bias,
             w_branch_gdn, w_branch_att, w_out, ffn2_norm, ffn2_w_gate, ffn2_w_up, ffn2_w_down):
    b, t, d = x.shape
    gw = HEADS * HEAD_DIM
    outs = []
    for bi in range(b):
        xb = x[bi]
        depth = ffn1_norm.shape[0]
        for l in range(depth):
            wi = w_in[l]
            o_ab = 4 * gw
            o_att = o_ab + 2 * HEADS
            w_ab = _pad_to(wi[:, o_ab:o_att].astype(BF16), 1, LANES)

            act = _ffn_up(_norm(xb, ffn1_norm[l]), ffn1_w_gate[l], ffn1_w_up[l])
            xb, hn, ab = _ffn_down(act, ffn1_w_down[l].astype(BF16), xb, mix_norm[l], w_ab)
            p3 = _proj(hn, wi, o_ab // PROJ_TN, o_att - o_ab, (wi.shape[1] - (o_att - o_ab)) // PROJ_TN)
            og3 = _gdn(p3, ab, gdn_conv[l], gdn_A_log[l], gdn_dt_bias[l], gdn_out_norm[l], 0)
            oa3 = _att(p3, att_q_norm[l], att_k_norm[l], att_rel_bias[l], 4, 5, 6)
            xb, hn = _merge(og3, oa3, p3, 7, xb, w_branch_gdn[l].astype(BF16), w_branch_att[l].astype(BF16),
                            w_out[l].astype(BF16), ffn2_norm[l])

            act = _ffn_up(hn, ffn2_w_gate[l], ffn2_w_up[l])
            xb = _ffn_down(act, ffn2_w_down[l].astype(BF16), xb)
        outs.append(xb)
    return jnp.stack(outs, axis=0)


def kernel(x, ffn1_norm, ffn1_w_gate, ffn1_w_up, ffn1_w_down, mix_norm, w_in, gdn_conv, gdn_A_log, gdn_dt_bias, gdn_out_norm, att_q_norm, att_k_norm, att_rel_bias, w_branch_gdn, w_branch_att, w_out, ffn2_norm, ffn2_w_gate, ffn2_w_up, ffn2_w_down):
    return _forward(x, ffn1_norm, ffn1_w_gate, ffn1_w_up, ffn1_w_down, mix_norm, w_in, gdn_conv,
                    gdn_A_log, gdn_dt_bias, gdn_out_norm, att_q_norm, att_k_norm, att_rel_bias,
                    w_branch_gdn, w_branch_att, w_out, ffn2_norm, ffn2_w_gate, ffn2_w_up, ffn2_w_down)
```

```python
import functools
import math

import jax
import jax.numpy as jnp
from jax import lax
from jax.experimental import pallas as pl
from jax.experimental.pallas import tpu as pltpu

F32 = jnp.float32
BF16 = jnp.bfloat16

NORM_EPS = 1e-6
CHUNK = 64
HEADS = 8
HEAD_DIM = 128
LANES = 128
GDN_CONV = 4
ATT_LEFT_CHUNKS = 8
REL_MAX = 256
NEG = -1e30

VMEM_LIMIT = 56 * 1024 * 1024

LOG2E = math.log2(math.e)

NORM_TM = 512
FFN_UP_TM = 1024
FFN_TF = 512
FFN_DOWN_TM = 256
PROJ_TM = 1024
PROJ_TN = 1024
ATT_HEADS_PER_ITER = 2
MERGE_TM = 512
SEQ_BLOCK = 512
ATT_QG = 256
ATT_WIN = ATT_QG + ATT_LEFT_CHUNKS * CHUNK
ATT_TBL = 1024
PAIR = 2 * CHUNK


def _rms(x, w):
    return x * lax.rsqrt(jnp.mean(x * x, axis=-1, keepdims=True) + NORM_EPS) * w


def _dot(a, b):
    return jnp.dot(a, b, preferred_element_type=F32)


def _dot_nt(a, b):
    return lax.dot_general(a, b, (((1,), (1,)), ((), ())), preferred_element_type=F32)


def _dot_tn(a, b):
    return lax.dot_general(a, b, (((0,), (0,)), ((), ())), preferred_element_type=F32)


def _norm_body(x_ref, nw_ref, o_ref):
    o_ref[...] = _rms(x_ref[...], nw_ref[...]).astype(BF16)


def _norm(x, norm_w):
    t, d = x.shape
    tm = min(NORM_TM, t)
    return pl.pallas_call(
        _norm_body,
        grid=(t // tm,),
        in_specs=[pl.BlockSpec((tm, d), lambda i: (i, 0)), pl.BlockSpec((1, d), lambda i: (0, 0))],
        out_specs=pl.BlockSpec((tm, d), lambda i: (i, 0)),
        out_shape=jax.ShapeDtypeStruct((t, d), BF16),
        compiler_params=pltpu.CompilerParams(dimension_semantics=("parallel",), vmem_limit_bytes=VMEM_LIMIT),
        name="norm",
    )(x, norm_w.reshape(1, d))


def _ffn_up_body(h_ref, wg_ref, wu_ref, wd_ref, o_ref, wdb_ref, w_scr):
    tf = wg_ref.shape[1]

    @pl.when(pl.program_id(1) == 0)
    def _():
        w_scr[:, :tf] = wg_ref[...].astype(BF16)
        w_scr[:, tf:] = wu_ref[...].astype(BF16)
        wdb_ref[...] = wd_ref[...].astype(BF16)

    gu = _dot(h_ref[...], w_scr[...])
    g, u = gu[:, :tf], gu[:, tf:]
    o_ref[...] = (g * jax.nn.sigmoid(g) * u).astype(BF16)


def _ffn_up(h, wg, wu, wd):
    t, d = h.shape
    f = wg.shape[1]
    tm, tf = min(FFN_UP_TM, t), FFN_TF
    return pl.pallas_call(
        _ffn_up_body,
        grid=(pl.cdiv(f, tf), t // tm),
        in_specs=[
            pl.BlockSpec((tm, d), lambda j, i: (i, 0)),
            pl.BlockSpec((d, tf), lambda j, i: (0, j)),
            pl.BlockSpec((d, tf), lambda j, i: (0, j)),
            pl.BlockSpec((tf, d), lambda j, i: (j, 0)),
        ],
        out_specs=[pl.BlockSpec((tm, tf), lambda j, i: (i, j)),
                   pl.BlockSpec((tf, d), lambda j, i: (j, 0))],
        out_shape=[jax.ShapeDtypeStruct((t, f), BF16), jax.ShapeDtypeStruct((f, d), BF16)],
        scratch_shapes=[pltpu.VMEM((d, 2 * tf), BF16)],
        compiler_params=pltpu.CompilerParams(
            dimension_semantics=("arbitrary", "arbitrary"), vmem_limit_bytes=VMEM_LIMIT),
        name="ffn_up",
    )(h, wg, wu, wd)


def _ffn_down_body(a_ref, wd_ref, x_ref, o_ref):
    o_ref[...] = x_ref[...] + 0.5 * _dot(a_ref[...], wd_ref[...])


def _ffn_down_norm_body(a_ref, wd_ref, x_ref, nw_ref, ws_ref, o_ref, hn_ref, side_ref):
    y = x_ref[...] + 0.5 * _dot(a_ref[...], wd_ref[...])
    o_ref[...] = y
    hn = _rms(y, nw_ref[...]).astype(BF16)
    hn_ref[...] = hn
    side_ref[...] = _dot(hn, ws_ref[...])


def _ffn_down(act, wd, x, next_norm_w=None, w_side=None):
    t, d = x.shape
    f = act.shape[1]
    tm = min(FFN_DOWN_TM, t)
    row = pl.BlockSpec((tm, d), lambda i: (i, 0))
    in_specs = [pl.BlockSpec((tm, f), lambda i: (i, 0)),
                pl.BlockSpec((f, d), lambda i: (0, 0), pipeline_mode=pl.Buffered(1)),
                row]
    params = pltpu.CompilerParams(dimension_semantics=("parallel",), vmem_limit_bytes=VMEM_LIMIT)
    if next_norm_w is None:
        assert w_side is None
        return pl.pallas_call(
            _ffn_down_body, grid=(t // tm,), in_specs=in_specs, out_specs=row,
            out_shape=jax.ShapeDtypeStruct((t, d), F32), compiler_params=params, name="ffn_down",
        )(act, wd, x)
    return pl.pallas_call(
        _ffn_down_norm_body, grid=(t // tm,),
        in_specs=in_specs + [pl.BlockSpec((1, d), lambda i: (0, 0)), pl.BlockSpec((d, LANES), lambda i: (0, 0))],
        out_specs=[row, row, pl.BlockSpec((tm, LANES), lambda i: (i, 0))],
        out_shape=[jax.ShapeDtypeStruct((t, d), F32), jax.ShapeDtypeStruct((t, d), BF16),
                   jax.ShapeDtypeStruct((t, LANES), F32)],
        compiler_params=params, name="ffn_down_norm",
    )(act, wd, x, next_norm_w.reshape(1, d), w_side)


def _proj_body(h_ref, wf_ref, wx_ref, p_ref, w_scr, *, n_aligned, shift):
    j, i = pl.program_id(0), pl.program_id(1)
    tn = wf_ref.shape[0]

    @pl.when(jnp.logical_and(i == 0, j < n_aligned))
    def _():
        w_scr[...] = wf_ref[...].astype(BF16)

    @pl.when(jnp.logical_and(i == 0, j >= n_aligned))
    def _():
        w_scr[:tn - shift, :] = wf_ref[shift:, :].astype(BF16)
        w_scr[tn - shift:, :] = wx_ref[:shift, :].astype(BF16)

    r = _dot_nt(h_ref[...], w_scr[...])
    for c in range(p_ref.shape[0]):
        p_ref[c] = r[:, c * LANES:(c + 1) * LANES].astype(BF16)


def _proj(h, wt, n_aligned, shift, n_tiles):
    t, d = h.shape
    tm, tn = min(PROJ_TM, t), PROJ_TN
    assert 0 < shift < LANES and shift % 16 == 0 and wt.shape[0] == n_tiles * tn + shift
    spb = tn // LANES
    return pl.pallas_call(
        functools.partial(_proj_body, n_aligned=n_aligned, shift=shift),
        grid=(n_tiles, t // tm),
        in_specs=[
            pl.BlockSpec((tm, d), lambda j, i: (i, 0)),
            pl.BlockSpec((tn, d), lambda j, i: (j, 0)),
            pl.BlockSpec((LANES, d), lambda j, i: ((j + 1) * spb, 0)),
        ],
        out_specs=pl.BlockSpec((spb, tm, LANES), lambda j, i: (j, i, 0)),
        out_shape=jax.ShapeDtypeStruct((n_tiles * spb, t, LANES), BF16),
        scratch_shapes=[pltpu.VMEM((tn, d), BF16)],
        compiler_params=pltpu.CompilerParams(
            dimension_semantics=("arbitrary", "arbitrary"), vmem_limit_bytes=VMEM_LIMIT),
        name="proj",
    )(h, wt, wt)


def _merge_body(og_ref, oa_ref, gg0_ref, gg1_ref, ga0_ref, ga1_ref, x_ref, wa_ref, wb_ref, wo_ref, nw_ref,
                o_ref, hn_ref):
    def slabs(*refs):
        return jnp.concatenate([ref[c] for ref in refs for c in range(ref.shape[0])], axis=-1)

    ya = _dot(slabs(og_ref), wa_ref[...])
    yb = _dot(slabs(oa_ref), wb_ref[...])
    m = (jax.nn.sigmoid(slabs(gg0_ref, gg1_ref).astype(F32)) * ya
         + jax.nn.sigmoid(slabs(ga0_ref, ga1_ref).astype(F32)) * yb)
    y = x_ref[...] + _dot(m.astype(BF16), wo_ref[...])
    o_ref[...] = y
    hn_ref[...] = _rms(y, nw_ref[...]).astype(BF16)


def _merge(og3, oa3, p3, gate_blk, x, wa, wb, wo, next_norm_w):
    t, d = x.shape
    tm = min(MERGE_TM, t)
    assert d == 2 * HEADS * LANES
    const = lambda i: (0, 0)
    row = pl.BlockSpec((tm, d), lambda i: (i, 0))
    slab = lambda blk: pl.BlockSpec((HEADS, tm, LANES), lambda i: (blk, i, 0))
    return pl.pallas_call(
        _merge_body,
        grid=(t // tm,),
        in_specs=[
            slab(0), slab(0),
            slab(gate_blk), slab(gate_blk + 1), slab(gate_blk + 2), slab(gate_blk + 3),
            row,
            pl.BlockSpec(wa.shape, const, pipeline_mode=pl.Buffered(1)),
            pl.BlockSpec(wb.shape, const, pipeline_mode=pl.Buffered(1)),
            pl.BlockSpec(wo.shape, const, pipeline_mode=pl.Buffered(1)),
            pl.BlockSpec((1, d), const),
        ],
        out_specs=[row, row],
        out_shape=[jax.ShapeDtypeStruct((t, d), F32), jax.ShapeDtypeStruct((t, d), BF16)],
        compiler_params=pltpu.CompilerParams(
            dimension_semantics=("parallel",), vmem_limit_bytes=VMEM_LIMIT),
        name="merge",
    )(og3, oa3, p3, p3, p3, p3, x, wa, wb, wo, next_norm_w.reshape(1, d))


def _att_body(q_ref, k_ref, v_ref, qw_ref, kw_ref, bvec_ref, w0_ref, w1_ref, w2_ref,
              o_ref, w0b_ref, w1b_ref, w2b_ref, kbuf, vbuf, bias_scr):
    blk = pl.program_id(0)
    rb = q_ref.shape[1]

    w0b_ref[...] = w0_ref[...].astype(BF16)
    w1b_ref[...] = w1_ref[...].astype(BF16)
    w2b_ref[...] = w2_ref[...].astype(BF16)

    @pl.when(blk == 0)
    def _():
        kbuf[:, :rb, :] = jnp.zeros((HEADS, rb, LANES), BF16)
        vbuf[:, :rb, :] = jnp.zeros((HEADS, rb, LANES), BF16)
        qc = lax.broadcasted_iota(jnp.int32, (ATT_QG, ATT_WIN), 0) // CHUNK
        kc = lax.broadcasted_iota(jnp.int32, (ATT_QG, ATT_WIN), 1) // CHUNK
        off = kc - qc
        band = jnp.where(off >= 0, jnp.where(off <= ATT_LEFT_CHUNKS, 0.0, NEG), NEG).astype(F32)
        for h in range(HEADS):
            tbl = jnp.broadcast_to(bvec_ref[h], (ATT_QG, ATT_TBL))
            tbl = pltpu.roll(tbl, 0, 1, stride=1, stride_axis=0)
            bias_scr[h] = tbl[:, :ATT_WIN] * LOG2E + band

    first_neg = jnp.where(blk == 0, NEG, 0.0).astype(F32)
    col = lax.broadcasted_iota(jnp.int32, (1, ATT_WIN), 1)
    qw = qw_ref[...] * (HEAD_DIM ** -0.5 * LOG2E)
    kw = kw_ref[...]
    groups = [(dh, g * ATT_QG) for dh in range(ATT_HEADS_PER_ITER) for g in range(rb // ATT_QG)]

    def heads(hi, carry):
        h0 = hi * ATT_HEADS_PER_ITER
        for dh in range(ATT_HEADS_PER_ITER):
            kbuf[h0 + dh, rb:, :] = _rms(k_ref[h0 + dh].astype(F32), kw).astype(BF16)
            vbuf[h0 + dh, rb:, :] = v_ref[h0 + dh]
        qn = [_rms(q_ref[h0 + dh, r0:r0 + ATT_QG, :].astype(F32), qw).astype(BF16) for dh, r0 in groups]
        s = [_dot_nt(qn[n], kbuf[h0 + dh, r0:r0 + ATT_WIN, :]) for n, (dh, r0) in enumerate(groups)]
        s = [s[n] + bias_scr[h0 + dh] + jnp.where(col < rb - r0, first_neg, 0.0)
             for n, (dh, r0) in enumerate(groups)]
        p = [jnp.exp2(sn - jnp.max(sn, axis=-1, keepdims=True)) for sn in s]
        l = [jnp.sum(pn, axis=-1, keepdims=True) for pn in p]
        o = [_dot(p[n].astype(BF16), vbuf[h0 + dh, r0:r0 + ATT_WIN, :]) for n, (dh, r0) in enumerate(groups)]
        for n, (dh, r0) in enumerate(groups):
            o_ref[h0 + dh, r0:r0 + ATT_QG, :] = (o[n] / l[n]).astype(BF16)
        return carry

    lax.fori_loop(0, HEADS // ATT_HEADS_PER_ITER, heads, 0)
    kbuf[:, :rb, :] = kbuf[:, rb:, :]
    vbuf[:, :rb, :] = vbuf[:, rb:, :]


def _att_bias_table(rel_bias):
    tbl = rel_bias.astype(F32)
    n_tbl = tbl.shape[1]
    far = ATT_LEFT_CHUNKS * CHUNK - REL_MAX
    rep = lambda c, n: jnp.broadcast_to(tbl[:, c:c + 1], (tbl.shape[0], n))
    vec = jnp.concatenate([rep(n_tbl - 1, far), tbl[:, ::-1],
                           rep(0, ATT_TBL - far - n_tbl - (ATT_QG - 1)), rep(n_tbl - 1, ATT_QG - 1)], axis=1)
    return vec[:, None, :]


def _att(p3, q_norm_w, k_norm_w, rel_bias, q_blk, k_blk, v_blk, side_weights):
    t = p3.shape[1]
    rb = SEQ_BLOCK
    assert rb == ATT_LEFT_CHUNKS * CHUNK and t % rb == 0
    nblk = t // rb
    bvec = _att_bias_table(rel_bias)
    assert all(w.shape[0] % (16 * nblk) == 0 for w in side_weights)
    side_specs = [pl.BlockSpec((w.shape[0] // nblk, w.shape[1]), lambda b: (b, 0)) for w in side_weights]
    outs = pl.pallas_call(
        _att_body,
        grid=(nblk,),
        in_specs=[
            pl.BlockSpec((HEADS, rb, LANES), lambda b: (q_blk, b, 0)),
            pl.BlockSpec((HEADS, rb, LANES), lambda b: (k_blk, b, 0)),
            pl.BlockSpec((HEADS, rb, LANES), lambda b: (v_blk, b, 0)),
            pl.BlockSpec((1, LANES), lambda b: (0, 0)),
            pl.BlockSpec((1, LANES), lambda b: (0, 0)),
            pl.BlockSpec(bvec.shape, lambda b: (0, 0, 0)),
        ] + side_specs,
        out_specs=[pl.BlockSpec((HEADS, rb, LANES), lambda b: (0, b, 0))] + side_specs,
        out_shape=[jax.ShapeDtypeStruct((HEADS, t, LANES), BF16)]
                  + [jax.ShapeDtypeStruct(w.shape, BF16) for w in side_weights],
        scratch_shapes=[pltpu.VMEM((HEADS, 2 * rb, LANES), BF16),
                        pltpu.VMEM((HEADS, 2 * rb, LANES), BF16),
                        pltpu.VMEM((HEADS, ATT_QG, ATT_WIN), F32)],
        compiler_params=pltpu.CompilerParams(
            dimension_semantics=("arbitrary",), vmem_limit_bytes=VMEM_LIMIT),
        name="att",
    )(p3, p3, p3, q_norm_w.reshape(1, LANES), k_norm_w.reshape(1, LANES), bvec, *side_weights)
    return outs[0], outs[1:]


def _split2(x):
    hi = x.astype(BF16)
    return hi, (x - hi.astype(F32)).astype(BF16)


def _gdn_body(q_ref, k_ref, v_ref, z_ref, ab_ref, cw_ref, alog_ref, dtb_ref, onw_ref, ls_ref, mx_ref,
              o_ref,
              s_scr, xp_scr, qn_scr, kn_scr, vn_scr, gb_scr, bb_scr, u_scr, wq_scr, aqk_scr, kw_scr,
              bm_scr, cd_scr):
    blk = pl.program_id(0)
    rows = q_ref.shape[1]
    npair = rows // PAIR
    hs = range(HEADS)

    @pl.when(blk == 0)
    def _():
        s_scr[...] = jnp.zeros_like(s_scr)
        xp_scr[:, 0:8, :] = jnp.zeros((3 * HEADS, 8, LANES), F32)

    ls = ls_ref[...]
    mx = mx_ref[...]
    ri = lax.broadcasted_iota(jnp.int32, (PAIR, PAIR), 0)
    ci = lax.broadcasted_iota(jnp.int32, (PAIR, PAIR), 1)
    same = (ri // CHUNK) == (ci // CHUNK)
    lower_f = jnp.where(same, jnp.where(ri >= ci, 1.0, 0.0), 0.0).astype(F32)
    strict_f = jnp.where(same, jnp.where(ri > ci, 1.0, 0.0), 0.0).astype(F32)
    eye = jnp.where(ri == ci, 1.0, 0.0).astype(F32)
    onw = onw_ref[...]
    alog, dtb = alog_ref[...], dtb_ref[...]

    def stage_a(p):
        r0 = p * PAIR
        sl = slice(r0, r0 + PAIR)
        ab = ab_ref[sl, :]
        xa = ab + dtb
        softplus = jnp.maximum(xa, 0.0) + jnp.log(1.0 + jnp.exp(-jnp.abs(xa)))
        g_all = -jnp.exp(alog) * softplus
        beta_all = jax.nn.sigmoid(ab)
        for h in hs:
            gb_scr[h, sl, :] = jnp.broadcast_to(g_all[:, h:h + 1], (PAIR, LANES))
            bb_scr[h, sl, :] = jnp.broadcast_to(beta_all[:, HEADS + h:HEADS + h + 1], (PAIR, LANES))
            outs = []
            for part, ref in enumerate((q_ref, k_ref, v_ref)):
                s = part * HEADS + h
                xp_scr[s, 8 + r0:8 + r0 + PAIR, :] = ref[h, sl, :].astype(F32)
                w = cw_ref[s]
                y = w[0:1] * xp_scr[s, 5 + r0:5 + r0 + PAIR, :]
                for tap in range(1, GDN_CONV):
                    y = y + w[tap:tap + 1] * xp_scr[s, 5 + tap + r0:5 + tap + r0 + PAIR, :]
                outs.append(y * jax.nn.sigmoid(y))
            qc, kc, vc = outs
            qn_scr[h, sl, :] = (qc * lax.rsqrt(jnp.sum(qc * qc, axis=-1, keepdims=True) + NORM_EPS)
                                * (HEAD_DIM ** -0.5))
            kn_scr[h, sl, :] = kc * lax.rsqrt(jnp.sum(kc * kc, axis=-1, keepdims=True) + NORM_EPS)
            vn_scr[h, sl, :] = vc

    def stage_b(p):
        sl = slice(p * PAIR, (p + 1) * PAIR)
        k = [kn_scr[h, sl, :] for h in hs]
        q = [qn_scr[h, sl, :] for h in hs]
        v = [vn_scr[h, sl, :] for h in hs]
        gb = [gb_scr[h, sl, :] for h in hs]
        bb = [bb_scr[h, sl, :] for h in hs]
        parts = [_split2(jnp.concatenate([gb[h], gb[h]], axis=1) * mx) for h in hs]
        gm = [_dot(ls, pt[0]) + _dot(ls, pt[1]) for pt in parts]
        yield
        decay = [jnp.exp(g[:, :PAIR]) for g in gm]
        gi = [g[:, PAIR:] for g in gm]
        glast = [jnp.concatenate([jnp.broadcast_to(g[c * CHUNK - 1:c * CHUNK, :], (CHUNK, LANES))
                                  for c in (1, 2)], axis=0) for g in gi]
        eg = [jnp.exp(g) for g in gi]
        er = [jnp.exp(glast[h] - gi[h]) for h in hs]
        kb = [k[h] * bb[h] for h in hs]
        kk = [_dot_nt(jnp.concatenate([kb[h], q[h]], axis=0).astype(BF16), k[h].astype(BF16)) for h in hs]
        yield
        n = [-(kk[h][:PAIR] * decay[h] * strict_f) for h in hs]
        aqk = [kk[h][PAIR:] * decay[h] * lower_f for h in hs]
        inv = [eye + n[h] for h in hs]
        nb = [n[h].astype(BF16) for h in hs]
        nb = [_dot(nb[h], nb[h]).astype(BF16) for h in hs]
        yield
        for it in range(5):
            if it < 4:
                m = [_dot(jnp.concatenate([inv[h].astype(BF16), nb[h]], axis=0), nb[h]) for h in hs]
                inv = [inv[h] + m[h][:PAIR] for h in hs]
                nb = [m[h][PAIR:].astype(BF16) for h in hs]
            else:
                inv = [inv[h] + _dot(inv[h].astype(BF16), nb[h]) for h in hs]
            yield
        uw = [_dot(inv[h].astype(BF16),
                   jnp.concatenate([v[h] * bb[h], kb[h] * eg[h]], axis=1).astype(BF16)) for h in hs]
        yield
        kd = [(k[h] * er[h]).astype(BF16) for h in hs]
        qg = [q[h] * eg[h] for h in hs]
        for cc in range(2):
            c = 2 * p + cc
            rs = slice(cc * CHUNK, (cc + 1) * CHUNK)
            kwb = [_dot_tn(kd[h][rs], jnp.concatenate([uw[h][rs, LANES:], uw[h][rs, :LANES]], axis=1).astype(BF16))
                   for h in hs]
            for h in hs:
                wq_scr[h, c] = jnp.concatenate([uw[h][rs, LANES:], qg[h][rs]], axis=0).astype(BF16)
                kw_scr[h, c] = kwb[h][:, :LANES].astype(BF16)
                bm_scr[h, c] = kwb[h][:, LANES:]
                cd_scr[h, c] = jnp.broadcast_to(eg[h][(cc + 1) * CHUNK - 1:(cc + 1) * CHUNK, :], (8, LANES))
            yield
        for h in hs:
            u_scr[h, sl, :] = uw[h][:, :LANES]
            aqk_scr[h, p] = aqk[h].astype(BF16)

    def stage_c(p):
        sl = slice(p * PAIR, (p + 1) * PAIR)
        c0, c1 = 2 * p, 2 * p + 1
        s0 = [s_scr[h] for h in hs]
        sb0 = [s0[h].astype(BF16) for h in hs]
        s1 = [s0[h] * cd_scr[h, c0][0:1, :] + (bm_scr[h, c0] - _dot(kw_scr[h, c0], sb0[h])) for h in hs]
        yield
        sb1 = [s1[h].astype(BF16) for h in hs]
        s2 = [s1[h] * cd_scr[h, c1][0:1, :] + (bm_scr[h, c1] - _dot(kw_scr[h, c1], sb1[h])) for h in hs]
        for h in hs:
            s_scr[h] = s2[h]
        yield
        a0 = [_dot(wq_scr[h, c0], sb0[h]) for h in hs]
        a1 = [_dot(wq_scr[h, c1], sb1[h]) for h in hs]
        yield
        vnew = [(u_scr[h, sl, :] - jnp.concatenate([a0[h][:CHUNK], a1[h][:CHUNK]], axis=0)).astype(BF16)
                for h in hs]
        o = [jnp.concatenate([a0[h][CHUNK:], a1[h][CHUNK:]], axis=0) + _dot(aqk_scr[h, p], vnew[h]) for h in hs]
        yield
        for h in hs:
            z = z_ref[h, sl, :].astype(F32)
            o_ref[h, sl, :] = (_rms(o[h], onw) * (z * jax.nn.sigmoid(z))).astype(BF16)

    def interleave(*gens):
        live = [g for g in gens if g is not None]
        while live:
            for g in list(live):
                try:
                    next(g)
                except StopIteration:
                    live.remove(g)

    stage_a(0)
    for step in range(npair + 1):
        if step + 1 < npair:
            stage_a(step + 1)
        interleave(stage_c(step - 1) if step >= 1 else None, stage_b(step) if step < npair else None)

    for s in range(3 * HEADS):
        xp_scr[s, 0:8, :] = xp_scr[s, rows:rows + 8, :]


def _gdn(p3, ab, conv_w, a_log, dt_bias, out_norm_w, q_blk):
    t = p3.shape[1]
    rows = min(SEQ_BLOCK, t)
    nchunk, npair = rows // CHUNK, rows // PAIR
    cw = conv_w.astype(F32).reshape(GDN_CONV, 3 * HEADS, LANES).transpose(1, 0, 2)
    pad = lambda v: jnp.zeros((1, LANES), F32).at[0, :HEADS].set(v.astype(F32))
    ti = jnp.arange(PAIR)
    same = (ti[:, None] // CHUNK) == (ti[None, :] // CHUNK)
    le = same & (ti[None, :] <= ti[:, None])
    gt = same & (ti[None, :] > ti[:, None])
    ls = le.astype(BF16)
    mx = jnp.concatenate([gt.T.astype(F32), jnp.ones((PAIR, LANES), F32)], axis=1)
    slab = lambda off: pl.BlockSpec((HEADS, rows, LANES), lambda b: (q_blk + off, b, 0))
    const2 = lambda b: (0, 0)
    hr = (HEADS, rows, LANES)
    return pl.pallas_call(
        _gdn_body,
        grid=(t // rows,),
        in_specs=[
            slab(0), slab(1), slab(2), slab(3),
            pl.BlockSpec((rows, LANES), lambda b: (b, 0)),
            pl.BlockSpec(cw.shape, lambda b: (0, 0, 0)),
            pl.BlockSpec((1, LANES), const2),
            pl.BlockSpec((1, LANES), const2),
            pl.BlockSpec((1, LANES), const2),
            pl.BlockSpec(ls.shape, const2),
            pl.BlockSpec(mx.shape, const2),
        ],
        out_specs=pl.BlockSpec(hr, lambda b: (0, b, 0)),
        out_shape=jax.ShapeDtypeStruct((HEADS, t, LANES), BF16),
        scratch_shapes=[
            pltpu.VMEM((HEADS, HEAD_DIM, HEAD_DIM), F32),
            pltpu.VMEM((3 * HEADS, rows + 8, LANES), F32),
            pltpu.VMEM(hr, F32), pltpu.VMEM(hr, F32), pltpu.VMEM(hr, F32),
            pltpu.VMEM(hr, F32), pltpu.VMEM(hr, F32),
            pltpu.VMEM(hr, F32),
            pltpu.VMEM((HEADS, nchunk, 2 * CHUNK, LANES), BF16),
            pltpu.VMEM((HEADS, npair, PAIR, PAIR), BF16),
            pltpu.VMEM((HEADS, nchunk, HEAD_DIM, HEAD_DIM), BF16),
            pltpu.VMEM((HEADS, nchunk, HEAD_DIM, HEAD_DIM), F32),
            pltpu.VMEM((HEADS, nchunk, 8, LANES), F32),
        ],
        compiler_params=pltpu.CompilerParams(
            dimension_semantics=("arbitrary",), vmem_limit_bytes=VMEM_LIMIT),
        name="gdn",
    )(p3, p3, p3, p3, ab, cw, pad(a_log), pad(dt_bias), out_norm_w.astype(F32).reshape(1, LANES), ls, mx)


def _pad_to(w, axis, mult):
    n = w.shape[axis]
    extra = (-n) % mult
    if extra == 0:
        return w
    widths = [(0, 0)] * w.ndim
    widths[axis] = (0, extra)
    return jnp.pad(w, widths)


@jax.jit
def _forward(x, ffn1_norm, ffn1_w_gate, ffn1_w_up, ffn1_w_down, mix_norm, w_in, gdn_conv,
             gdn_A_log, gdn_dt_bias, gdn_out_norm, att_q_norm, att_k_norm, att_rel_bias,
             w_branch_gdn, w_branch_att, w_out, ffn2_norm, ffn2_w_gate, ffn2_w_up, ffn2_w_down):
    b, t, d = x.shape
    gw = HEADS * HEAD_DIM
    outs = []
    for bi in range(b):
        xb = x[bi]
        depth = ffn1_norm.shape[0]
        for l in range(depth):
            wi = w_in[l]
            o_ab = 4 * gw
            o_att = o_ab + 2 * HEADS
            w_ab = _pad_to(wi[:, o_ab:o_att].astype(BF16), 1, LANES)

            act, wd = _ffn_up(_norm(xb, ffn1_norm[l]), ffn1_w_gate[l], ffn1_w_up[l], ffn1_w_down[l])
            xb, hn, ab = _ffn_down(act, wd, xb, mix_norm[l], w_ab)
            p3 = _proj(hn, wi.T, o_ab // PROJ_TN, o_att - o_ab, (wi.shape[1] - (o_att - o_ab)) // PROJ_TN)
            og3 = _gdn(p3, ab, gdn_conv[l], gdn_A_log[l], gdn_dt_bias[l], gdn_out_norm[l], 0)
            oa3, (wa, wb, wo) = _att(p3, att_q_norm[l], att_k_norm[l], att_rel_bias[l], 4, 5, 6,
                                     (w_branch_gdn[l], w_branch_att[l], w_out[l]))
            xb, hn = _merge(og3, oa3, p3, 7, xb, wa, wb, wo, ffn2_norm[l])

            act, wd = _ffn_up(hn, ffn2_w_gate[l], ffn2_w_up[l], ffn2_w_down[l])
            xb = _ffn_down(act, wd, xb)
        outs.append(xb)
    return jnp.stack(outs, axis=0)


def kernel(x, ffn1_norm, ffn1_w_gate, ffn1_w_up, ffn1_w_down, mix_norm, w_in, gdn_conv, gdn_A_log, gdn_dt_bias, gdn_out_norm, att_q_norm, att_k_norm, att_rel_bias, w_branch_gdn, w_branch_att, w_out, ffn2_norm, ffn2_w_gate, ffn2_w_up, ffn2_w_down):
    return _forward(x, ffn1_norm, ffn1_w_gate, ffn1_w_up, ffn1_w_down, mix_norm, w_in, gdn_conv,
                    gdn_A_log, gdn_dt_bias, gdn_out_norm, att_q_norm, att_k_norm, att_rel_bias,
                    w_branch_gdn, w_branch_att, w_out, ffn2_norm, ffn2_w_gate, ffn2_w_up, ffn2_w_down)
```

```python
import functools
import math

import jax
import jax.numpy as jnp
from jax import lax
from jax.experimental import pallas as pl
from jax.experimental.pallas import tpu as pltpu

F32 = jnp.float32
BF16 = jnp.bfloat16

NORM_EPS = 1e-6
CHUNK = 64
HEADS = 8
HEAD_DIM = 128
LANES = 128
GDN_CONV = 4
ATT_LEFT_CHUNKS = 8
REL_MAX = 256
NEG = -1e30

VMEM_LIMIT = 56 * 1024 * 1024

LOG2E = math.log2(math.e)

NORM_TM = 512
FFN_UP_TM = 1024
FFN_TF = 512
FFN_DOWN_TM = 256
PROJ_TM = 1024
PROJ_TN = 1024
ATT_HEADS_PER_ITER = 2
MERGE_TM = 512
SEQ_BLOCK = 512
ATT_QG = 256
ATT_WIN = ATT_QG + ATT_LEFT_CHUNKS * CHUNK
ATT_TBL = 1024
PAIR = 2 * CHUNK


def _rms(x, w):
    return x * lax.rsqrt(jnp.mean(x * x, axis=-1, keepdims=True) + NORM_EPS) * w


def _dot(a, b):
    return jnp.dot(a, b, preferred_element_type=F32)


def _dot_nt(a, b):
    return lax.dot_general(a, b, (((1,), (1,)), ((), ())), preferred_element_type=F32)


def _dot_tn(a, b):
    return lax.dot_general(a, b, (((0,), (0,)), ((), ())), preferred_element_type=F32)


def _norm_body(x_ref, nw_ref, o_ref):
    o_ref[...] = _rms(x_ref[...], nw_ref[...]).astype(BF16)


def _norm(x, norm_w):
    t, d = x.shape
    tm = min(NORM_TM, t)
    return pl.pallas_call(
        _norm_body,
        grid=(t // tm,),
        in_specs=[pl.BlockSpec((tm, d), lambda i: (i, 0)), pl.BlockSpec((1, d), lambda i: (0, 0))],
        out_specs=pl.BlockSpec((tm, d), lambda i: (i, 0)),
        out_shape=jax.ShapeDtypeStruct((t, d), BF16),
        compiler_params=pltpu.CompilerParams(dimension_semantics=("parallel",), vmem_limit_bytes=VMEM_LIMIT),
        name="norm",
    )(x, norm_w.reshape(1, d))


def _ffn_up_body(h_ref, wg_ref, wu_ref, wd_ref, o_ref, wdb_ref, w_scr, *, n_side):
    tf = wg_ref.shape[1]

    @pl.when(pl.program_id(1) == 0)
    def _():
        w_scr[:, :tf] = wg_ref[...].astype(BF16)
        w_scr[:, tf:] = wu_ref[...].astype(BF16)

    @pl.when(pl.program_id(0) * pl.num_programs(1) + pl.program_id(1) < n_side)
    def _():
        wdb_ref[...] = wd_ref[...].astype(BF16)

    gu = _dot(h_ref[...], w_scr[...])
    g, u = gu[:, :tf], gu[:, tf:]
    o_ref[...] = (g * jax.nn.sigmoid(g) * u).astype(BF16)


def _ffn_up(h, wg, wu, wd):
    t, d = h.shape
    f = wg.shape[1]
    tm, tf = min(FFN_UP_TM, t), FFN_TF
    nj, ni = pl.cdiv(f, tf), t // tm
    rows = next(r for r in range(16, f + 1, 16) if f % r == 0 and f // r <= nj * ni)
    n_side = f // rows
    side = pl.BlockSpec((rows, d), lambda j, i: (jnp.minimum(j * ni + i, n_side - 1), 0))
    return pl.pallas_call(
        functools.partial(_ffn_up_body, n_side=n_side),
        grid=(nj, ni),
        in_specs=[
            pl.BlockSpec((tm, d), lambda j, i: (i, 0)),
            pl.BlockSpec((d, tf), lambda j, i: (0, j)),
            pl.BlockSpec((d, tf), lambda j, i: (0, j)),
            side,
        ],
        out_specs=[pl.BlockSpec((tm, tf), lambda j, i: (i, j)), side],
        out_shape=[jax.ShapeDtypeStruct((t, f), BF16), jax.ShapeDtypeStruct((f, d), BF16)],
        scratch_shapes=[pltpu.VMEM((d, 2 * tf), BF16)],
        compiler_params=pltpu.CompilerParams(
            dimension_semantics=("arbitrary", "arbitrary"), vmem_limit_bytes=VMEM_LIMIT),
        name="ffn_up",
    )(h, wg, wu, wd)


def _ffn_down_body(a_ref, wd_ref, x_ref, o_ref):
    o_ref[...] = x_ref[...] + 0.5 * _dot(a_ref[...], wd_ref[...])


def _ffn_down_norm_body(a_ref, wd_ref, x_ref, nw_ref, ws_ref, o_ref, hn_ref, side_ref):
    y = x_ref[...] + 0.5 * _dot(a_ref[...], wd_ref[...])
    o_ref[...] = y
    hn = _rms(y, nw_ref[...]).astype(BF16)
    hn_ref[...] = hn
    side_ref[...] = _dot(hn, ws_ref[...])


def _ffn_down(act, wd, x, next_norm_w=None, w_side=None):
    t, d = x.shape
    f = act.shape[1]
    tm = min(FFN_DOWN_TM, t)
    row = pl.BlockSpec((tm, d), lambda i: (i, 0))
    in_specs = [pl.BlockSpec((tm, f), lambda i: (i, 0)),
                pl.BlockSpec((f, d), lambda i: (0, 0), pipeline_mode=pl.Buffered(1)),
                row]
    params = pltpu.CompilerParams(dimension_semantics=("parallel",), vmem_limit_bytes=VMEM_LIMIT)
    if next_norm_w is None:
        assert w_side is None
        return pl.pallas_call(
            _ffn_down_body, grid=(t // tm,), in_specs=in_specs, out_specs=row,
            out_shape=jax.ShapeDtypeStruct((t, d), F32), compiler_params=params, name="ffn_down",
        )(act, wd, x)
    return pl.pallas_call(
        _ffn_down_norm_body, grid=(t // tm,),
        in_specs=in_specs + [pl.BlockSpec((1, d), lambda i: (0, 0)), pl.BlockSpec((d, LANES), lambda i: (0, 0))],
        out_specs=[row, row, pl.BlockSpec((tm, LANES), lambda i: (i, 0))],
        out_shape=[jax.ShapeDtypeStruct((t, d), F32), jax.ShapeDtypeStruct((t, d), BF16),
                   jax.ShapeDtypeStruct((t, LANES), F32)],
        compiler_params=params, name="ffn_down_norm",
    )(act, wd, x, next_norm_w.reshape(1, d), w_side)


def _proj_body(h_ref, wf_ref, wx_ref, p_ref, w_scr, *, n_aligned, shift):
    j, i = pl.program_id(0), pl.program_id(1)
    tn = wf_ref.shape[0]

    @pl.when(jnp.logical_and(i == 0, j < n_aligned))
    def _():
        w_scr[...] = wf_ref[...].astype(BF16)

    @pl.when(jnp.logical_and(i == 0, j >= n_aligned))
    def _():
        w_scr[:tn - shift, :] = wf_ref[shift:, :].astype(BF16)
        w_scr[tn - shift:, :] = wx_ref[:shift, :].astype(BF16)

    r = _dot_nt(h_ref[...], w_scr[...])
    for c in range(p_ref.shape[0]):
        p_ref[c] = r[:, c * LANES:(c + 1) * LANES].astype(BF16)


def _proj(h, wt, n_aligned, shift, n_tiles):
    t, d = h.shape
    tm, tn = min(PROJ_TM, t), PROJ_TN
    assert 0 < shift < LANES and shift % 16 == 0 and wt.shape[0] == n_tiles * tn + shift
    spb = tn // LANES
    return pl.pallas_call(
        functools.partial(_proj_body, n_aligned=n_aligned, shift=shift),
        grid=(n_tiles, t // tm),
        in_specs=[
            pl.BlockSpec((tm, d), lambda j, i: (i, 0)),
            pl.BlockSpec((tn, d), lambda j, i: (j, 0)),
            pl.BlockSpec((LANES, d), lambda j, i: ((j + 1) * spb, 0)),
        ],
        out_specs=pl.BlockSpec((spb, tm, LANES), lambda j, i: (j, i, 0)),
        out_shape=jax.ShapeDtypeStruct((n_tiles * spb, t, LANES), BF16),
        scratch_shapes=[pltpu.VMEM((tn, d), BF16)],
        compiler_params=pltpu.CompilerParams(
            dimension_semantics=("arbitrary", "arbitrary"), vmem_limit_bytes=VMEM_LIMIT),
        name="proj",
    )(h, wt, wt)


def _merge_body(og_ref, oa_ref, gg0_ref, gg1_ref, ga0_ref, ga1_ref, x_ref, wa_ref, wb_ref, wo_ref, nw_ref,
                o_ref, hn_ref):
    def slabs(*refs):
        return jnp.concatenate([ref[c] for ref in refs for c in range(ref.shape[0])], axis=-1)

    ya = _dot(slabs(og_ref), wa_ref[...])
    yb = _dot(slabs(oa_ref), wb_ref[...])
    m = (jax.nn.sigmoid(slabs(gg0_ref, gg1_ref).astype(F32)) * ya
         + jax.nn.sigmoid(slabs(ga0_ref, ga1_ref).astype(F32)) * yb)
    y = x_ref[...] + _dot(m.astype(BF16), wo_ref[...])
    o_ref[...] = y
    hn_ref[...] = _rms(y, nw_ref[...]).astype(BF16)


def _merge(og3, oa3, p3, gate_blk, x, wa, wb, wo, next_norm_w):
    t, d = x.shape
    tm = min(MERGE_TM, t)
    assert d == 2 * HEADS * LANES
    const = lambda i: (0, 0)
    row = pl.BlockSpec((tm, d), lambda i: (i, 0))
    slab = lambda blk: pl.BlockSpec((HEADS, tm, LANES), lambda i: (blk, i, 0))
    return pl.pallas_call(
        _merge_body,
        grid=(t // tm,),
        in_specs=[
            slab(0), slab(0),
            slab(gate_blk), slab(gate_blk + 1), slab(gate_blk + 2), slab(gate_blk + 3),
            row,
            pl.BlockSpec(wa.shape, const, pipeline_mode=pl.Buffered(1)),
            pl.BlockSpec(wb.shape, const, pipeline_mode=pl.Buffered(1)),
            pl.BlockSpec(wo.shape, const, pipeline_mode=pl.Buffered(1)),
            pl.BlockSpec((1, d), const),
        ],
        out_specs=[row, row],
        out_shape=[jax.ShapeDtypeStruct((t, d), F32), jax.ShapeDtypeStruct((t, d), BF16)],
        compiler_params=pltpu.CompilerParams(
            dimension_semantics=("parallel",), vmem_limit_bytes=VMEM_LIMIT),
        name="merge",
    )(og3, oa3, p3, p3, p3, p3, x, wa, wb, wo, next_norm_w.reshape(1, d))


def _att_body(q_ref, k_ref, v_ref, qw_ref, kw_ref, bvec_ref, w0_ref, w1_ref, w2_ref,
              o_ref, w0b_ref, w1b_ref, w2b_ref, kbuf, vbuf, bias_scr):
    blk = pl.program_id(0)
    rb = q_ref.shape[1]

    w0b_ref[...] = w0_ref[...].astype(BF16)
    w1b_ref[...] = w1_ref[...].astype(BF16)
    w2b_ref[...] = w2_ref[...].astype(BF16)

    @pl.when(blk == 0)
    def _():
        kbuf[:, :rb, :] = jnp.zeros((HEADS, rb, LANES), BF16)
        vbuf[:, :rb, :] = jnp.zeros((HEADS, rb, LANES), BF16)
        qc = lax.broadcasted_iota(jnp.int32, (ATT_QG, ATT_WIN), 0) // CHUNK
        kc = lax.broadcasted_iota(jnp.int32, (ATT_QG, ATT_WIN), 1) // CHUNK
        off = kc - qc
        band = jnp.where(off >= 0, jnp.where(off <= ATT_LEFT_CHUNKS, 0.0, NEG), NEG).astype(F32)
        for h in range(HEADS):
            tbl = jnp.broadcast_to(bvec_ref[h], (ATT_QG, ATT_TBL))
            tbl = pltpu.roll(tbl, 0, 1, stride=1, stride_axis=0)
            bias_scr[h] = tbl[:, :ATT_WIN] * LOG2E + band

    first_neg = jnp.where(blk == 0, NEG, 0.0).astype(F32)
    col = lax.broadcasted_iota(jnp.int32, (1, ATT_WIN), 1)
    qw = qw_ref[...] * (HEAD_DIM ** -0.5 * LOG2E)
    kw = kw_ref[...]
    groups = [(dh, g * ATT_QG) for dh in range(ATT_HEADS_PER_ITER) for g in range(rb // ATT_QG)]

    def heads(hi, carry):
        h0 = hi * ATT_HEADS_PER_ITER
        for dh in range(ATT_HEADS_PER_ITER):
            kbuf[h0 + dh, rb:, :] = _rms(k_ref[h0 + dh].astype(F32), kw).astype(BF16)
            vbuf[h0 + dh, rb:, :] = v_ref[h0 + dh]
        qn = [_rms(q_ref[h0 + dh, r0:r0 + ATT_QG, :].astype(F32), qw).astype(BF16) for dh, r0 in groups]
        s = [_dot_nt(qn[n], kbuf[h0 + dh, r0:r0 + ATT_WIN, :]) for n, (dh, r0) in enumerate(groups)]
        s = [s[n] + bias_scr[h0 + dh] + jnp.where(col < rb - r0, first_neg, 0.0)
             for n, (dh, r0) in enumerate(groups)]
        p = [jnp.exp2(sn - jnp.max(sn, axis=-1, keepdims=True)) for sn in s]
        l = [jnp.sum(pn, axis=-1, keepdims=True) for pn in p]
        o = [_dot(p[n].astype(BF16), vbuf[h0 + dh, r0:r0 + ATT_WIN, :]) for n, (dh, r0) in enumerate(groups)]
        for n, (dh, r0) in enumerate(groups):
            o_ref[h0 + dh, r0:r0 + ATT_QG, :] = (o[n] / l[n]).astype(BF16)
        return carry

    lax.fori_loop(0, HEADS // ATT_HEADS_PER_ITER, heads, 0)
    kbuf[:, :rb, :] = kbuf[:, rb:, :]
    vbuf[:, :rb, :] = vbuf[:, rb:, :]


def _att_bias_table(rel_bias):
    tbl = rel_bias.astype(F32)
    n_tbl = tbl.shape[1]
    far = ATT_LEFT_CHUNKS * CHUNK - REL_MAX
    rep = lambda c, n: jnp.broadcast_to(tbl[:, c:c + 1], (tbl.shape[0], n))
    vec = jnp.concatenate([rep(n_tbl - 1, far), tbl[:, ::-1],
                           rep(0, ATT_TBL - far - n_tbl - (ATT_QG - 1)), rep(n_tbl - 1, ATT_QG - 1)], axis=1)
    return vec[:, None, :]


def _att(p3, q_norm_w, k_norm_w, rel_bias, q_blk, k_blk, v_blk, side_weights):
    t = p3.shape[1]
    rb = SEQ_BLOCK
    assert rb == ATT_LEFT_CHUNKS * CHUNK and t % rb == 0
    nblk = t // rb
    bvec = _att_bias_table(rel_bias)
    assert all(w.shape[0] % (16 * nblk) == 0 for w in side_weights)
    side_specs = [pl.BlockSpec((w.shape[0] // nblk, w.shape[1]), lambda b: (b, 0)) for w in side_weights]
    outs = pl.pallas_call(
        _att_body,
        grid=(nblk,),
        in_specs=[
            pl.BlockSpec((HEADS, rb, LANES), lambda b: (q_blk, b, 0)),
            pl.BlockSpec((HEADS, rb, LANES), lambda b: (k_blk, b, 0)),
            pl.BlockSpec((HEADS, rb, LANES), lambda b: (v_blk, b, 0)),
            pl.BlockSpec((1, LANES), lambda b: (0, 0)),
            pl.BlockSpec((1, LANES), lambda b: (0, 0)),
            pl.BlockSpec(bvec.shape, lambda b: (0, 0, 0)),
        ] + side_specs,
        out_specs=[pl.BlockSpec((HEADS, rb, LANES), lambda b: (0, b, 0))] + side_specs,
        out_shape=[jax.ShapeDtypeStruct((HEADS, t, LANES), BF16)]
                  + [jax.ShapeDtypeStruct(w.shape, BF16) for w in side_weights],
        scratch_shapes=[pltpu.VMEM((HEADS, 2 * rb, LANES), BF16),
                        pltpu.VMEM((HEADS, 2 * rb, LANES), BF16),
                        pltpu.VMEM((HEADS, ATT_QG, ATT_WIN), F32)],
        compiler_params=pltpu.CompilerParams(
            dimension_semantics=("arbitrary",), vmem_limit_bytes=VMEM_LIMIT),
        name="att",
    )(p3, p3, p3, q_norm_w.reshape(1, LANES), k_norm_w.reshape(1, LANES), bvec, *side_weights)
    return outs[0], outs[1:]


def _split2(x):
    hi = x.astype(BF16)
    return hi, (x - hi.astype(F32)).astype(BF16)


def _gdn_body(q_ref, k_ref, v_ref, z_ref, ab_ref, cw_ref, alog_ref, dtb_ref, onw_ref, ls_ref, mx_ref,
              o_ref,
              s_scr, xp_scr, qn_scr, kn_scr, vn_scr, gb_scr, bb_scr, u_scr, wq_scr, aqk_scr, kw_scr,
              bm_scr, cd_scr):
    blk = pl.program_id(0)
    rows = q_ref.shape[1]
    npair = rows // PAIR
    hs = range(HEADS)

    @pl.when(blk == 0)
    def _():
        s_scr[...] = jnp.zeros_like(s_scr)
        xp_scr[:, 0:8, :] = jnp.zeros((3 * HEADS, 8, LANES), F32)

    ls = ls_ref[...]
    mx = mx_ref[...]
    ri = lax.broadcasted_iota(jnp.int32, (PAIR, PAIR), 0)
    ci = lax.broadcasted_iota(jnp.int32, (PAIR, PAIR), 1)
    same = (ri // CHUNK) == (ci // CHUNK)
    lower_f = jnp.where(same, jnp.where(ri >= ci, 1.0, 0.0), 0.0).astype(F32)
    strict_f = jnp.where(same, jnp.where(ri > ci, 1.0, 0.0), 0.0).astype(F32)
    eye = jnp.where(ri == ci, 1.0, 0.0).astype(F32)
    onw = onw_ref[...]
    alog, dtb = alog_ref[...], dtb_ref[...]

    def stage_a(p):
        r0 = p * PAIR
        sl = slice(r0, r0 + PAIR)
        ab = ab_ref[sl, :]
        xa = ab + dtb
        softplus = jnp.maximum(xa, 0.0) + jnp.log(1.0 + jnp.exp(-jnp.abs(xa)))
        g_all = -jnp.exp(alog) * softplus
        beta_all = jax.nn.sigmoid(ab)
        for h in hs:
            gb_scr[h, sl, :] = jnp.broadcast_to(g_all[:, h:h + 1], (PAIR, LANES))
            bb_scr[h, sl, :] = jnp.broadcast_to(beta_all[:, HEADS + h:HEADS + h + 1], (PAIR, LANES))
            outs = []
            for part, ref in enumerate((q_ref, k_ref, v_ref)):
                s = part * HEADS + h
                xp_scr[s, 8 + r0:8 + r0 + PAIR, :] = ref[h, sl, :].astype(F32)
                w = cw_ref[s]
                y = w[0:1] * xp_scr[s, 5 + r0:5 + r0 + PAIR, :]
                for tap in range(1, GDN_CONV):
                    y = y + w[tap:tap + 1] * xp_scr[s, 5 + tap + r0:5 + tap + r0 + PAIR, :]
                outs.append(y * jax.nn.sigmoid(y))
            qc, kc, vc = outs
            qn_scr[h, sl, :] = (qc * lax.rsqrt(jnp.sum(qc * qc, axis=-1, keepdims=True) + NORM_EPS)
                                * (HEAD_DIM ** -0.5))
            kn_scr[h, sl, :] = kc * lax.rsqrt(jnp.sum(kc * kc, axis=-1, keepdims=True) + NORM_EPS)
            vn_scr[h, sl, :] = vc

    def stage_b(p):
        sl = slice(p * PAIR, (p + 1) * PAIR)
        k = [kn_scr[h, sl, :] for h in hs]
        q = [qn_scr[h, sl, :] for h in hs]
        v = [vn_scr[h, sl, :] for h in hs]
        gb = [gb_scr[h, sl, :] for h in hs]
        bb = [bb_scr[h, sl, :] for h in hs]
        parts = [_split2(jnp.concatenate([gb[h], gb[h]], axis=1) * mx) for h in hs]
        gm = [_dot(ls, pt[0]) + _dot(ls, pt[1]) for pt in parts]
        yield
        decay = [jnp.exp(g[:, :PAIR]) for g in gm]
        gi = [g[:, PAIR:] for g in gm]
        glast = [jnp.concatenate([jnp.broadcast_to(g[c * CHUNK - 1:c * CHUNK, :], (CHUNK, LANES))
                                  for c in (1, 2)], axis=0) for g in gi]
        eg = [jnp.exp(g) for g in gi]
        er = [jnp.exp(glast[h] - gi[h]) for h in hs]
        kb = [k[h] * bb[h] for h in hs]
        kk = [_dot_nt(jnp.concatenate([kb[h], q[h]], axis=0).astype(BF16), k[h].astype(BF16)) for h in hs]
        yield
        n = [-(kk[h][:PAIR] * decay[h] * strict_f) for h in hs]
        aqk = [kk[h][PAIR:] * decay[h] * lower_f for h in hs]
        inv = [eye + n[h] for h in hs]
        nb = [n[h].astype(BF16) for h in hs]
        nb = [_dot(nb[h], nb[h]).astype(BF16) for h in hs]
        yield
        for it in range(5):
            if it < 4:
                m = [_dot(jnp.concatenate([inv[h].astype(BF16), nb[h]], axis=0), nb[h]) for h in hs]
                inv = [inv[h] + m[h][:PAIR] for h in hs]
                nb = [m[h][PAIR:].astype(BF16) for h in hs]
            else:
                inv = [inv[h] + _dot(inv[h].astype(BF16), nb[h]) for h in hs]
            yield
        uw = [_dot(inv[h].astype(BF16),
                   jnp.concatenate([v[h] * bb[h], kb[h] * eg[h]], axis=1).astype(BF16)) for h in hs]
        yield
        kd = [(k[h] * er[h]).astype(BF16) for h in hs]
        qg = [q[h] * eg[h] for h in hs]
        for cc in range(2):
            c = 2 * p + cc
            rs = slice(cc * CHUNK, (cc + 1) * CHUNK)
            kwb = [_dot_tn(kd[h][rs], jnp.concatenate([uw[h][rs, LANES:], uw[h][rs, :LANES]], axis=1).astype(BF16))
                   for h in hs]
            for h in hs:
                wq_scr[h, c] = jnp.concatenate([uw[h][rs, LANES:], qg[h][rs]], axis=0).astype(BF16)
                kw_scr[h, c] = kwb[h][:, :LANES].astype(BF16)
                bm_scr[h, c] = kwb[h][:, LANES:]
                cd_scr[h, c] = jnp.broadcast_to(eg[h][(cc + 1) * CHUNK - 1:(cc + 1) * CHUNK, :], (8, LANES))
            yield
        for h in hs:
            u_scr[h, sl, :] = uw[h][:, :LANES]
            aqk_scr[h, p] = aqk[h].astype(BF16)

    def stage_c(p):
        sl = slice(p * PAIR, (p + 1) * PAIR)
        c0, c1 = 2 * p, 2 * p + 1
        s0 = [s_scr[h] for h in hs]
        sb0 = [s0[h].astype(BF16) for h in hs]
        s1 = [s0[h] * cd_scr[h, c0][0:1, :] + (bm_scr[h, c0] - _dot(kw_scr[h, c0], sb0[h])) for h in hs]
        yield
        sb1 = [s1[h].astype(BF16) for h in hs]
        s2 = [s1[h] * cd_scr[h, c1][0:1, :] + (bm_scr[h, c1] - _dot(kw_scr[h, c1], sb1[h])) for h in hs]
        for h in hs:
            s_scr[h] = s2[h]
        yield
        a0 = [_dot(wq_scr[h, c0], sb0[h]) for h in hs]
        a1 = [_dot(wq_scr[h, c1], sb1[h]) for h in hs]
        yield
        vnew = [(u_scr[h, sl, :] - jnp.concatenate([a0[h][:CHUNK], a1[h][:CHUNK]], axis=0)).astype(BF16)
                for h in hs]
        o = [jnp.concatenate([a0[h][CHUNK:], a1[h][CHUNK:]], axis=0) + _dot(aqk_scr[h, p], vnew[h]) for h in hs]
        yield
        for h in hs:
            z = z_ref[h, sl, :].astype(F32)
            o_ref[h, sl, :] = (_rms(o[h], onw) * (z * jax.nn.sigmoid(z))).astype(BF16)

    def interleave(*gens):
        live = [g for g in gens if g is not None]
        while live:
            for g in list(live):
                try:
                    next(g)
                except StopIteration:
                    live.remove(g)

    stage_a(0)
    for step in range(npair + 1):
        if step + 1 < npair:
            stage_a(step + 1)
        interleave(stage_c(step - 1) if step >= 1 else None, stage_b(step) if step < npair else None)

    for s in range(3 * HEADS):
        xp_scr[s, 0:8, :] = xp_scr[s, rows:rows + 8, :]


def _gdn(p3, ab, conv_w, a_log, dt_bias, out_norm_w, q_blk):
    t = p3.shape[1]
    rows = min(SEQ_BLOCK, t)
    nchunk, npair = rows // CHUNK, rows // PAIR
    cw = conv_w.astype(F32).reshape(GDN_CONV, 3 * HEADS, LANES).transpose(1, 0, 2)
    pad = lambda v: jnp.zeros((1, LANES), F32).at[0, :HEADS].set(v.astype(F32))
    ti = jnp.arange(PAIR)
    same = (ti[:, None] // CHUNK) == (ti[None, :] // CHUNK)
    le = same & (ti[None, :] <= ti[:, None])
    gt = same & (ti[None, :] > ti[:, None])
    ls = le.astype(BF16)
    mx = jnp.concatenate([gt.T.astype(F32), jnp.ones((PAIR, LANES), F32)], axis=1)
    slab = lambda off: pl.BlockSpec((HEADS, rows, LANES), lambda b: (q_blk + off, b, 0))
    const2 = lambda b: (0, 0)
    hr = (HEADS, rows, LANES)
    return pl.pallas_call(
        _gdn_body,
        grid=(t // rows,),
        in_specs=[
            slab(0), slab(1), slab(2), slab(3),
            pl.BlockSpec((rows, LANES), lambda b: (b, 0)),
            pl.BlockSpec(cw.shape, lambda b: (0, 0, 0)),
            pl.BlockSpec((1, LANES), const2),
            pl.BlockSpec((1, LANES), const2),
            pl.BlockSpec((1, LANES), const2),
            pl.BlockSpec(ls.shape, const2),
            pl.BlockSpec(mx.shape, const2),
        ],
        out_specs=pl.BlockSpec(hr, lambda b: (0, b, 0)),
        out_shape=jax.ShapeDtypeStruct((HEADS, t, LANES), BF16),
        scratch_shapes=[
            pltpu.VMEM((HEADS, HEAD_DIM, HEAD_DIM), F32),
            pltpu.VMEM((3 * HEADS, rows + 8, LANES), F32),
            pltpu.VMEM(hr, F32), pltpu.VMEM(hr, F32), pltpu.VMEM(hr, F32),
            pltpu.VMEM(hr, F32), pltpu.VMEM(hr, F32),
            pltpu.VMEM(hr, F32),
            pltpu.VMEM((HEADS, nchunk, 2 * CHUNK, LANES), BF16),
            pltpu.VMEM((HEADS, npair, PAIR, PAIR), BF16),
            pltpu.VMEM((HEADS, nchunk, HEAD_DIM, HEAD_DIM), BF16),
            pltpu.VMEM((HEADS, nchunk, HEAD_DIM, HEAD_DIM), F32),
            pltpu.VMEM((HEADS, nchunk, 8, LANES), F32),
        ],
        compiler_params=pltpu.CompilerParams(
            dimension_semantics=("arbitrary",), vmem_limit_bytes=VMEM_LIMIT),
        name="gdn",
    )(p3, p3, p3, p3, ab, cw, pad(a_log), pad(dt_bias), out_norm_w.astype(F32).reshape(1, LANES), ls, mx)


def _pad_to(w, axis, mult):
    n = w.shape[axis]
    extra = (-n) % mult
    if extra == 0:
        return w
    widths = [(0, 0)] * w.ndim
    widths[axis] = (0, extra)
    return jnp.pad(w, widths)


@jax.jit
def _forward(x, ffn1_norm, ffn1_w_gate, ffn1_w_up, ffn1_w_down, mix_norm, w_in, gdn_conv,
             gdn_A_log, gdn_dt_bias, gdn_out_norm, att_q_norm, att_k_norm, att_rel_bias,
             w_branch_gdn, w_branch_att, w_out, ffn2_norm, ffn2_w_gate, ffn2_w_up, ffn2_w_down):
    b, t, d = x.shape
    gw = HEADS * HEAD_DIM
    outs = []
    for bi in range(b):
        xb = x[bi]
        depth = ffn1_norm.shape[0]
        for l in range(depth):
            wi = w_in[l]
            o_ab = 4 * gw
            o_att = o_ab + 2 * HEADS
            w_ab = _pad_to(wi[:, o_ab:o_att].astype(BF16), 1, LANES)

            act, wd = _ffn_up(_norm(xb, ffn1_norm[l]), ffn1_w_gate[l], ffn1_w_up[l], ffn1_w_down[l])
            xb, hn, ab = _ffn_down(act, wd, xb, mix_norm[l], w_ab)
            p3 = _proj(hn, wi.T, o_ab // PROJ_TN, o_att - o_ab, (wi.shape[1] - (o_att - o_ab)) // PROJ_TN)
            og3 = _gdn(p3, ab, gdn_conv[l], gdn_A_log[l], gdn_dt_bias[l], gdn_out_norm[l], 0)
            oa3, (wa, wb, wo) = _att(p3, att_q_norm[l], att_k_norm[l], att_rel_bias[l], 4, 5, 6,
                                     (w_branch_gdn[l], w_branch_att[l], w_out[l]))
            xb, hn = _merge(og3, oa3, p3, 7, xb, wa, wb, wo, ffn2_norm[l])

            act, wd = _ffn_up(hn, ffn2_w_gate[l], ffn2_w_up[l], ffn2_w_down[l])
            xb = _ffn_down(act, wd, xb)
        outs.append(xb)
    return jnp.stack(outs, axis=0)


def kernel(x, ffn1_norm, ffn1_w_gate, ffn1_w_up, ffn1_w_down, mix_norm, w_in, gdn_conv, gdn_A_log, gdn_dt_bias, gdn_out_norm, att_q_norm, att_k_norm, att_rel_bias, w_branch_gdn, w_branch_att, w_out, ffn2_norm, ffn2_w_gate, ffn2_w_up, ffn2_w_down):
    return _forward(x, ffn1_norm, ffn1_w_gate, ffn1_w_up, ffn1_w_down, mix_norm, w_in, gdn_conv,
                    gdn_A_log, gdn_dt_bias, gdn_out_norm, att_q_norm, att_k_norm, att_rel_bias,
                    w_branch_gdn, w_branch_att, w_out, ffn2_norm, ffn2_w_gate, ffn2_w_up, ffn2_w_down)
```

```python
import functools
import math

import jax
import jax.numpy as jnp
from jax import lax
from jax.experimental import pallas as pl
from jax.experimental.pallas import tpu as pltpu

F32 = jnp.float32
BF16 = jnp.bfloat16

NORM_EPS = 1e-6
CHUNK = 64
HEADS = 8
HEAD_DIM = 128
LANES = 128
GDN_CONV = 4
ATT_LEFT_CHUNKS = 8
REL_MAX = 256
NEG = -1e30

VMEM_LIMIT = 56 * 1024 * 1024

LOG2E = math.log2(math.e)

NORM_TM = 512
FFN_UP_TM = 1024
FFN_TF = 512
FFN_DOWN_TM = 512
FFN_DOWN_NORM_TM = 512
FFN_DOWN_NORM_VMEM = 62 * 1024 * 1024
ROW_GROUP = 256
PROJ_TM = 1024
PROJ_TN = 1024
ATT_HEADS_PER_ITER = 2
MERGE_TM = 512
SEQ_BLOCK = 512
ATT_QG = 256
ATT_WIN = ATT_QG + ATT_LEFT_CHUNKS * CHUNK
ATT_TBL = 1024
PAIR = 2 * CHUNK
GDN_HEAD_GROUP = 8


def _rms(x, w):
    return x * lax.rsqrt(jnp.mean(x * x, axis=-1, keepdims=True) + NORM_EPS) * w


def _dot(a, b):
    return jnp.dot(a, b, preferred_element_type=F32)


def _dot_nt(a, b):
    return lax.dot_general(a, b, (((1,), (1,)), ((), ())), preferred_element_type=F32)


def _dot_tn(a, b):
    return lax.dot_general(a, b, (((0,), (0,)), ((), ())), preferred_element_type=F32)


def _norm_body(x_ref, nw_ref, o_ref):
    o_ref[...] = _rms(x_ref[...], nw_ref[...]).astype(BF16)


def _norm(x, norm_w):
    t, d = x.shape
    tm = min(NORM_TM, t)
    return pl.pallas_call(
        _norm_body,
        grid=(t // tm,),
        in_specs=[pl.BlockSpec((tm, d), lambda i: (i, 0)), pl.BlockSpec((1, d), lambda i: (0, 0))],
        out_specs=pl.BlockSpec((tm, d), lambda i: (i, 0)),
        out_shape=jax.ShapeDtypeStruct((t, d), BF16),
        compiler_params=pltpu.CompilerParams(dimension_semantics=("parallel",), vmem_limit_bytes=VMEM_LIMIT),
        name="norm",
    )(x, norm_w.reshape(1, d))


def _ffn_up_body(h_ref, wg_ref, wu_ref, wd_ref, o_ref, wdb_ref, w_scr, *, n_side):
    tf = wg_ref.shape[1]

    @pl.when(pl.program_id(1) == 0)
    def _():
        w_scr[:, :tf] = wg_ref[...].astype(BF16)
        w_scr[:, tf:] = wu_ref[...].astype(BF16)

    @pl.when(pl.program_id(0) * pl.num_programs(1) + pl.program_id(1) < n_side)
    def _():
        wdb_ref[...] = wd_ref[...].astype(BF16)

    gu = _dot(h_ref[...], w_scr[...])
    g, u = gu[:, :tf], gu[:, tf:]
    o_ref[...] = (g * jax.nn.sigmoid(g) * u).astype(BF16)


def _ffn_up(h, wg, wu, wd):
    t, d = h.shape
    f = wg.shape[1]
    tm, tf = min(FFN_UP_TM, t), FFN_TF
    nj, ni = pl.cdiv(f, tf), t // tm
    rows = next(r for r in range(16, f + 1, 16) if f % r == 0 and f // r <= nj * ni)
    n_side = f // rows
    side = pl.BlockSpec((rows, d), lambda j, i: (jnp.minimum(j * ni + i, n_side - 1), 0))
    return pl.pallas_call(
        functools.partial(_ffn_up_body, n_side=n_side),
        grid=(nj, ni),
        in_specs=[
            pl.BlockSpec((tm, d), lambda j, i: (i, 0)),
            pl.BlockSpec((d, tf), lambda j, i: (0, j)),
            pl.BlockSpec((d, tf), lambda j, i: (0, j)),
            side,
        ],
        out_specs=[pl.BlockSpec((tm, tf), lambda j, i: (i, j)), side],
        out_shape=[jax.ShapeDtypeStruct((t, f), BF16), jax.ShapeDtypeStruct((f, d), BF16)],
        scratch_shapes=[pltpu.VMEM((d, 2 * tf), BF16)],
        compiler_params=pltpu.CompilerParams(
            dimension_semantics=("arbitrary", "arbitrary"), vmem_limit_bytes=VMEM_LIMIT),
        name="ffn_up",
    )(h, wg, wu, wd)


def _ffn_down_body(a_ref, wd_ref, x_ref, o_ref):
    o_ref[...] = x_ref[...] + 0.5 * _dot(a_ref[...], wd_ref[...])


def _ffn_down_norm_body(a_ref, wd_ref, x_ref, nw_ref, ws_ref, o_ref, hn_ref, side_ref):
    rows = a_ref.shape[0]
    group = min(ROW_GROUP, rows)
    for r0 in range(0, rows, group):
        rs = slice(r0, r0 + group)
        y = x_ref[rs, :] + 0.5 * _dot(a_ref[rs, :], wd_ref[...])
        o_ref[rs, :] = y
        hn = _rms(y, nw_ref[...]).astype(BF16)
        hn_ref[rs, :] = hn
        side_ref[rs, :] = _dot(hn, ws_ref[...])


def _ffn_down(act, wd, x, next_norm_w=None, w_side=None):
    t, d = x.shape
    f = act.shape[1]
    tm = min(FFN_DOWN_TM if next_norm_w is None else FFN_DOWN_NORM_TM, t)
    row = pl.BlockSpec((tm, d), lambda i: (i, 0))
    in_specs = [pl.BlockSpec((tm, f), lambda i: (i, 0)),
                pl.BlockSpec((f, d), lambda i: (0, 0), pipeline_mode=pl.Buffered(1)),
                row]
    params = pltpu.CompilerParams(dimension_semantics=("parallel",), vmem_limit_bytes=VMEM_LIMIT)
    if next_norm_w is None:
        assert w_side is None
        return pl.pallas_call(
            _ffn_down_body, grid=(t // tm,), in_specs=in_specs, out_specs=row,
            out_shape=jax.ShapeDtypeStruct((t, d), F32), compiler_params=params, name="ffn_down",
        )(act, wd, x)
    return pl.pallas_call(
        _ffn_down_norm_body, grid=(t // tm,),
        in_specs=in_specs + [pl.BlockSpec((1, d), lambda i: (0, 0)), pl.BlockSpec((d, LANES), lambda i: (0, 0))],
        out_specs=[row, row, pl.BlockSpec((tm, LANES), lambda i: (i, 0))],
        out_shape=[jax.ShapeDtypeStruct((t, d), F32), jax.ShapeDtypeStruct((t, d), BF16),
                   jax.ShapeDtypeStruct((t, LANES), F32)],
        compiler_params=pltpu.CompilerParams(dimension_semantics=("parallel",),
                                             vmem_limit_bytes=FFN_DOWN_NORM_VMEM),
        name="ffn_down_norm",
    )(act, wd, x, next_norm_w.reshape(1, d), w_side)


def _proj_body(h_ref, wf_ref, wx_ref, p_ref, w_scr, *, n_aligned, shift):
    j, i = pl.program_id(0), pl.program_id(1)
    tn = wf_ref.shape[0]

    @pl.when(jnp.logical_and(i == 0, j < n_aligned))
    def _():
        w_scr[...] = wf_ref[...].astype(BF16)

    @pl.when(jnp.logical_and(i == 0, j >= n_aligned))
    def _():
        w_scr[:tn - shift, :] = wf_ref[shift:, :].astype(BF16)
        w_scr[tn - shift:, :] = wx_ref[:shift, :].astype(BF16)

    r = _dot_nt(h_ref[...], w_scr[...])
    for c in range(p_ref.shape[0]):
        p_ref[c] = r[:, c * LANES:(c + 1) * LANES].astype(BF16)


def _proj(h, wt, n_aligned, shift, n_tiles):
    t, d = h.shape
    tm, tn = min(PROJ_TM, t), PROJ_TN
    assert 0 < shift < LANES and shift % 16 == 0 and wt.shape[0] == n_tiles * tn + shift
    spb = tn // LANES
    return pl.pallas_call(
        functools.partial(_proj_body, n_aligned=n_aligned, shift=shift),
        grid=(n_tiles, t // tm),
        in_specs=[
            pl.BlockSpec((tm, d), lambda j, i: (i, 0)),
            pl.BlockSpec((tn, d), lambda j, i: (j, 0)),
            pl.BlockSpec((LANES, d), lambda j, i: ((j + 1) * spb, 0)),
        ],
        out_specs=pl.BlockSpec((spb, tm, LANES), lambda j, i: (j, i, 0)),
        out_shape=jax.ShapeDtypeStruct((n_tiles * spb, t, LANES), BF16),
        scratch_shapes=[pltpu.VMEM((tn, d), BF16)],
        compiler_params=pltpu.CompilerParams(
            dimension_semantics=("arbitrary", "arbitrary"), vmem_limit_bytes=VMEM_LIMIT),
        name="proj",
    )(h, wt, wt)


def _merge_body(og_ref, oa_ref, gg0_ref, gg1_ref, ga0_ref, ga1_ref, x_ref, wa_ref, wb_ref, wo_ref, nw_ref,
                o_ref, hn_ref):
    def slabs(*refs):
        return jnp.concatenate([ref[c] for ref in refs for c in range(ref.shape[0])], axis=-1)

    ya = _dot(slabs(og_ref), wa_ref[...])
    yb = _dot(slabs(oa_ref), wb_ref[...])
    m = (jax.nn.sigmoid(slabs(gg0_ref, gg1_ref).astype(F32)) * ya
         + jax.nn.sigmoid(slabs(ga0_ref, ga1_ref).astype(F32)) * yb)
    y = x_ref[...] + _dot(m.astype(BF16), wo_ref[...])
    o_ref[...] = y
    hn_ref[...] = _rms(y, nw_ref[...]).astype(BF16)


def _merge(og3, oa3, p3, gate_blk, x, wa, wb, wo, next_norm_w):
    t, d = x.shape
    tm = min(MERGE_TM, t)
    assert d == 2 * HEADS * LANES
    const = lambda i: (0, 0)
    row = pl.BlockSpec((tm, d), lambda i: (i, 0))
    slab = lambda blk: pl.BlockSpec((HEADS, tm, LANES), lambda i: (blk, i, 0))
    return pl.pallas_call(
        _merge_body,
        grid=(t // tm,),
        in_specs=[
            slab(0), slab(0),
            slab(gate_blk), slab(gate_blk + 1), slab(gate_blk + 2), slab(gate_blk + 3),
            row,
            pl.BlockSpec(wa.shape, const, pipeline_mode=pl.Buffered(1)),
            pl.BlockSpec(wb.shape, const, pipeline_mode=pl.Buffered(1)),
            pl.BlockSpec(wo.shape, const, pipeline_mode=pl.Buffered(1)),
            pl.BlockSpec((1, d), const),
        ],
        out_specs=[row, row],
        out_shape=[jax.ShapeDtypeStruct((t, d), F32), jax.ShapeDtypeStruct((t, d), BF16)],
        compiler_params=pltpu.CompilerParams(
            dimension_semantics=("parallel",), vmem_limit_bytes=VMEM_LIMIT),
        name="merge",
    )(og3, oa3, p3, p3, p3, p3, x, wa, wb, wo, next_norm_w.reshape(1, d))


def _att_body(q_ref, k_ref, v_ref, qw_ref, kw_ref, bvec_ref, w0_ref, w1_ref, w2_ref,
              o_ref, w0b_ref, w1b_ref, w2b_ref, kbuf, vbuf, bias_scr):
    blk = pl.program_id(0)
    rb = q_ref.shape[1]

    w0b_ref[...] = w0_ref[...].astype(BF16)
    w1b_ref[...] = w1_ref[...].astype(BF16)
    w2b_ref[...] = w2_ref[...].astype(BF16)

    @pl.when(blk == 0)
    def _():
        kbuf[:, :rb, :] = jnp.zeros((HEADS, rb, LANES), BF16)
        vbuf[:, :rb, :] = jnp.zeros((HEADS, rb, LANES), BF16)
        qc = lax.broadcasted_iota(jnp.int32, (ATT_QG, ATT_WIN), 0) // CHUNK
        kc = lax.broadcasted_iota(jnp.int32, (ATT_QG, ATT_WIN), 1) // CHUNK
        off = kc - qc
        band = jnp.where(off >= 0, jnp.where(off <= ATT_LEFT_CHUNKS, 0.0, NEG), NEG).astype(F32)
        for h in range(HEADS):
            tbl = jnp.broadcast_to(bvec_ref[h], (ATT_QG, ATT_TBL))
            tbl = pltpu.roll(tbl, 0, 1, stride=1, stride_axis=0)
            bias_scr[h] = tbl[:, :ATT_WIN] * LOG2E + band

    first_neg = jnp.where(blk == 0, NEG, 0.0).astype(F32)
    col = lax.broadcasted_iota(jnp.int32, (1, ATT_WIN), 1)
    qw = qw_ref[...] * (HEAD_DIM ** -0.5 * LOG2E)
    kw = kw_ref[...]
    groups = [(dh, g * ATT_QG) for dh in range(ATT_HEADS_PER_ITER) for g in range(rb // ATT_QG)]

    def heads(hi, carry):
        h0 = hi * ATT_HEADS_PER_ITER
        for dh in range(ATT_HEADS_PER_ITER):
            kbuf[h0 + dh, rb:, :] = _rms(k_ref[h0 + dh].astype(F32), kw).astype(BF16)
            vbuf[h0 + dh, rb:, :] = v_ref[h0 + dh]
        qn = [_rms(q_ref[h0 + dh, r0:r0 + ATT_QG, :].astype(F32), qw).astype(BF16) for dh, r0 in groups]
        s = [_dot_nt(qn[n], kbuf[h0 + dh, r0:r0 + ATT_WIN, :]) for n, (dh, r0) in enumerate(groups)]
        s = [s[n] + bias_scr[h0 + dh] + jnp.where(col < rb - r0, first_neg, 0.0)
             for n, (dh, r0) in enumerate(groups)]
        p = [jnp.exp2(sn - jnp.max(sn, axis=-1, keepdims=True)) for sn in s]
        l = [jnp.sum(pn, axis=-1, keepdims=True) for pn in p]
        o = [_dot(p[n].astype(BF16), vbuf[h0 + dh, r0:r0 + ATT_WIN, :]) for n, (dh, r0) in enumerate(groups)]
        for n, (dh, r0) in enumerate(groups):
            o_ref[h0 + dh, r0:r0 + ATT_QG, :] = (o[n] / l[n]).astype(BF16)
        return carry

    lax.fori_loop(0, HEADS // ATT_HEADS_PER_ITER, heads, 0)
    kbuf[:, :rb, :] = kbuf[:, rb:, :]
    vbuf[:, :rb, :] = vbuf[:, rb:, :]


def _att_bias_table(rel_bias):
    tbl = rel_bias.astype(F32)
    n_tbl = tbl.shape[1]
    far = ATT_LEFT_CHUNKS * CHUNK - REL_MAX
    rep = lambda c, n: jnp.broadcast_to(tbl[:, c:c + 1], (tbl.shape[0], n))
    vec = jnp.concatenate([rep(n_tbl - 1, far), tbl[:, ::-1],
                           rep(0, ATT_TBL - far - n_tbl - (ATT_QG - 1)), rep(n_tbl - 1, ATT_QG - 1)], axis=1)
    return vec[:, None, :]


def _att(p3, q_norm_w, k_norm_w, rel_bias, q_blk, k_blk, v_blk, side_weights):
    t = p3.shape[1]
    rb = SEQ_BLOCK
    assert rb == ATT_LEFT_CHUNKS * CHUNK and t % rb == 0
    nblk = t // rb
    bvec = _att_bias_table(rel_bias)
    assert all(w.shape[0] % (16 * nblk) == 0 for w in side_weights)
    side_specs = [pl.BlockSpec((w.shape[0] // nblk, w.shape[1]), lambda b: (b, 0)) for w in side_weights]
    outs = pl.pallas_call(
        _att_body,
        grid=(nblk,),
        in_specs=[
            pl.BlockSpec((HEADS, rb, LANES), lambda b: (q_blk, b, 0)),
            pl.BlockSpec((HEADS, rb, LANES), lambda b: (k_blk, b, 0)),
            pl.BlockSpec((HEADS, rb, LANES), lambda b: (v_blk, b, 0)),
            pl.BlockSpec((1, LANES), lambda b: (0, 0)),
            pl.BlockSpec((1, LANES), lambda b: (0, 0)),
            pl.BlockSpec(bvec.shape, lambda b: (0, 0, 0)),
        ] + side_specs,
        out_specs=[pl.BlockSpec((HEADS, rb, LANES), lambda b: (0, b, 0))] + side_specs,
        out_shape=[jax.ShapeDtypeStruct((HEADS, t, LANES), BF16)]
                  + [jax.ShapeDtypeStruct(w.shape, BF16) for w in side_weights],
        scratch_shapes=[pltpu.VMEM((HEADS, 2 * rb, LANES), BF16),
                        pltpu.VMEM((HEADS, 2 * rb, LANES), BF16),
                        pltpu.VMEM((HEADS, ATT_QG, ATT_WIN), F32)],
        compiler_params=pltpu.CompilerParams(
            dimension_semantics=("arbitrary",), vmem_limit_bytes=VMEM_LIMIT),
        name="att",
    )(p3, p3, p3, q_norm_w.reshape(1, LANES), k_norm_w.reshape(1, LANES), bvec, *side_weights)
    return outs[0], outs[1:]


def _split2(x):
    hi = x.astype(BF16)
    return hi, (x - hi.astype(F32)).astype(BF16)


def _gdn_body(q_ref, k_ref, v_ref, z_ref, ab_ref, cw_ref, alog_ref, dtb_ref, onw_ref, ls_ref, mx_ref,
              o_ref,
              s_scr, xp_scr, qn_scr, kn_scr, vn_scr, gb_scr, bb_scr, u_scr, wq_scr, aqk_scr, kw_scr,
              bm_scr, cd_scr):
    blk = pl.program_id(0)
    rows = q_ref.shape[1]
    npair = rows // PAIR
    hs = range(HEADS)
    hd = tuple(hs)

    @pl.when(blk == 0)
    def _():
        s_scr[...] = jnp.zeros_like(s_scr)
        xp_scr[:, 0:8, :] = jnp.zeros((3 * HEADS, 8, LANES), F32)

    ls = ls_ref[...]
    mx = mx_ref[...]
    ri = lax.broadcasted_iota(jnp.int32, (PAIR, PAIR), 0)
    ci = lax.broadcasted_iota(jnp.int32, (PAIR, PAIR), 1)
    same = (ri // CHUNK) == (ci // CHUNK)
    lower_f = jnp.where(same, jnp.where(ri >= ci, 1.0, 0.0), 0.0).astype(F32)
    strict_f = jnp.where(same, jnp.where(ri > ci, 1.0, 0.0), 0.0).astype(F32)
    eye = jnp.where(ri == ci, 1.0, 0.0).astype(F32)
    onw = onw_ref[...]
    alog, dtb = alog_ref[...], dtb_ref[...]

    def stage_a(p):
        r0 = p * PAIR
        sl = slice(r0, r0 + PAIR)
        ab = ab_ref[sl, :]
        xa = ab + dtb
        softplus = jnp.maximum(xa, 0.0) + jnp.log(1.0 + jnp.exp(-jnp.abs(xa)))
        g_all = -jnp.exp(alog) * softplus
        beta_all = jax.nn.sigmoid(ab)
        for h in hs:
            gb_scr[hd[h], sl, :] = jnp.broadcast_to(g_all[:, h:h + 1], (PAIR, LANES))
            bb_scr[hd[h], sl, :] = jnp.broadcast_to(beta_all[:, HEADS + h:HEADS + h + 1], (PAIR, LANES))
            outs = []
            for part, ref in enumerate((q_ref, k_ref, v_ref)):
                s = part * HEADS + h
                xp_scr[s, 8 + r0:8 + r0 + PAIR, :] = ref[h, sl, :].astype(F32)
                w = cw_ref[s]
                y = w[0:1] * xp_scr[s, 5 + r0:5 + r0 + PAIR, :]
                for tap in range(1, GDN_CONV):
                    y = y + w[tap:tap + 1] * xp_scr[s, 5 + tap + r0:5 + tap + r0 + PAIR, :]
                outs.append(y * jax.nn.sigmoid(y))
            qc, kc, vc = outs
            qn_scr[hd[h], sl, :] = (qc * lax.rsqrt(jnp.sum(qc * qc, axis=-1, keepdims=True) + NORM_EPS)
                                * (HEAD_DIM ** -0.5))
            kn_scr[hd[h], sl, :] = kc * lax.rsqrt(jnp.sum(kc * kc, axis=-1, keepdims=True) + NORM_EPS)
            vn_scr[hd[h], sl, :] = vc

    def stage_b(p, hd):
        hs = range(len(hd))
        sl = slice(p * PAIR, (p + 1) * PAIR)
        k = [kn_scr[hd[h], sl, :] for h in hs]
        q = [qn_scr[hd[h], sl, :] for h in hs]
        v = [vn_scr[hd[h], sl, :] for h in hs]
        gb = [gb_scr[hd[h], sl, :] for h in hs]
        bb = [bb_scr[hd[h], sl, :] for h in hs]
        parts = [_split2(jnp.concatenate([gb[h], gb[h]], axis=1) * mx) for h in hs]
        gm = [_dot(ls, pt[0]) + _dot(ls, pt[1]) for pt in parts]
        yield
        decay = [jnp.exp(g[:, :PAIR]) for g in gm]
        gi = [g[:, PAIR:] for g in gm]
        glast = [jnp.concatenate([jnp.broadcast_to(g[c * CHUNK - 1:c * CHUNK, :], (CHUNK, LANES))
                                  for c in (1, 2)], axis=0) for g in gi]
        eg = [jnp.exp(g) for g in gi]
        er = [jnp.exp(glast[h] - gi[h]) for h in hs]
        kb = [k[h] * bb[h] for h in hs]
        kk = [_dot_nt(jnp.concatenate([kb[h], q[h]], axis=0).astype(BF16), k[h].astype(BF16)) for h in hs]
        yield
        n = [-(kk[h][:PAIR] * decay[h] * strict_f) for h in hs]
        aqk = [kk[h][PAIR:] * decay[h] * lower_f for h in hs]
        inv = [eye + n[h] for h in hs]
        nb = [n[h].astype(BF16) for h in hs]
        nb = [_dot(nb[h], nb[h]).astype(BF16) for h in hs]
        yield
        for it in range(5):
            if it < 4:
                m = [_dot(jnp.concatenate([inv[h].astype(BF16), nb[h]], axis=0), nb[h]) for h in hs]
                inv = [inv[h] + m[h][:PAIR] for h in hs]
                nb = [m[h][PAIR:].astype(BF16) for h in hs]
            else:
                inv = [inv[h] + _dot(inv[h].astype(BF16), nb[h]) for h in hs]
            yield
        uw = [_dot(inv[h].astype(BF16),
                   jnp.concatenate([v[h] * bb[h], kb[h] * eg[h]], axis=1).astype(BF16)) for h in hs]
        yield
        kd = [(k[h] * er[h]).astype(BF16) for h in hs]
        qg = [q[h] * eg[h] for h in hs]
        for cc in range(2):
            c = 2 * p + cc
            rs = slice(cc * CHUNK, (cc + 1) * CHUNK)
            kwb = [_dot_tn(kd[h][rs], jnp.concatenate([uw[h][rs, LANES:], uw[h][rs, :LANES]], axis=1).astype(BF16))
                   for h in hs]
            for h in hs:
                wq_scr[hd[h], c] = jnp.concatenate([uw[h][rs, LANES:], qg[h][rs]], axis=0).astype(BF16)
                kw_scr[hd[h], c] = kwb[h][:, :LANES].astype(BF16)
                bm_scr[hd[h], c] = kwb[h][:, LANES:]
                cd_scr[hd[h], c] = jnp.broadcast_to(eg[h][(cc + 1) * CHUNK - 1:(cc + 1) * CHUNK, :], (8, LANES))
            yield
        for h in hs:
            u_scr[hd[h], sl, :] = uw[h][:, :LANES]
            aqk_scr[hd[h], p] = aqk[h].astype(BF16)

    def stage_c(p, hd):
        hs = range(len(hd))
        sl = slice(p * PAIR, (p + 1) * PAIR)
        c0, c1 = 2 * p, 2 * p + 1
        s0 = [s_scr[hd[h]] for h in hs]
        sb0 = [s0[h].astype(BF16) for h in hs]
        s1 = [s0[h] * cd_scr[hd[h], c0][0:1, :] + (bm_scr[hd[h], c0] - _dot(kw_scr[hd[h], c0], sb0[h])) for h in hs]
        yield
        sb1 = [s1[h].astype(BF16) for h in hs]
        s2 = [s1[h] * cd_scr[hd[h], c1][0:1, :] + (bm_scr[hd[h], c1] - _dot(kw_scr[hd[h], c1], sb1[h])) for h in hs]
        for h in hs:
            s_scr[hd[h]] = s2[h]
        yield
        a0 = [_dot(wq_scr[hd[h], c0], sb0[h]) for h in hs]
        a1 = [_dot(wq_scr[hd[h], c1], sb1[h]) for h in hs]
        yield
        vnew = [(u_scr[hd[h], sl, :] - jnp.concatenate([a0[h][:CHUNK], a1[h][:CHUNK]], axis=0)).astype(BF16)
                for h in hs]
        o = [jnp.concatenate([a0[h][CHUNK:], a1[h][CHUNK:]], axis=0) + _dot(aqk_scr[hd[h], p], vnew[h]) for h in hs]
        yield
        for h in hs:
            z = z_ref[hd[h], sl, :].astype(F32)
            o_ref[hd[h], sl, :] = (_rms(o[h], onw) * (z * jax.nn.sigmoid(z))).astype(BF16)

    def interleave(*gens):
        live = [g for g in gens if g is not None]
        while live:
            for g in list(live):
                try:
                    next(g)
                except StopIteration:
                    live.remove(g)

    stage_a(0)
    for step in range(npair + 1):
        if step + 1 < npair:
            stage_a(step + 1)
        for g0 in range(0, HEADS, GDN_HEAD_GROUP):
            grp = tuple(range(g0, g0 + GDN_HEAD_GROUP))
            interleave(stage_c(step - 1, grp) if step >= 1 else None, stage_b(step, grp) if step < npair else None)

    for s in range(3 * HEADS):
        xp_scr[s, 0:8, :] = xp_scr[s, rows:rows + 8, :]


def _gdn(p3, ab, conv_w, a_log, dt_bias, out_norm_w, q_blk):
    t = p3.shape[1]
    rows = min(SEQ_BLOCK, t)
    nchunk, npair = rows // CHUNK, rows // PAIR
    cw = conv_w.astype(F32).reshape(GDN_CONV, 3 * HEADS, LANES).transpose(1, 0, 2)
    pad = lambda v: jnp.zeros((1, LANES), F32).at[0, :HEADS].set(v.astype(F32))
    ti = jnp.arange(PAIR)
    same = (ti[:, None] // CHUNK) == (ti[None, :] // CHUNK)
    le = same & (ti[None, :] <= ti[:, None])
    gt = same & (ti[None, :] > ti[:, None])
    ls = le.astype(BF16)
    mx = jnp.concatenate([gt.T.astype(F32), jnp.ones((PAIR, LANES), F32)], axis=1)
    slab = lambda off: pl.BlockSpec((HEADS, rows, LANES), lambda b: (q_blk + off, b, 0))
    const2 = lambda b: (0, 0)
    hr = (HEADS, rows, LANES)
    return pl.pallas_call(
        _gdn_body,
        grid=(t // rows,),
        in_specs=[
            slab(0), slab(1), slab(2), slab(3),
            pl.BlockSpec((rows, LANES), lambda b: (b, 0)),
            pl.BlockSpec(cw.shape, lambda b: (0, 0, 0)),
            pl.BlockSpec((1, LANES), const2),
            pl.BlockSpec((1, LANES), const2),
            pl.BlockSpec((1, LANES), const2),
            pl.BlockSpec(ls.shape, const2),
            pl.BlockSpec(mx.shape, const2),
        ],
        out_specs=pl.BlockSpec(hr, lambda b: (0, b, 0)),
        out_shape=jax.ShapeDtypeStruct((HEADS, t, LANES), BF16),
        scratch_shapes=[
            pltpu.VMEM((HEADS, HEAD_DIM, HEAD_DIM), F32),
            pltpu.VMEM((3 * HEADS, rows + 8, LANES), F32),
            pltpu.VMEM(hr, F32), pltpu.VMEM(hr, F32), pltpu.VMEM(hr, F32),
            pltpu.VMEM(hr, F32), pltpu.VMEM(hr, F32),
            pltpu.VMEM(hr, F32),
            pltpu.VMEM((HEADS, nchunk, 2 * CHUNK, LANES), BF16),
            pltpu.VMEM((HEADS, npair, PAIR, PAIR), BF16),
            pltpu.VMEM((HEADS, nchunk, HEAD_DIM, HEAD_DIM), BF16),
            pltpu.VMEM((HEADS, nchunk, HEAD_DIM, HEAD_DIM), F32),
            pltpu.VMEM((HEADS, nchunk, 8, LANES), F32),
        ],
        compiler_params=pltpu.CompilerParams(
            dimension_semantics=("arbitrary",), vmem_limit_bytes=VMEM_LIMIT),
        name="gdn",
    )(p3, p3, p3, p3, ab, cw, pad(a_log), pad(dt_bias), out_norm_w.astype(F32).reshape(1, LANES), ls, mx)


def _pad_to(w, axis, mult):
    n = w.shape[axis]
    extra = (-n) % mult
    if extra == 0:
        return w
    widths = [(0, 0)] * w.ndim
    widths[axis] = (0, extra)
    return jnp.pad(w, widths)


@jax.jit
def _forward(x, ffn1_norm, ffn1_w_gate, ffn1_w_up, ffn1_w_down, mix_norm, w_in, gdn_conv,
             gdn_A_log, gdn_dt_bias, gdn_out_norm, att_q_norm, att_k_norm, att_rel_bias,
             w_branch_gdn, w_branch_att, w_out, ffn2_norm, ffn2_w_gate, ffn2_w_up, ffn2_w_down):
    b, t, d = x.shape
    gw = HEADS * HEAD_DIM
    outs = []
    for bi in range(b):
        xb = x[bi]
        depth = ffn1_norm.shape[0]
        for l in range(depth):
            wi = w_in[l]
            o_ab = 4 * gw
            o_att = o_ab + 2 * HEADS
            w_ab = _pad_to(wi[:, o_ab:o_att].astype(BF16), 1, LANES)

            act, wd = _ffn_up(_norm(xb, ffn1_norm[l]), ffn1_w_gate[l], ffn1_w_up[l], ffn1_w_down[l])
            xb, hn, ab = _ffn_down(act, wd, xb, mix_norm[l], w_ab)
            p3 = _proj(hn, wi.T, o_ab // PROJ_TN, o_att - o_ab, (wi.shape[1] - (o_att - o_ab)) // PROJ_TN)
            og3 = _gdn(p3, ab, gdn_conv[l], gdn_A_log[l], gdn_dt_bias[l], gdn_out_norm[l], 0)
            oa3, (wa, wb, wo) = _att(p3, att_q_norm[l], att_k_norm[l], att_rel_bias[l], 4, 5, 6,
                                     (w_branch_gdn[l], w_branch_att[l], w_out[l]))
            xb, hn = _merge(og3, oa3, p3, 7, xb, wa, wb, wo, ffn2_norm[l])

            act, wd = _ffn_up(hn, ffn2_w_gate[l], ffn2_w_up[l], ffn2_w_down[l])
            xb = _ffn_down(act, wd, xb)
        outs.append(xb)
    return jnp.stack(outs, axis=0)


def kernel(x, ffn1_norm, ffn1_w_gate, ffn1_w_up, ffn1_w_down, mix_norm, w_in, gdn_conv, gdn_A_log, gdn_dt_bias, gdn_out_norm, att_q_norm, att_k_norm, att_rel_bias, w_branch_gdn, w_branch_att, w_out, ffn2_norm, ffn2_w_gate, ffn2_w_up, ffn2_w_down):
    return _forward(x, ffn1_norm, ffn1_w_gate, ffn1_w_up, ffn1_w_down, mix_norm, w_in, gdn_conv,
                    gdn_A_log, gdn_dt_bias, gdn_out_norm, att_q_norm, att_k_norm, att_rel_bias,
                    w_branch_gdn, w_branch_att, w_out, ffn2_norm, ffn2_w_gate, ffn2_w_up, ffn2_w_down)
```

```python
import functools
import math

import jax
import jax.numpy as jnp
from jax import lax
from jax.experimental import pallas as pl
from jax.experimental.pallas import tpu as pltpu

F32 = jnp.float32
BF16 = jnp.bfloat16

NORM_EPS = 1e-6
CHUNK = 64
HEADS = 8
HEAD_DIM = 128
LANES = 128
GDN_CONV = 4
ATT_LEFT_CHUNKS = 8
REL_MAX = 256
NEG = -1e30

VMEM_LIMIT = 56 * 1024 * 1024

LOG2E = math.log2(math.e)

NORM_TM = 512
FFN_UP_TM = 1024
FFN_TF = 512
FFN_DOWN_TM = 512
FFN_DOWN_NORM_TM = 512
FFN_DOWN_NORM_VMEM = 62 * 1024 * 1024
ROW_GROUP = 256
PROJ_TM = 1024
PROJ_TN = 1024
ATT_HEADS_PER_ITER = 2
MERGE_TM = 512
SEQ_BLOCK = 512
ATT_QG = 128
ATT_WIN = ATT_QG + ATT_LEFT_CHUNKS * CHUNK
ATT_TBL = 1024
PAIR = 2 * CHUNK
GDN_HEAD_GROUP = 8


def _rms(x, w):
    return x * lax.rsqrt(jnp.mean(x * x, axis=-1, keepdims=True) + NORM_EPS) * w


def _dot(a, b):
    return jnp.dot(a, b, preferred_element_type=F32)


def _dot_nt(a, b):
    return lax.dot_general(a, b, (((1,), (1,)), ((), ())), preferred_element_type=F32)


def _dot_tn(a, b):
    return lax.dot_general(a, b, (((0,), (0,)), ((), ())), preferred_element_type=F32)


def _norm_body(x_ref, nw_ref, o_ref):
    o_ref[...] = _rms(x_ref[...], nw_ref[...]).astype(BF16)


def _norm(x, norm_w):
    t, d = x.shape
    tm = min(NORM_TM, t)
    return pl.pallas_call(
        _norm_body,
        grid=(t // tm,),
        in_specs=[pl.BlockSpec((tm, d), lambda i: (i, 0)), pl.BlockSpec((1, d), lambda i: (0, 0))],
        out_specs=pl.BlockSpec((tm, d), lambda i: (i, 0)),
        out_shape=jax.ShapeDtypeStruct((t, d), BF16),
        compiler_params=pltpu.CompilerParams(dimension_semantics=("parallel",), vmem_limit_bytes=VMEM_LIMIT),
        name="norm",
    )(x, norm_w.reshape(1, d))


def _ffn_up_body(h_ref, wg_ref, wu_ref, wd_ref, o_ref, wdb_ref, w_scr, *, n_side):
    tf = wg_ref.shape[1]

    @pl.when(pl.program_id(1) == 0)
    def _():
        w_scr[:, :tf] = wg_ref[...].astype(BF16)
        w_scr[:, tf:] = wu_ref[...].astype(BF16)

    @pl.when(pl.program_id(0) * pl.num_programs(1) + pl.program_id(1) < n_side)
    def _():
        wdb_ref[...] = wd_ref[...].astype(BF16)

    gu = _dot(h_ref[...], w_scr[...])
    g, u = gu[:, :tf], gu[:, tf:]
    o_ref[...] = (g * jax.nn.sigmoid(g) * u).astype(BF16)


def _ffn_up(h, wg, wu, wd):
    t, d = h.shape
    f = wg.shape[1]
    tm, tf = min(FFN_UP_TM, t), FFN_TF
    nj, ni = pl.cdiv(f, tf), t // tm
    rows = next(r for r in range(16, f + 1, 16) if f % r == 0 and f // r <= nj * ni)
    n_side = f // rows
    side = pl.BlockSpec((rows, d), lambda j, i: (jnp.minimum(j * ni + i, n_side - 1), 0))
    return pl.pallas_call(
        functools.partial(_ffn_up_body, n_side=n_side),
        grid=(nj, ni),
        in_specs=[
            pl.BlockSpec((tm, d), lambda j, i: (i, 0)),
            pl.BlockSpec((d, tf), lambda j, i: (0, j)),
            pl.BlockSpec((d, tf), lambda j, i: (0, j)),
            side,
        ],
        out_specs=[pl.BlockSpec((tm, tf), lambda j, i: (i, j)), side],
        out_shape=[jax.ShapeDtypeStruct((t, f), BF16), jax.ShapeDtypeStruct((f, d), BF16)],
        scratch_shapes=[pltpu.VMEM((d, 2 * tf), BF16)],
        compiler_params=pltpu.CompilerParams(
            dimension_semantics=("arbitrary", "arbitrary"), vmem_limit_bytes=VMEM_LIMIT),
        name="ffn_up",
    )(h, wg, wu, wd)


def _ffn_down_body(a_ref, wd_ref, x_ref, o_ref):
    o_ref[...] = x_ref[...] + 0.5 * _dot(a_ref[...], wd_ref[...])


def _ffn_down_norm_body(a_ref, wd_ref, x_ref, nw_ref, ws_ref, o_ref, hn_ref, side_ref):
    rows = a_ref.shape[0]
    group = min(ROW_GROUP, rows)
    for r0 in range(0, rows, group):
        rs = slice(r0, r0 + group)
        y = x_ref[rs, :] + 0.5 * _dot(a_ref[rs, :], wd_ref[...])
        o_ref[rs, :] = y
        hn = _rms(y, nw_ref[...]).astype(BF16)
        hn_ref[rs, :] = hn
        side_ref[rs, :] = _dot(hn, ws_ref[...])


def _ffn_down(act, wd, x, next_norm_w=None, w_side=None):
    t, d = x.shape
    f = act.shape[1]
    tm = min(FFN_DOWN_TM if next_norm_w is None else FFN_DOWN_NORM_TM, t)
    row = pl.BlockSpec((tm, d), lambda i: (i, 0))
    in_specs = [pl.BlockSpec((tm, f), lambda i: (i, 0)),
                pl.BlockSpec((f, d), lambda i: (0, 0), pipeline_mode=pl.Buffered(1)),
                row]
    params = pltpu.CompilerParams(dimension_semantics=("parallel",), vmem_limit_bytes=VMEM_LIMIT)
    if next_norm_w is None:
        assert w_side is None
        return pl.pallas_call(
            _ffn_down_body, grid=(t // tm,), in_specs=in_specs, out_specs=row,
            out_shape=jax.ShapeDtypeStruct((t, d), F32), compiler_params=params, name="ffn_down",
        )(act, wd, x)
    return pl.pallas_call(
        _ffn_down_norm_body, grid=(t // tm,),
        in_specs=in_specs + [pl.BlockSpec((1, d), lambda i: (0, 0)), pl.BlockSpec((d, LANES), lambda i: (0, 0))],
        out_specs=[row, row, pl.BlockSpec((tm, LANES), lambda i: (i, 0))],
        out_shape=[jax.ShapeDtypeStruct((t, d), F32), jax.ShapeDtypeStruct((t, d), BF16),
                   jax.ShapeDtypeStruct((t, LANES), F32)],
        compiler_params=pltpu.CompilerParams(dimension_semantics=("parallel",),
                                             vmem_limit_bytes=FFN_DOWN_NORM_VMEM),
        name="ffn_down_norm",
    )(act, wd, x, next_norm_w.reshape(1, d), w_side)


def _proj_body(h_ref, wf_ref, wx_ref, p_ref, w_scr, *, n_aligned, shift):
    j, i = pl.program_id(0), pl.program_id(1)
    tn = wf_ref.shape[0]

    @pl.when(jnp.logical_and(i == 0, j < n_aligned))
    def _():
        w_scr[...] = wf_ref[...].astype(BF16)

    @pl.when(jnp.logical_and(i == 0, j >= n_aligned))
    def _():
        w_scr[:tn - shift, :] = wf_ref[shift:, :].astype(BF16)
        w_scr[tn - shift:, :] = wx_ref[:shift, :].astype(BF16)

    r = _dot_nt(h_ref[...], w_scr[...])
    for c in range(p_ref.shape[0]):
        p_ref[c] = r[:, c * LANES:(c + 1) * LANES].astype(BF16)


def _proj(h, wt, n_aligned, shift, n_tiles):
    t, d = h.shape
    tm, tn = min(PROJ_TM, t), PROJ_TN
    assert 0 < shift < LANES and shift % 16 == 0 and wt.shape[0] == n_tiles * tn + shift
    spb = tn // LANES
    return pl.pallas_call(
        functools.partial(_proj_body, n_aligned=n_aligned, shift=shift),
        grid=(n_tiles, t // tm),
        in_specs=[
            pl.BlockSpec((tm, d), lambda j, i: (i, 0)),
            pl.BlockSpec((tn, d), lambda j, i: (j, 0)),
            pl.BlockSpec((LANES, d), lambda j, i: ((j + 1) * spb, 0)),
        ],
        out_specs=pl.BlockSpec((spb, tm, LANES), lambda j, i: (j, i, 0)),
        out_shape=jax.ShapeDtypeStruct((n_tiles * spb, t, LANES), BF16),
        scratch_shapes=[pltpu.VMEM((tn, d), BF16)],
        compiler_params=pltpu.CompilerParams(
            dimension_semantics=("arbitrary", "arbitrary"), vmem_limit_bytes=VMEM_LIMIT),
        name="proj",
    )(h, wt, wt)


def _merge_body(og_ref, oa_ref, gg0_ref, gg1_ref, ga0_ref, ga1_ref, x_ref, wa_ref, wb_ref, wo_ref, nw_ref,
                o_ref, hn_ref):
    def slabs(*refs):
        return jnp.concatenate([ref[c] for ref in refs for c in range(ref.shape[0])], axis=-1)

    ya = _dot(slabs(og_ref), wa_ref[...])
    yb = _dot(slabs(oa_ref), wb_ref[...])
    m = (jax.nn.sigmoid(slabs(gg0_ref, gg1_ref).astype(F32)) * ya
         + jax.nn.sigmoid(slabs(ga0_ref, ga1_ref).astype(F32)) * yb)
    y = x_ref[...] + _dot(m.astype(BF16), wo_ref[...])
    o_ref[...] = y
    hn_ref[...] = _rms(y, nw_ref[...]).astype(BF16)


def _merge(og3, oa3, p3, gate_blk, x, wa, wb, wo, next_norm_w):
    t, d = x.shape
    tm = min(MERGE_TM, t)
    assert d == 2 * HEADS * LANES
    const = lambda i: (0, 0)
    row = pl.BlockSpec((tm, d), lambda i: (i, 0))
    slab = lambda blk: pl.BlockSpec((HEADS, tm, LANES), lambda i: (blk, i, 0))
    return pl.pallas_call(
        _merge_body,
        grid=(t // tm,),
        in_specs=[
            slab(0), slab(0),
            slab(gate_blk), slab(gate_blk + 1), slab(gate_blk + 2), slab(gate_blk + 3),
            row,
            pl.BlockSpec(wa.shape, const, pipeline_mode=pl.Buffered(1)),
            pl.BlockSpec(wb.shape, const, pipeline_mode=pl.Buffered(1)),
            pl.BlockSpec(wo.shape, const, pipeline_mode=pl.Buffered(1)),
            pl.BlockSpec((1, d), const),
        ],
        out_specs=[row, row],
        out_shape=[jax.ShapeDtypeStruct((t, d), F32), jax.ShapeDtypeStruct((t, d), BF16)],
        compiler_params=pltpu.CompilerParams(
            dimension_semantics=("parallel",), vmem_limit_bytes=VMEM_LIMIT),
        name="merge",
    )(og3, oa3, p3, p3, p3, p3, x, wa, wb, wo, next_norm_w.reshape(1, d))


def _att_body(q_ref, k_ref, v_ref, qw_ref, kw_ref, bvec_ref, w0_ref, w1_ref, w2_ref,
              o_ref, w0b_ref, w1b_ref, w2b_ref, kbuf, vbuf, bias_scr):
    blk = pl.program_id(0)
    rb = q_ref.shape[1]

    w0b_ref[...] = w0_ref[...].astype(BF16)
    w1b_ref[...] = w1_ref[...].astype(BF16)
    w2b_ref[...] = w2_ref[...].astype(BF16)

    @pl.when(blk == 0)
    def _():
        kbuf[:, :rb, :] = jnp.zeros((HEADS, rb, LANES), BF16)
        vbuf[:, :rb, :] = jnp.zeros((HEADS, rb, LANES), BF16)
        qc = lax.broadcasted_iota(jnp.int32, (ATT_QG, ATT_WIN), 0) // CHUNK
        kc = lax.broadcasted_iota(jnp.int32, (ATT_QG, ATT_WIN), 1) // CHUNK
        off = kc - qc
        band = jnp.where(off >= 0, jnp.where(off <= ATT_LEFT_CHUNKS, 0.0, NEG), NEG).astype(F32)
        for h in range(HEADS):
            tbl = jnp.broadcast_to(bvec_ref[h], (ATT_QG, ATT_TBL))
            tbl = pltpu.roll(tbl, 0, 1, stride=1, stride_axis=0)
            bias_scr[h] = tbl[:, :ATT_WIN] * LOG2E + band

    first_neg = jnp.where(blk == 0, NEG, 0.0).astype(F32)
    col = lax.broadcasted_iota(jnp.int32, (1, ATT_WIN), 1)
    qw = qw_ref[...] * (HEAD_DIM ** -0.5 * LOG2E)
    kw = kw_ref[...]
    groups = [(dh, g * ATT_QG) for dh in range(ATT_HEADS_PER_ITER) for g in range(rb // ATT_QG)]

    def heads(hi, carry):
        h0 = hi * ATT_HEADS_PER_ITER
        for dh in range(ATT_HEADS_PER_ITER):
            kbuf[h0 + dh, rb:, :] = _rms(k_ref[h0 + dh].astype(F32), kw).astype(BF16)
            vbuf[h0 + dh, rb:, :] = v_ref[h0 + dh]
        qn = [_rms(q_ref[h0 + dh, r0:r0 + ATT_QG, :].astype(F32), qw).astype(BF16) for dh, r0 in groups]
        s = [_dot_nt(qn[n], kbuf[h0 + dh, r0:r0 + ATT_WIN, :]) for n, (dh, r0) in enumerate(groups)]
        s = [s[n] + bias_scr[h0 + dh] + jnp.where(col < rb - r0, first_neg, 0.0)
             for n, (dh, r0) in enumerate(groups)]
        p = [jnp.exp2(sn - jnp.max(sn, axis=-1, keepdims=True)) for sn in s]
        l = [jnp.sum(pn, axis=-1, keepdims=True) for pn in p]
        o = [_dot(p[n].astype(BF16), vbuf[h0 + dh, r0:r0 + ATT_WIN, :]) for n, (dh, r0) in enumerate(groups)]
        for n, (dh, r0) in enumerate(groups):
            o_ref[h0 + dh, r0:r0 + ATT_QG, :] = (o[n] / l[n]).astype(BF16)
        return carry

    lax.fori_loop(0, HEADS // ATT_HEADS_PER_ITER, heads, 0)
    kbuf[:, :rb, :] = kbuf[:, rb:, :]
    vbuf[:, :rb, :] = vbuf[:, rb:, :]


def _att_bias_table(rel_bias):
    tbl = rel_bias.astype(F32)
    n_tbl = tbl.shape[1]
    far = ATT_LEFT_CHUNKS * CHUNK - REL_MAX
    rep = lambda c, n: jnp.broadcast_to(tbl[:, c:c + 1], (tbl.shape[0], n))
    vec = jnp.concatenate([rep(n_tbl - 1, far), tbl[:, ::-1],
                           rep(0, ATT_TBL - far - n_tbl - (ATT_QG - 1)), rep(n_tbl - 1, ATT_QG - 1)], axis=1)
    return vec[:, None, :]


def _att(p3, q_norm_w, k_norm_w, rel_bias, q_blk, k_blk, v_blk, side_weights):
    t = p3.shape[1]
    rb = SEQ_BLOCK
    assert rb == ATT_LEFT_CHUNKS * CHUNK and t % rb == 0
    nblk = t // rb
    bvec = _att_bias_table(rel_bias)
    assert all(w.shape[0] % (16 * nblk) == 0 for w in side_weights)
    side_specs = [pl.BlockSpec((w.shape[0] // nblk, w.shape[1]), lambda b: (b, 0)) for w in side_weights]
    outs = pl.pallas_call(
        _att_body,
        grid=(nblk,),
        in_specs=[
            pl.BlockSpec((HEADS, rb, LANES), lambda b: (q_blk, b, 0)),
            pl.BlockSpec((HEADS, rb, LANES), lambda b: (k_blk, b, 0)),
            pl.BlockSpec((HEADS, rb, LANES), lambda b: (v_blk, b, 0)),
            pl.BlockSpec((1, LANES), lambda b: (0, 0)),
            pl.BlockSpec((1, LANES), lambda b: (0, 0)),
            pl.BlockSpec(bvec.shape, lambda b: (0, 0, 0)),
        ] + side_specs,
        out_specs=[pl.BlockSpec((HEADS, rb, LANES), lambda b: (0, b, 0))] + side_specs,
        out_shape=[jax.ShapeDtypeStruct((HEADS, t, LANES), BF16)]
                  + [jax.ShapeDtypeStruct(w.shape, BF16) for w in side_weights],
        scratch_shapes=[pltpu.VMEM((HEADS, 2 * rb, LANES), BF16),
                        pltpu.VMEM((HEADS, 2 * rb, LANES), BF16),
                        pltpu.VMEM((HEADS, ATT_QG, ATT_WIN), F32)],
        compiler_params=pltpu.CompilerParams(
            dimension_semantics=("arbitrary",), vmem_limit_bytes=VMEM_LIMIT),
        name="att",
    )(p3, p3, p3, q_norm_w.reshape(1, LANES), k_norm_w.reshape(1, LANES), bvec, *side_weights)
    return outs[0], outs[1:]


def _split2(x):
    hi = x.astype(BF16)
    return hi, (x - hi.astype(F32)).astype(BF16)


def _gdn_body(q_ref, k_ref, v_ref, z_ref, ab_ref, cw_ref, alog_ref, dtb_ref, onw_ref, ls_ref, mx_ref,
              o_ref,
              s_scr, xp_scr, qn_scr, kn_scr, vn_scr, gb_scr, bb_scr, u_scr, wq_scr, aqk_scr, kw_scr,
              bm_scr, cd_scr):
    blk = pl.program_id(0)
    rows = q_ref.shape[1]
    npair = rows // PAIR
    hs = range(HEADS)
    hd = tuple(hs)

    @pl.when(blk == 0)
    def _():
        s_scr[...] = jnp.zeros_like(s_scr)
        xp_scr[:, 0:8, :] = jnp.zeros((3 * HEADS, 8, LANES), F32)

    ls = ls_ref[...]
    mx = mx_ref[...]
    ri = lax.broadcasted_iota(jnp.int32, (PAIR, PAIR), 0)
    ci = lax.broadcasted_iota(jnp.int32, (PAIR, PAIR), 1)
    same = (ri // CHUNK) == (ci // CHUNK)
    lower_f = jnp.where(same, jnp.where(ri >= ci, 1.0, 0.0), 0.0).astype(F32)
    strict_f = jnp.where(same, jnp.where(ri > ci, 1.0, 0.0), 0.0).astype(F32)
    eye = jnp.where(ri == ci, 1.0, 0.0).astype(F32)
    onw = onw_ref[...]
    alog, dtb = alog_ref[...], dtb_ref[...]

    def stage_a(p):
        r0 = p * PAIR
        sl = slice(r0, r0 + PAIR)
        ab = ab_ref[sl, :]
        xa = ab + dtb
        softplus = jnp.maximum(xa, 0.0) + jnp.log(1.0 + jnp.exp(-jnp.abs(xa)))
        g_all = -jnp.exp(alog) * softplus
        beta_all = jax.nn.sigmoid(ab)
        for h in hs:
            gb_scr[hd[h], sl, :] = jnp.broadcast_to(g_all[:, h:h + 1], (PAIR, LANES))
            bb_scr[hd[h], sl, :] = jnp.broadcast_to(beta_all[:, HEADS + h:HEADS + h + 1], (PAIR, LANES))
            outs = []
            for part, ref in enumerate((q_ref, k_ref, v_ref)):
                s = part * HEADS + h
                xp_scr[s, 8 + r0:8 + r0 + PAIR, :] = ref[h, sl, :].astype(F32)
                w = cw_ref[s]
                y = w[0:1] * xp_scr[s, 5 + r0:5 + r0 + PAIR, :]
                for tap in range(1, GDN_CONV):
                    y = y + w[tap:tap + 1] * xp_scr[s, 5 + tap + r0:5 + tap + r0 + PAIR, :]
                outs.append(y * jax.nn.sigmoid(y))
            qc, kc, vc = outs
            qn_scr[hd[h], sl, :] = (qc * lax.rsqrt(jnp.sum(qc * qc, axis=-1, keepdims=True) + NORM_EPS)
                                * (HEAD_DIM ** -0.5))
            kn_scr[hd[h], sl, :] = kc * lax.rsqrt(jnp.sum(kc * kc, axis=-1, keepdims=True) + NORM_EPS)
            vn_scr[hd[h], sl, :] = vc

    def stage_b(p, hd):
        hs = range(len(hd))
        sl = slice(p * PAIR, (p + 1) * PAIR)
        k = [kn_scr[hd[h], sl, :] for h in hs]
        q = [qn_scr[hd[h], sl, :] for h in hs]
        v = [vn_scr[hd[h], sl, :] for h in hs]
        gb = [gb_scr[hd[h], sl, :] for h in hs]
        bb = [bb_scr[hd[h], sl, :] for h in hs]
        parts = [_split2(jnp.concatenate([gb[h], gb[h]], axis=1) * mx) for h in hs]
        gm = [_dot(ls, pt[0]) + _dot(ls, pt[1]) for pt in parts]
        yield
        decay = [jnp.exp(g[:, :PAIR]) for g in gm]
        gi = [g[:, PAIR:] for g in gm]
        glast = [jnp.concatenate([jnp.broadcast_to(g[c * CHUNK - 1:c * CHUNK, :], (CHUNK, LANES))
                                  for c in (1, 2)], axis=0) for g in gi]
        eg = [jnp.exp(g) for g in gi]
        er = [jnp.exp(glast[h] - gi[h]) for h in hs]
        kb = [k[h] * bb[h] for h in hs]
        kk = [_dot_nt(jnp.concatenate([kb[h], q[h]], axis=0).astype(BF16), k[h].astype(BF16)) for h in hs]
        yield
        n = [-(kk[h][:PAIR] * decay[h] * strict_f) for h in hs]
        aqk = [kk[h][PAIR:] * decay[h] * lower_f for h in hs]
        inv = [eye + n[h] for h in hs]
        nb = [n[h].astype(BF16) for h in hs]
        nb = [_dot(nb[h], nb[h]).astype(BF16) for h in hs]
        yield
        for it in range(5):
            if it < 4:
                m = [_dot(jnp.concatenate([inv[h].astype(BF16), nb[h]], axis=0), nb[h]) for h in hs]
                inv = [inv[h] + m[h][:PAIR] for h in hs]
                nb = [m[h][PAIR:].astype(BF16) for h in hs]
            else:
                inv = [inv[h] + _dot(inv[h].astype(BF16), nb[h]) for h in hs]
            yield
        uw = [_dot(inv[h].astype(BF16),
                   jnp.concatenate([v[h] * bb[h], kb[h] * eg[h]], axis=1).astype(BF16)) for h in hs]
        yield
        kd = [(k[h] * er[h]).astype(BF16) for h in hs]
        qg = [q[h] * eg[h] for h in hs]
        for cc in range(2):
            c = 2 * p + cc
            rs = slice(cc * CHUNK, (cc + 1) * CHUNK)
            kwb = [_dot_tn(kd[h][rs], jnp.concatenate([uw[h][rs, LANES:], uw[h][rs, :LANES]], axis=1).astype(BF16))
                   for h in hs]
            for h in hs:
                wq_scr[hd[h], c] = jnp.concatenate([uw[h][rs, LANES:], qg[h][rs]], axis=0).astype(BF16)
                kw_scr[hd[h], c] = kwb[h][:, :LANES].astype(BF16)
                bm_scr[hd[h], c] = kwb[h][:, LANES:]
                cd_scr[hd[h], c] = jnp.broadcast_to(eg[h][(cc + 1) * CHUNK - 1:(cc + 1) * CHUNK, :], (8, LANES))
            yield
        for h in hs:
            u_scr[hd[h], sl, :] = uw[h][:, :LANES]
            aqk_scr[hd[h], p] = aqk[h].astype(BF16)

    def stage_c(p, hd):
        hs = range(len(hd))
        sl = slice(p * PAIR, (p + 1) * PAIR)
        c0, c1 = 2 * p, 2 * p + 1
        s0 = [s_scr[hd[h]] for h in hs]
        sb0 = [s0[h].astype(BF16) for h in hs]
        s1 = [s0[h] * cd_scr[hd[h], c0][0:1, :] + (bm_scr[hd[h], c0] - _dot(kw_scr[hd[h], c0], sb0[h])) for h in hs]
        yield
        sb1 = [s1[h].astype(BF16) for h in hs]
        s2 = [s1[h] * cd_scr[hd[h], c1][0:1, :] + (bm_scr[hd[h], c1] - _dot(kw_scr[hd[h], c1], sb1[h])) for h in hs]
        for h in hs:
            s_scr[hd[h]] = s2[h]
        yield
        a0 = [_dot(wq_scr[hd[h], c0], sb0[h]) for h in hs]
        a1 = [_dot(wq_scr[hd[h], c1], sb1[h]) for h in hs]
        yield
        vnew = [(u_scr[hd[h], sl, :] - jnp.concatenate([a0[h][:CHUNK], a1[h][:CHUNK]], axis=0)).astype(BF16)
                for h in hs]
        o = [jnp.concatenate([a0[h][CHUNK:], a1[h][CHUNK:]], axis=0) + _dot(aqk_scr[hd[h], p], vnew[h]) for h in hs]
        yield
        for h in hs:
            z = z_ref[hd[h], sl, :].astype(F32)
            o_ref[hd[h], sl, :] = (_rms(o[h], onw) * (z * jax.nn.sigmoid(z))).astype(BF16)

    def interleave(*gens):
        live = [g for g in gens if g is not None]
        while live:
            for g in list(live):
                try:
                    next(g)
                except StopIteration:
                    live.remove(g)

    stage_a(0)
    for step in range(npair + 1):
        if step + 1 < npair:
            stage_a(step + 1)
        for g0 in range(0, HEADS, GDN_HEAD_GROUP):
            grp = tuple(range(g0, g0 + GDN_HEAD_GROUP))
            interleave(stage_c(step - 1, grp) if step >= 1 else None, stage_b(step, grp) if step < npair else None)

    for s in range(3 * HEADS):
        xp_scr[s, 0:8, :] = xp_scr[s, rows:rows + 8, :]


def _gdn(p3, ab, conv_w, a_log, dt_bias, out_norm_w, q_blk):
    t = p3.shape[1]
    rows = min(SEQ_BLOCK, t)
    nchunk, npair = rows // CHUNK, rows // PAIR
    cw = conv_w.astype(F32).reshape(GDN_CONV, 3 * HEADS, LANES).transpose(1, 0, 2)
    pad = lambda v: jnp.zeros((1, LANES), F32).at[0, :HEADS].set(v.astype(F32))
    ti = jnp.arange(PAIR)
    same = (ti[:, None] // CHUNK) == (ti[None, :] // CHUNK)
    le = same & (ti[None, :] <= ti[:, None])
    gt = same & (ti[None, :] > ti[:, None])
    ls = le.astype(BF16)
    mx = jnp.concatenate([gt.T.astype(F32), jnp.ones((PAIR, LANES), F32)], axis=1)
    slab = lambda off: pl.BlockSpec((HEADS, rows, LANES), lambda b: (q_blk + off, b, 0))
    const2 = lambda b: (0, 0)
    hr = (HEADS, rows, LANES)
    return pl.pallas_call(
        _gdn_body,
        grid=(t // rows,),
        in_specs=[
            slab(0), slab(1), slab(2), slab(3),
            pl.BlockSpec((rows, LANES), lambda b: (b, 0)),
            pl.BlockSpec(cw.shape, lambda b: (0, 0, 0)),
            pl.BlockSpec((1, LANES), const2),
            pl.BlockSpec((1, LANES), const2),
            pl.BlockSpec((1, LANES), const2),
            pl.BlockSpec(ls.shape, const2),
            pl.BlockSpec(mx.shape, const2),
        ],
        out_specs=pl.BlockSpec(hr, lambda b: (0, b, 0)),
        out_shape=jax.ShapeDtypeStruct((HEADS, t, LANES), BF16),
        scratch_shapes=[
            pltpu.VMEM((HEADS, HEAD_DIM, HEAD_DIM), F32),
            pltpu.VMEM((3 * HEADS, rows + 8, LANES), F32),
            pltpu.VMEM(hr, F32), pltpu.VMEM(hr, F32), pltpu.VMEM(hr, F32),
            pltpu.VMEM(hr, F32), pltpu.VMEM(hr, F32),
            pltpu.VMEM(hr, F32),
            pltpu.VMEM((HEADS, nchunk, 2 * CHUNK, LANES), BF16),
            pltpu.VMEM((HEADS, npair, PAIR, PAIR), BF16),
            pltpu.VMEM((HEADS, nchunk, HEAD_DIM, HEAD_DIM), BF16),
            pltpu.VMEM((HEADS, nchunk, HEAD_DIM, HEAD_DIM), F32),
            pltpu.VMEM((HEADS, nchunk, 8, LANES), F32),
        ],
        compiler_params=pltpu.CompilerParams(
            dimension_semantics=("arbitrary",), vmem_limit_bytes=VMEM_LIMIT),
        name="gdn",
    )(p3, p3, p3, p3, ab, cw, pad(a_log), pad(dt_bias), out_norm_w.astype(F32).reshape(1, LANES), ls, mx)


def _pad_to(w, axis, mult):
    n = w.shape[axis]
    extra = (-n) % mult
    if extra == 0:
        return w
    widths = [(0, 0)] * w.ndim
    widths[axis] = (0, extra)
    return jnp.pad(w, widths)


@jax.jit
def _forward(x, ffn1_norm, ffn1_w_gate, ffn1_w_up, ffn1_w_down, mix_norm, w_in, gdn_conv,
             gdn_A_log, gdn_dt_bias, gdn_out_norm, att_q_norm, att_k_norm, att_rel_bias,
             w_branch_gdn, w_branch_att, w_out, ffn2_norm, ffn2_w_gate, ffn2_w_up, ffn2_w_down):
    b, t, d = x.shape
    gw = HEADS * HEAD_DIM
    outs = []
    for bi in range(b):
        xb = x[bi]
        depth = ffn1_norm.shape[0]
        for l in range(depth):
            wi = w_in[l]
            o_ab = 4 * gw
            o_att = o_ab + 2 * HEADS
            w_ab = _pad_to(wi[:, o_ab:o_att].astype(BF16), 1, LANES)

            act, wd = _ffn_up(_norm(xb, ffn1_norm[l]), ffn1_w_gate[l], ffn1_w_up[l], ffn1_w_down[l])
            xb, hn, ab = _ffn_down(act, wd, xb, mix_norm[l], w_ab)
            p3 = _proj(hn, wi.T, o_ab // PROJ_TN, o_att - o_ab, (wi.shape[1] - (o_att - o_ab)) // PROJ_TN)
            og3 = _gdn(p3, ab, gdn_conv[l], gdn_A_log[l], gdn_dt_bias[l], gdn_out_norm[l], 0)
            oa3, (wa, wb, wo) = _att(p3, att_q_norm[l], att_k_norm[l], att_rel_bias[l], 4, 5, 6,
                                     (w_branch_gdn[l], w_branch_att[l], w_out[l]))
            xb, hn = _merge(og3, oa3, p3, 7, xb, wa, wb, wo, ffn2_norm[l])

            act, wd = _ffn_up(hn, ffn2_w_gate[l], ffn2_w_up[l], ffn2_w_down[l])
            xb = _ffn_down(act, wd, xb)
        outs.append(xb)
    return jnp.stack(outs, axis=0)


def kernel(x, ffn1_norm, ffn1_w_gate, ffn1_w_up, ffn1_w_down, mix_norm, w_in, gdn_conv, gdn_A_log, gdn_dt_bias, gdn_out_norm, att_q_norm, att_k_norm, att_rel_bias, w_branch_gdn, w_branch_att, w_out, ffn2_norm, ffn2_w_gate, ffn2_w_up, ffn2_w_down):
    return _forward(x, ffn1_norm, ffn1_w_gate, ffn1_w_up, ffn1_w_down, mix_norm, w_in, gdn_conv,
                    gdn_A_log, gdn_dt_bias, gdn_out_norm, att_q_norm, att_k_norm, att_rel_bias,
                    w_branch_gdn, w_branch_att, w_out, ffn2_norm, ffn2_w_gate, ffn2_w_up, ffn2_w_down)
```

```python
import functools
import math

import jax
import jax.numpy as jnp
from jax import lax
from jax.experimental import pallas as pl
from jax.experimental.pallas import tpu as pltpu

F32 = jnp.float32
BF16 = jnp.bfloat16

NORM_EPS = 1e-6
CHUNK = 64
HEADS = 8
HEAD_DIM = 128
LANES = 128
SUBLANES = 8
BF16_ROWS = 16
GDN_CONV = 4
CONV_LEAD = SUBLANES - (GDN_CONV - 1)
ATT_LEFT_CHUNKS = 8
REL_MAX = 256
NEG = -1e30

MIB = 1024 * 1024
VMEM_V7X = 64 * MIB
VMEM_LIMIT = VMEM_V7X - 8 * MIB

LOG2E = math.log2(math.e)

NORM_TM = 512
FFN_UP_TM = 2048
FFN_TF = 512
FFN_COL_GROUP = 256
FFN_DOWN_TM = 512
FFN_DOWN_NORM_TM = 512
FFN_DOWN_NORM_VMEM = VMEM_V7X - 2 * MIB
ROW_GROUP = 256
PROJ_TM = 2048
PROJ_TN = 1024
ATT_HEADS_PER_ITER = 2
MERGE_TM = 512
SEQ_BLOCK = 512
ATT_QG = 128
ATT_WIN = ATT_QG + ATT_LEFT_CHUNKS * CHUNK
ATT_TBL = 1024
PAIR = 2 * CHUNK
GDN_HEAD_GROUP = 8


def _rms(x, w):
    return x * lax.rsqrt(jnp.mean(x * x, axis=-1, keepdims=True) + NORM_EPS) * w


def _dot(a, b):
    return jnp.dot(a, b, preferred_element_type=F32)


def _dot_nt(a, b):
    return lax.dot_general(a, b, (((1,), (1,)), ((), ())), preferred_element_type=F32)


def _dot_tn(a, b):
    return lax.dot_general(a, b, (((0,), (0,)), ((), ())), preferred_element_type=F32)


def _norm_body(x_ref, nw_ref, o_ref):
    o_ref[...] = _rms(x_ref[...], nw_ref[...]).astype(BF16)


def _norm(x, norm_w):
    t, d = x.shape
    tm = min(NORM_TM, t)
    return pl.pallas_call(
        _norm_body,
        grid=(t // tm,),
        in_specs=[pl.BlockSpec((tm, d), lambda i: (i, 0)), pl.BlockSpec((1, d), lambda i: (0, 0))],
        out_specs=pl.BlockSpec((tm, d), lambda i: (i, 0)),
        out_shape=jax.ShapeDtypeStruct((t, d), BF16),
        compiler_params=pltpu.CompilerParams(dimension_semantics=("parallel",), vmem_limit_bytes=VMEM_LIMIT),
        name="norm",
    )(x, norm_w.reshape(1, d))


def _ffn_up_body(h_ref, wg_ref, wu_ref, wd_ref, o_ref, wdb_ref, w_scr, *, n_side):
    tf = wg_ref.shape[1]

    @pl.when(pl.program_id(1) == 0)
    def _():
        w_scr[:, :tf] = wg_ref[...].astype(BF16)
        w_scr[:, tf:] = wu_ref[...].astype(BF16)

    @pl.when(pl.program_id(0) * pl.num_programs(1) + pl.program_id(1) < n_side)
    def _():
        wdb_ref[...] = wd_ref[...].astype(BF16)

    for n0 in range(0, tf, FFN_COL_GROUP):
        g = _dot(h_ref[...], w_scr[:, n0:n0 + FFN_COL_GROUP])
        u = _dot(h_ref[...], w_scr[:, tf + n0:tf + n0 + FFN_COL_GROUP])
        o_ref[:, n0:n0 + FFN_COL_GROUP] = (g * jax.nn.sigmoid(g) * u).astype(BF16)


def _ffn_up(h, wg, wu, wd):
    t, d = h.shape
    f = wg.shape[1]
    tm, tf = min(FFN_UP_TM, t), FFN_TF
    nj, ni = pl.cdiv(f, tf), t // tm
    rows = next(r for r in range(BF16_ROWS, f + 1, BF16_ROWS) if f % r == 0 and f // r <= nj * ni)
    n_side = f // rows
    side = pl.BlockSpec((rows, d), lambda j, i: (jnp.minimum(j * ni + i, n_side - 1), 0))
    return pl.pallas_call(
        functools.partial(_ffn_up_body, n_side=n_side),
        grid=(nj, ni),
        in_specs=[
            pl.BlockSpec((tm, d), lambda j, i: (i, 0)),
            pl.BlockSpec((d, tf), lambda j, i: (0, j)),
            pl.BlockSpec((d, tf), lambda j, i: (0, j)),
            side,
        ],
        out_specs=[pl.BlockSpec((tm, tf), lambda j, i: (i, j)), side],
        out_shape=[jax.ShapeDtypeStruct((t, f), BF16), jax.ShapeDtypeStruct((f, d), BF16)],
        scratch_shapes=[pltpu.VMEM((d, 2 * tf), BF16)],
        compiler_params=pltpu.CompilerParams(
            dimension_semantics=("arbitrary", "arbitrary"), vmem_limit_bytes=VMEM_LIMIT),
        name="ffn_up",
    )(h, wg, wu, wd)


def _ffn_down_body(a_ref, wd_ref, x_ref, o_ref):
    o_ref[...] = x_ref[...] + 0.5 * _dot(a_ref[...], wd_ref[...])


def _ffn_down_norm_body(a_ref, wd_ref, x_ref, nw_ref, ws_ref, o_ref, hn_ref, side_ref):
    rows = a_ref.shape[0]
    group = min(ROW_GROUP, rows)
    for r0 in range(0, rows, group):
        rs = slice(r0, r0 + group)
        y = x_ref[rs, :] + 0.5 * _dot(a_ref[rs, :], wd_ref[...])
        o_ref[rs, :] = y
        hn = _rms(y, nw_ref[...]).astype(BF16)
        hn_ref[rs, :] = hn
        side_ref[rs, :] = _dot(hn, ws_ref[...])


def _ffn_down(act, wd, x, next_norm_w=None, w_side=None):
    t, d = x.shape
    f = act.shape[1]
    tm = min(FFN_DOWN_TM if next_norm_w is None else FFN_DOWN_NORM_TM, t)
    row = pl.BlockSpec((tm, d), lambda i: (i, 0))
    in_specs = [pl.BlockSpec((tm, f), lambda i: (i, 0)),
                pl.BlockSpec((f, d), lambda i: (0, 0), pipeline_mode=pl.Buffered(1)),
                row]
    params = pltpu.CompilerParams(dimension_semantics=("parallel",), vmem_limit_bytes=VMEM_LIMIT)
    if next_norm_w is None:
        assert w_side is None
        return pl.pallas_call(
            _ffn_down_body, grid=(t // tm,), in_specs=in_specs, out_specs=row,
            out_shape=jax.ShapeDtypeStruct((t, d), F32), compiler_params=params, name="ffn_down",
        )(act, wd, x)
    return pl.pallas_call(
        _ffn_down_norm_body, grid=(t // tm,),
        in_specs=in_specs + [pl.BlockSpec((1, d), lambda i: (0, 0)), pl.BlockSpec((d, LANES), lambda i: (0, 0))],
        out_specs=[row, row, pl.BlockSpec((tm, LANES), lambda i: (i, 0))],
        out_shape=[jax.ShapeDtypeStruct((t, d), F32), jax.ShapeDtypeStruct((t, d), BF16),
                   jax.ShapeDtypeStruct((t, LANES), F32)],
        compiler_params=pltpu.CompilerParams(dimension_semantics=("parallel",),
                                             vmem_limit_bytes=FFN_DOWN_NORM_VMEM),
        name="ffn_down_norm",
    )(act, wd, x, next_norm_w.reshape(1, d), w_side)


def _proj_body(h_ref, wf_ref, wx_ref, p_ref, w_scr, *, n_aligned, shift):
    j, i = pl.program_id(0), pl.program_id(1)
    tn = wf_ref.shape[0]

    @pl.when(jnp.logical_and(i == 0, j < n_aligned))
    def _():
        w_scr[...] = wf_ref[...].astype(BF16)

    @pl.when(jnp.logical_and(i == 0, j >= n_aligned))
    def _():
        w_scr[:tn - shift, :] = wf_ref[shift:, :].astype(BF16)
        w_scr[tn - shift:, :] = wx_ref[:shift, :].astype(BF16)

    half = tn // 2
    for n0 in (0, half):
        r = _dot_nt(h_ref[...], w_scr[n0:n0 + half, :])
        for c in range(half // LANES):
            p_ref[n0 // LANES + c] = r[:, c * LANES:(c + 1) * LANES].astype(BF16)


def _proj(h, wt, n_aligned, shift, n_tiles):
    t, d = h.shape
    tm, tn = min(PROJ_TM, t), PROJ_TN
    assert 0 < shift < LANES and shift % BF16_ROWS == 0 and wt.shape[0] == n_tiles * tn + shift
    spb = tn // LANES
    return pl.pallas_call(
        functools.partial(_proj_body, n_aligned=n_aligned, shift=shift),
        grid=(n_tiles, t // tm),
        in_specs=[
            pl.BlockSpec((tm, d), lambda j, i: (i, 0)),
            pl.BlockSpec((tn, d), lambda j, i: (j, 0)),
            pl.BlockSpec((LANES, d), lambda j, i: ((j + 1) * spb, 0)),
        ],
        out_specs=pl.BlockSpec((spb, tm, LANES), lambda j, i: (j, i, 0)),
        out_shape=jax.ShapeDtypeStruct((n_tiles * spb, t, LANES), BF16),
        scratch_shapes=[pltpu.VMEM((tn, d), BF16)],
        compiler_params=pltpu.CompilerParams(
            dimension_semantics=("arbitrary", "arbitrary"), vmem_limit_bytes=VMEM_LIMIT),
        name="proj",
    )(h, wt, wt)


def _merge_body(og_ref, oa_ref, gg0_ref, gg1_ref, ga0_ref, ga1_ref, x_ref, wa_ref, wb_ref, wo_ref, nw_ref,
                o_ref, hn_ref):
    def slabs(*refs):
        return jnp.concatenate([ref[c] for ref in refs for c in range(ref.shape[0])], axis=-1)

    ya = _dot(slabs(og_ref), wa_ref[...])
    yb = _dot(slabs(oa_ref), wb_ref[...])
    m = (jax.nn.sigmoid(slabs(gg0_ref, gg1_ref).astype(F32)) * ya
         + jax.nn.sigmoid(slabs(ga0_ref, ga1_ref).astype(F32)) * yb)
    y = x_ref[...] + _dot(m.astype(BF16), wo_ref[...])
    o_ref[...] = y
    hn_ref[...] = _rms(y, nw_ref[...]).astype(BF16)


def _merge(og3, oa3, p3, gate_blk, x, wa, wb, wo, next_norm_w):
    t, d = x.shape
    tm = min(MERGE_TM, t)
    assert d == 2 * HEADS * LANES
    const = lambda i: (0, 0)
    row = pl.BlockSpec((tm, d), lambda i: (i, 0))
    slab = lambda blk: pl.BlockSpec((HEADS, tm, LANES), lambda i: (blk, i, 0))
    return pl.pallas_call(
        _merge_body,
        grid=(t // tm,),
        in_specs=[
            slab(0), slab(0),
            slab(gate_blk), slab(gate_blk + 1), slab(gate_blk + 2), slab(gate_blk + 3),
            row,
            pl.BlockSpec(wa.shape, const, pipeline_mode=pl.Buffered(1)),
            pl.BlockSpec(wb.shape, const, pipeline_mode=pl.Buffered(1)),
            pl.BlockSpec(wo.shape, const, pipeline_mode=pl.Buffered(1)),
            pl.BlockSpec((1, d), const),
        ],
        out_specs=[row, row],
        out_shape=[jax.ShapeDtypeStruct((t, d), F32), jax.ShapeDtypeStruct((t, d), BF16)],
        compiler_params=pltpu.CompilerParams(
            dimension_semantics=("parallel",), vmem_limit_bytes=VMEM_LIMIT),
        name="merge",
    )(og3, oa3, p3, p3, p3, p3, x, wa, wb, wo, next_norm_w.reshape(1, d))


def _att_body(q_ref, k_ref, v_ref, qw_ref, kw_ref, bvec_ref, w0_ref, w1_ref, w2_ref,
              o_ref, w0b_ref, w1b_ref, w2b_ref, kbuf, vbuf, bias_scr):
    blk = pl.program_id(0)
    rb = q_ref.shape[1]

    w0b_ref[...] = w0_ref[...].astype(BF16)
    w1b_ref[...] = w1_ref[...].astype(BF16)
    w2b_ref[...] = w2_ref[...].astype(BF16)

    @pl.when(blk == 0)
    def _():
        kbuf[:, :rb, :] = jnp.zeros((HEADS, rb, LANES), BF16)
        vbuf[:, :rb, :] = jnp.zeros((HEADS, rb, LANES), BF16)
        qc = lax.broadcasted_iota(jnp.int32, (ATT_QG, ATT_WIN), 0) // CHUNK
        kc = lax.broadcasted_iota(jnp.int32, (ATT_QG, ATT_WIN), 1) // CHUNK
        off = kc - qc
        band = jnp.where(off >= 0, jnp.where(off <= ATT_LEFT_CHUNKS, 0.0, NEG), NEG).astype(F32)
        for h in range(HEADS):
            tbl = jnp.broadcast_to(bvec_ref[h], (ATT_QG, ATT_TBL))
            tbl = pltpu.roll(tbl, 0, 1, stride=1, stride_axis=0)
            bias_scr[h] = tbl[:, :ATT_WIN] * LOG2E + band

    first_neg = jnp.where(blk == 0, NEG, 0.0).astype(F32)
    col = lax.broadcasted_iota(jnp.int32, (1, ATT_WIN), 1)
    qw = qw_ref[...] * (HEAD_DIM ** -0.5 * LOG2E)
    kw = kw_ref[...]
    groups = [(dh, g * ATT_QG) for dh in range(ATT_HEADS_PER_ITER) for g in range(rb // ATT_QG)]

    def heads(hi, carry):
        h0 = hi * ATT_HEADS_PER_ITER
        for dh in range(ATT_HEADS_PER_ITER):
            kbuf[h0 + dh, rb:, :] = _rms(k_ref[h0 + dh].astype(F32), kw).astype(BF16)
            vbuf[h0 + dh, rb:, :] = v_ref[h0 + dh]
        qn = [_rms(q_ref[h0 + dh, r0:r0 + ATT_QG, :].astype(F32), qw).astype(BF16) for dh, r0 in groups]
        s = [_dot_nt(qn[n], kbuf[h0 + dh, r0:r0 + ATT_WIN, :]) for n, (dh, r0) in enumerate(groups)]
        s = [s[n] + bias_scr[h0 + dh] + jnp.where(col < rb - r0, first_neg, 0.0)
             for n, (dh, r0) in enumerate(groups)]
        p = [jnp.exp2(sn - jnp.max(sn, axis=-1, keepdims=True)) for sn in s]
        l = [jnp.sum(pn, axis=-1, keepdims=True) for pn in p]
        o = [_dot(p[n].astype(BF16), vbuf[h0 + dh, r0:r0 + ATT_WIN, :]) for n, (dh, r0) in enumerate(groups)]
        for n, (dh, r0) in enumerate(groups):
            o_ref[h0 + dh, r0:r0 + ATT_QG, :] = (o[n] / l[n]).astype(BF16)
        return carry

    lax.fori_loop(0, HEADS // ATT_HEADS_PER_ITER, heads, 0)
    kbuf[:, :rb, :] = kbuf[:, rb:, :]
    vbuf[:, :rb, :] = vbuf[:, rb:, :]


def _att_bias_table(rel_bias):
    tbl = rel_bias.astype(F32)
    n_tbl = tbl.shape[1]
    far = ATT_LEFT_CHUNKS * CHUNK - REL_MAX
    rep = lambda c, n: jnp.broadcast_to(tbl[:, c:c + 1], (tbl.shape[0], n))
    vec = jnp.concatenate([rep(n_tbl - 1, far), tbl[:, ::-1],
                           rep(0, ATT_TBL - far - n_tbl - (ATT_QG - 1)), rep(n_tbl - 1, ATT_QG - 1)], axis=1)
    return vec[:, None, :]


def _att(p3, q_norm_w, k_norm_w, rel_bias, q_blk, k_blk, v_blk, side_weights):
    t = p3.shape[1]
    rb = SEQ_BLOCK
    assert rb == ATT_LEFT_CHUNKS * CHUNK and t % rb == 0
    nblk = t // rb
    bvec = _att_bias_table(rel_bias)
    assert all(w.shape[0] % (BF16_ROWS * nblk) == 0 for w in side_weights)
    side_specs = [pl.BlockSpec((w.shape[0] // nblk, w.shape[1]), lambda b: (b, 0)) for w in side_weights]
    outs = pl.pallas_call(
        _att_body,
        grid=(nblk,),
        in_specs=[
            pl.BlockSpec((HEADS, rb, LANES), lambda b: (q_blk, b, 0)),
            pl.BlockSpec((HEADS, rb, LANES), lambda b: (k_blk, b, 0)),
            pl.BlockSpec((HEADS, rb, LANES), lambda b: (v_blk, b, 0)),
            pl.BlockSpec((1, LANES), lambda b: (0, 0)),
            pl.BlockSpec((1, LANES), lambda b: (0, 0)),
            pl.BlockSpec(bvec.shape, lambda b: (0, 0, 0)),
        ] + side_specs,
        out_specs=[pl.BlockSpec((HEADS, rb, LANES), lambda b: (0, b, 0))] + side_specs,
        out_shape=[jax.ShapeDtypeStruct((HEADS, t, LANES), BF16)]
                  + [jax.ShapeDtypeStruct(w.shape, BF16) for w in side_weights],
        scratch_shapes=[pltpu.VMEM((HEADS, 2 * rb, LANES), BF16),
                        pltpu.VMEM((HEADS, 2 * rb, LANES), BF16),
                        pltpu.VMEM((HEADS, ATT_QG, ATT_WIN), F32)],
        compiler_params=pltpu.CompilerParams(
            dimension_semantics=("arbitrary",), vmem_limit_bytes=VMEM_LIMIT),
        name="att",
    )(p3, p3, p3, q_norm_w.reshape(1, LANES), k_norm_w.reshape(1, LANES), bvec, *side_weights)
    return outs[0], outs[1:]


def _split2(x):
    hi = x.astype(BF16)
    return hi, (x - hi.astype(F32)).astype(BF16)


def _gdn_body(q_ref, k_ref, v_ref, z_ref, ab_ref, cw_ref, alog_ref, dtb_ref, onw_ref, ls_ref, mx_ref,
              o_ref,
              s_scr, xp_scr, qn_scr, kn_scr, vn_scr, gb_scr, bb_scr, u_scr, wq_scr, aqk_scr, kw_scr,
              bm_scr, cd_scr):
    blk = pl.program_id(0)
    rows = q_ref.shape[1]
    npair = rows // PAIR
    hs = range(HEADS)
    hd = tuple(hs)

    @pl.when(blk == 0)
    def _():
        s_scr[...] = jnp.zeros_like(s_scr)
        xp_scr[:, 0:SUBLANES, :] = jnp.zeros((3 * HEADS, SUBLANES, LANES), F32)

    ls = ls_ref[...]
    mx = mx_ref[...]
    ri = lax.broadcasted_iota(jnp.int32, (PAIR, PAIR), 0)
    ci = lax.broadcasted_iota(jnp.int32, (PAIR, PAIR), 1)
    same = (ri // CHUNK) == (ci // CHUNK)
    lower_f = jnp.where(same, jnp.where(ri >= ci, 1.0, 0.0), 0.0).astype(F32)
    strict_f = jnp.where(same, jnp.where(ri > ci, 1.0, 0.0), 0.0).astype(F32)
    eye = jnp.where(ri == ci, 1.0, 0.0).astype(F32)
    onw = onw_ref[...]
    alog, dtb = alog_ref[...], dtb_ref[...]

    def stage_a(p):
        r0 = p * PAIR
        sl = slice(r0, r0 + PAIR)
        ab = ab_ref[sl, :]
        xa = ab + dtb
        softplus = jnp.maximum(xa, 0.0) + jnp.log(1.0 + jnp.exp(-jnp.abs(xa)))
        g_all = -jnp.exp(alog) * softplus
        beta_all = jax.nn.sigmoid(ab)
        for h in hs:
            gb_scr[hd[h], sl, :] = jnp.broadcast_to(g_all[:, h:h + 1], (PAIR, LANES))
            bb_scr[hd[h], sl, :] = jnp.broadcast_to(beta_all[:, HEADS + h:HEADS + h + 1], (PAIR, LANES))
            outs = []
            for part, ref in enumerate((q_ref, k_ref, v_ref)):
                s = part * HEADS + h
                xp_scr[s, SUBLANES + r0:SUBLANES + r0 + PAIR, :] = ref[h, sl, :].astype(F32)
                w = cw_ref[s]
                y = w[0:1] * xp_scr[s, CONV_LEAD + r0:CONV_LEAD + r0 + PAIR, :]
                for tap in range(1, GDN_CONV):
                    y = y + w[tap:tap + 1] * xp_scr[s, CONV_LEAD + tap + r0:CONV_LEAD + tap + r0 + PAIR, :]
                outs.append(y * jax.nn.sigmoid(y))
            qc, kc, vc = outs
            qn_scr[hd[h], sl, :] = (qc * lax.rsqrt(jnp.sum(qc * qc, axis=-1, keepdims=True) + NORM_EPS)
                                * (HEAD_DIM ** -0.5))
            kn_scr[hd[h], sl, :] = kc * lax.rsqrt(jnp.sum(kc * kc, axis=-1, keepdims=True) + NORM_EPS)
            vn_scr[hd[h], sl, :] = vc

    def stage_b(p, hd):
        hs = range(len(hd))
        sl = slice(p * PAIR, (p + 1) * PAIR)
        k = [kn_scr[hd[h], sl, :] for h in hs]
        q = [qn_scr[hd[h], sl, :] for h in hs]
        v = [vn_scr[hd[h], sl, :] for h in hs]
        gb = [gb_scr[hd[h], sl, :] for h in hs]
        bb = [bb_scr[hd[h], sl, :] for h in hs]
        parts = [_split2(jnp.concatenate([gb[h], gb[h]], axis=1) * mx) for h in hs]
        gm = [_dot(ls, pt[0]) + _dot(ls, pt[1]) for pt in parts]
        yield
        decay = [jnp.exp(g[:, :PAIR]) for g in gm]
        gi = [g[:, PAIR:] for g in gm]
        glast = [jnp.concatenate([jnp.broadcast_to(g[c * CHUNK - 1:c * CHUNK, :], (CHUNK, LANES))
                                  for c in (1, 2)], axis=0) for g in gi]
        eg = [jnp.exp(g) for g in gi]
        er = [jnp.exp(glast[h] - gi[h]) for h in hs]
        kb = [k[h] * bb[h] for h in hs]
        kk = [_dot_nt(jnp.concatenate([kb[h], q[h]], axis=0).astype(BF16), k[h].astype(BF16)) for h in hs]
        yield
        n = [-(kk[h][:PAIR] * decay[h] * strict_f) for h in hs]
        aqk = [kk[h][PAIR:] * decay[h] * lower_f for h in hs]
        inv = [eye + n[h] for h in hs]
        nb = [n[h].astype(BF16) for h in hs]
        nb = [_dot(nb[h], nb[h]).astype(BF16) for h in hs]
        yield
        for it in range(5):
            if it < 4:
                m = [_dot(jnp.concatenate([inv[h].astype(BF16), nb[h]], axis=0), nb[h]) for h in hs]
                inv = [inv[h] + m[h][:PAIR] for h in hs]
                nb = [m[h][PAIR:].astype(BF16) for h in hs]
            else:
                inv = [inv[h] + _dot(inv[h].astype(BF16), nb[h]) for h in hs]
            yield
        uw = [_dot(inv[h].astype(BF16),
                   jnp.concatenate([v[h] * bb[h], kb[h] * eg[h]], axis=1).astype(BF16)) for h in hs]
        yield
        kd = [(k[h] * er[h]).astype(BF16) for h in hs]
        qg = [q[h] * eg[h] for h in hs]
        for cc in range(2):
            c = 2 * p + cc
            rs = slice(cc * CHUNK, (cc + 1) * CHUNK)
            kwb = [_dot_tn(kd[h][rs], jnp.concatenate([uw[h][rs, LANES:], uw[h][rs, :LANES]], axis=1).astype(BF16))
                   for h in hs]
            for h in hs:
                wq_scr[hd[h], c] = jnp.concatenate([uw[h][rs, LANES:], qg[h][rs]], axis=0).astype(BF16)
                kw_scr[hd[h], c] = kwb[h][:, :LANES].astype(BF16)
                bm_scr[hd[h], c] = kwb[h][:, LANES:]
                cd_scr[hd[h], c] = jnp.broadcast_to(eg[h][(cc + 1) * CHUNK - 1:(cc + 1) * CHUNK, :], (SUBLANES, LANES))
            yield
        for h in hs:
            u_scr[hd[h], sl, :] = uw[h][:, :LANES]
            aqk_scr[hd[h], p] = aqk[h].astype(BF16)

    def stage_c(p, hd):
        hs = range(len(hd))
        sl = slice(p * PAIR, (p + 1) * PAIR)
        c0, c1 = 2 * p, 2 * p + 1
        s0 = [s_scr[hd[h]] for h in hs]
        sb0 = [s0[h].astype(BF16) for h in hs]
        s1 = [s0[h] * cd_scr[hd[h], c0][0:1, :] + (bm_scr[hd[h], c0] - _dot(kw_scr[hd[h], c0], sb0[h])) for h in hs]
        yield
        sb1 = [s1[h].astype(BF16) for h in hs]
        s2 = [s1[h] * cd_scr[hd[h], c1][0:1, :] + (bm_scr[hd[h], c1] - _dot(kw_scr[hd[h], c1], sb1[h])) for h in hs]
        for h in hs:
            s_scr[hd[h]] = s2[h]
        yield
        a0 = [_dot(wq_scr[hd[h], c0], sb0[h]) for h in hs]
        a1 = [_dot(wq_scr[hd[h], c1], sb1[h]) for h in hs]
        yield
        vnew = [(u_scr[hd[h], sl, :] - jnp.concatenate([a0[h][:CHUNK], a1[h][:CHUNK]], axis=0)).astype(BF16)
                for h in hs]
        o = [jnp.concatenate([a0[h][CHUNK:], a1[h][CHUNK:]], axis=0) + _dot(aqk_scr[hd[h], p], vnew[h]) for h in hs]
        yield
        for h in hs:
            z = z_ref[hd[h], sl, :].astype(F32)
            o_ref[hd[h], sl, :] = (_rms(o[h], onw) * (z * jax.nn.sigmoid(z))).astype(BF16)

    def interleave(*gens):
        live = [g for g in gens if g is not None]
        while live:
            for g in list(live):
                try:
                    next(g)
                except StopIteration:
                    live.remove(g)

    stage_a(0)
    for step in range(npair + 1):
        if step + 1 < npair:
            stage_a(step + 1)
        for g0 in range(0, HEADS, GDN_HEAD_GROUP):
            grp = tuple(range(g0, g0 + GDN_HEAD_GROUP))
            interleave(stage_c(step - 1, grp) if step >= 1 else None, stage_b(step, grp) if step < npair else None)

    for s in range(3 * HEADS):
        xp_scr[s, 0:SUBLANES, :] = xp_scr[s, rows:rows + SUBLANES, :]


def _gdn(p3, ab, conv_w, a_log, dt_bias, out_norm_w, q_blk):
    t = p3.shape[1]
    rows = min(SEQ_BLOCK, t)
    nchunk, npair = rows // CHUNK, rows // PAIR
    cw = conv_w.astype(F32).reshape(GDN_CONV, 3 * HEADS, LANES).transpose(1, 0, 2)
    pad = lambda v: jnp.zeros((1, LANES), F32).at[0, :HEADS].set(v.astype(F32))
    ti = jnp.arange(PAIR)
    same = (ti[:, None] // CHUNK) == (ti[None, :] // CHUNK)
    le = same & (ti[None, :] <= ti[:, None])
    gt = same & (ti[None, :] > ti[:, None])
    ls = le.astype(BF16)
    mx = jnp.concatenate([gt.T.astype(F32), jnp.ones((PAIR, LANES), F32)], axis=1)
    slab = lambda off: pl.BlockSpec((HEADS, rows, LANES), lambda b: (q_blk + off, b, 0))
    const2 = lambda b: (0, 0)
    hr = (HEADS, rows, LANES)
    return pl.pallas_call(
        _gdn_body,
        grid=(t // rows,),
        in_specs=[
            slab(0), slab(1), slab(2), slab(3),
            pl.BlockSpec((rows, LANES), lambda b: (b, 0)),
            pl.BlockSpec(cw.shape, lambda b: (0, 0, 0)),
            pl.BlockSpec((1, LANES), const2),
            pl.BlockSpec((1, LANES), const2),
            pl.BlockSpec((1, LANES), const2),
            pl.BlockSpec(ls.shape, const2),
            pl.BlockSpec(mx.shape, const2),
        ],
        out_specs=pl.BlockSpec(hr, lambda b: (0, b, 0)),
        out_shape=jax.ShapeDtypeStruct((HEADS, t, LANES), BF16),
        scratch_shapes=[
            pltpu.VMEM((HEADS, HEAD_DIM, HEAD_DIM), F32),
            pltpu.VMEM((3 * HEADS, rows + SUBLANES, LANES), F32),
            pltpu.VMEM(hr, F32), pltpu.VMEM(hr, F32), pltpu.VMEM(hr, F32),
            pltpu.VMEM(hr, F32), pltpu.VMEM(hr, F32),
            pltpu.VMEM(hr, F32),
            pltpu.VMEM((HEADS, nchunk, 2 * CHUNK, LANES), BF16),
            pltpu.VMEM((HEADS, npair, PAIR, PAIR), BF16),
            pltpu.VMEM((HEADS, nchunk, HEAD_DIM, HEAD_DIM), BF16),
            pltpu.VMEM((HEADS, nchunk, HEAD_DIM, HEAD_DIM), F32),
            pltpu.VMEM((HEADS, nchunk, SUBLANES, LANES), F32),
        ],
        compiler_params=pltpu.CompilerParams(
            dimension_semantics=("arbitrary",), vmem_limit_bytes=VMEM_LIMIT),
        name="gdn",
    )(p3, p3, p3, p3, ab, cw, pad(a_log), pad(dt_bias), out_norm_w.astype(F32).reshape(1, LANES), ls, mx)


def _pad_to(w, axis, mult):
    n = w.shape[axis]
    extra = (-n) % mult
    if extra == 0:
        return w
    widths = [(0, 0)] * w.ndim
    widths[axis] = (0, extra)
    return jnp.pad(w, widths)


@jax.jit
def _forward(x, ffn1_norm, ffn1_w_gate, ffn1_w_up, ffn1_w_down, mix_norm, w_in, gdn_conv,
             gdn_A_log, gdn_dt_bias, gdn_out_norm, att_q_norm, att_k_norm, att_rel_bias,
             w_branch_gdn, w_branch_att, w_out, ffn2_norm, ffn2_w_gate, ffn2_w_up, ffn2_w_down):
    b, t, d = x.shape
    gw = HEADS * HEAD_DIM
    outs = []
    for bi in range(b):
        xb = x[bi]
        depth = ffn1_norm.shape[0]
        for l in range(depth):
            wi = w_in[l]
            o_ab = 4 * gw
            o_att = o_ab + 2 * HEADS
            w_ab = _pad_to(wi[:, o_ab:o_att].astype(BF16), 1, LANES)

            act, wd = _ffn_up(_norm(xb, ffn1_norm[l]), ffn1_w_gate[l], ffn1_w_up[l], ffn1_w_down[l])
            xb, hn, ab = _ffn_down(act, wd, xb, mix_norm[l], w_ab)
            p3 = _proj(hn, wi.T, o_ab // PROJ_TN, o_att - o_ab, (wi.shape[1] - (o_att - o_ab)) // PROJ_TN)
            og3 = _gdn(p3, ab, gdn_conv[l], gdn_A_log[l], gdn_dt_bias[l], gdn_out_norm[l], 0)
            oa3, (wa, wb, wo) = _att(p3, att_q_norm[l], att_k_norm[l], att_rel_bias[l], 4, 5, 6,
                                     (w_branch_gdn[l], w_branch_att[l], w_out[l]))
            xb, hn = _merge(og3, oa3, p3, 7, xb, wa, wb, wo, ffn2_norm[l])

            act, wd = _ffn_up(hn, ffn2_w_gate[l], ffn2_w_up[l], ffn2_w_down[l])
            xb = _ffn_down(act, wd, xb)
        outs.append(xb)
    return jnp.stack(outs, axis=0)


def kernel(x, ffn1_norm, ffn1_w_gate, ffn1_w_up, ffn1_w_down, mix_norm, w_in, gdn_conv, gdn_A_log, gdn_dt_bias, gdn_out_norm, att_q_norm, att_k_norm, att_rel_bias, w_branch_gdn, w_branch_att, w_out, ffn2_norm, ffn2_w_gate, ffn2_w_up, ffn2_w_down):
    return _forward(x, ffn1_norm, ffn1_w_gate, ffn1_w_up, ffn1_w_down, mix_norm, w_in, gdn_conv,
                    gdn_A_log, gdn_dt_bias, gdn_out_norm, att_q_norm, att_k_norm, att_rel_bias,
                    w_branch_gdn, w_branch_att, w_out, ffn2_norm, ffn2_w_gate, ffn2_w_up, ffn2_w_down)
```

```python
import functools
import math

import jax
import jax.numpy as jnp
from jax import lax
from jax.experimental import pallas as pl
from jax.experimental.pallas import tpu as pltpu

F32 = jnp.float32
BF16 = jnp.bfloat16

NORM_EPS = 1e-6
CHUNK = 64
HEADS = 8
HEAD_DIM = 128
LANES = 128
SUBLANES = 8
BF16_ROWS = 16
GDN_CONV = 4
CONV_LEAD = SUBLANES - (GDN_CONV - 1)
ATT_LEFT_CHUNKS = 8
REL_MAX = 256
NEG = -1e30

MIB = 1024 * 1024
VMEM_V7X = 64 * MIB
VMEM_LIMIT = VMEM_V7X - 8 * MIB

LOG2E = math.log2(math.e)

NORM_TM = 512
FFN_UP_TM = 2048
FFN_TF = 512
FFN_COL_GROUP = 256
FFN_DOWN_TM = 512
FFN_DOWN_NORM_TM = 512
FFN_DOWN_NORM_VMEM = VMEM_V7X - 2 * MIB
ROW_GROUP = 256
PROJ_TM = 2048
PROJ_TN = 1024
ATT_HEADS_PER_ITER = 2
MERGE_TM = 512
SEQ_BLOCK = 512
ATT_QG = 128
ATT_WIN = ATT_QG + ATT_LEFT_CHUNKS * CHUNK
ATT_TBL = 1024
PAIR = 2 * CHUNK
GDN_HEAD_GROUP = 8


def _rms(x, w):
    return x * lax.rsqrt(jnp.mean(x * x, axis=-1, keepdims=True) + NORM_EPS) * w


def _dot(a, b):
    return jnp.dot(a, b, preferred_element_type=F32)


def _dot_nt(a, b):
    return lax.dot_general(a, b, (((1,), (1,)), ((), ())), preferred_element_type=F32)


def _dot_tn(a, b):
    return lax.dot_general(a, b, (((0,), (0,)), ((), ())), preferred_element_type=F32)


def _norm_body(x_ref, nw_ref, o_ref):
    o_ref[...] = _rms(x_ref[...], nw_ref[...]).astype(BF16)


def _norm(x, norm_w):
    t, d = x.shape
    tm = min(NORM_TM, t)
    return pl.pallas_call(
        _norm_body,
        grid=(t // tm,),
        in_specs=[pl.BlockSpec((tm, d), lambda i: (i, 0)), pl.BlockSpec((1, d), lambda i: (0, 0))],
        out_specs=pl.BlockSpec((tm, d), lambda i: (i, 0)),
        out_shape=jax.ShapeDtypeStruct((t, d), BF16),
        compiler_params=pltpu.CompilerParams(dimension_semantics=("parallel",), vmem_limit_bytes=VMEM_LIMIT),
        name="norm",
    )(x, norm_w.reshape(1, d))


def _ffn_up_body(h_ref, wg_ref, wu_ref, wd_ref, o_ref, wdb_ref, w_scr, *, n_side):
    tf = wg_ref.shape[1]

    @pl.when(pl.program_id(1) == 0)
    def _():
        w_scr[:, :tf] = wg_ref[...].astype(BF16)
        w_scr[:, tf:] = wu_ref[...].astype(BF16)

    @pl.when(pl.program_id(0) * pl.num_programs(1) + pl.program_id(1) < n_side)
    def _():
        wdb_ref[...] = wd_ref[...].astype(BF16)

    for n0 in range(0, tf, FFN_COL_GROUP):
        g = _dot(h_ref[...], w_scr[:, n0:n0 + FFN_COL_GROUP])
        u = _dot(h_ref[...], w_scr[:, tf + n0:tf + n0 + FFN_COL_GROUP])
        o_ref[:, n0:n0 + FFN_COL_GROUP] = (g * jax.nn.sigmoid(g) * u).astype(BF16)


def _ffn_up(h, wg, wu, wd):
    t, d = h.shape
    f = wg.shape[1]
    tm, tf = min(FFN_UP_TM, t), FFN_TF
    nj, ni = pl.cdiv(f, tf), t // tm
    rows = next(r for r in range(BF16_ROWS, f + 1, BF16_ROWS) if f % r == 0 and f // r <= nj * ni)
    n_side = f // rows
    side = pl.BlockSpec((rows, d), lambda j, i: (jnp.minimum(j * ni + i, n_side - 1), 0))
    return pl.pallas_call(
        functools.partial(_ffn_up_body, n_side=n_side),
        grid=(nj, ni),
        in_specs=[
            pl.BlockSpec((tm, d), lambda j, i: (i, 0)),
            pl.BlockSpec((d, tf), lambda j, i: (0, j)),
            pl.BlockSpec((d, tf), lambda j, i: (0, j)),
            side,
        ],
        out_specs=[pl.BlockSpec((tm, tf), lambda j, i: (i, j)), side],
        out_shape=[jax.ShapeDtypeStruct((t, f), BF16), jax.ShapeDtypeStruct((f, d), BF16)],
        scratch_shapes=[pltpu.VMEM((d, 2 * tf), BF16)],
        compiler_params=pltpu.CompilerParams(
            dimension_semantics=("arbitrary", "arbitrary"), vmem_limit_bytes=VMEM_LIMIT),
        name="ffn_up",
    )(h, wg, wu, wd)


def _ffn_down_body(a_ref, wd_ref, x_ref, o_ref):
    o_ref[...] = x_ref[...] + 0.5 * _dot(a_ref[...], wd_ref[...])


def _ffn_down_norm_body(a_ref, wd_ref, x_ref, nw_ref, ws_ref, o_ref, hn_ref, side_ref):
    rows = a_ref.shape[0]
    group = min(ROW_GROUP, rows)
    for r0 in range(0, rows, group):
        rs = slice(r0, r0 + group)
        y = x_ref[rs, :] + 0.5 * _dot(a_ref[rs, :], wd_ref[...])
        o_ref[rs, :] = y
        hn = _rms(y, nw_ref[...]).astype(BF16)
        hn_ref[rs, :] = hn
        side_ref[rs, :] = _dot(hn, ws_ref[...])


def _ffn_down(act, wd, x, next_norm_w=None, w_side=None):
    t, d = x.shape
    f = act.shape[1]
    tm = min(FFN_DOWN_TM if next_norm_w is None else FFN_DOWN_NORM_TM, t)
    row = pl.BlockSpec((tm, d), lambda i: (i, 0))
    in_specs = [pl.BlockSpec((tm, f), lambda i: (i, 0)),
                pl.BlockSpec((f, d), lambda i: (0, 0), pipeline_mode=pl.Buffered(1)),
                row]
    params = pltpu.CompilerParams(dimension_semantics=("parallel",), vmem_limit_bytes=VMEM_LIMIT)
    if next_norm_w is None:
        assert w_side is None
        return pl.pallas_call(
            _ffn_down_body, grid=(t // tm,), in_specs=in_specs, out_specs=row,
            out_shape=jax.ShapeDtypeStruct((t, d), F32), compiler_params=params, name="ffn_down",
        )(act, wd, x)
    return pl.pallas_call(
        _ffn_down_norm_body, grid=(t // tm,),
        in_specs=in_specs + [pl.BlockSpec((1, d), lambda i: (0, 0)), pl.BlockSpec((d, LANES), lambda i: (0, 0))],
        out_specs=[row, row, pl.BlockSpec((tm, LANES), lambda i: (i, 0))],
        out_shape=[jax.ShapeDtypeStruct((t, d), F32), jax.ShapeDtypeStruct((t, d), BF16),
                   jax.ShapeDtypeStruct((t, LANES), F32)],
        compiler_params=pltpu.CompilerParams(dimension_semantics=("parallel",),
                                             vmem_limit_bytes=FFN_DOWN_NORM_VMEM),
        name="ffn_down_norm",
    )(act, wd, x, next_norm_w.reshape(1, d), w_side)


def _proj_body(h_ref, wf_ref, wx_ref, p_ref, w_scr, *, n_aligned, shift):
    j, i = pl.program_id(0), pl.program_id(1)
    tn = wf_ref.shape[0]

    @pl.when(jnp.logical_and(i == 0, j < n_aligned))
    def _():
        w_scr[...] = wf_ref[...].astype(BF16)

    @pl.when(jnp.logical_and(i == 0, j >= n_aligned))
    def _():
        w_scr[:tn - shift, :] = wf_ref[shift:, :].astype(BF16)
        w_scr[tn - shift:, :] = wx_ref[:shift, :].astype(BF16)

    half = tn // 2
    for n0 in (0, half):
        r = _dot_nt(h_ref[...], w_scr[n0:n0 + half, :])
        for c in range(half // LANES):
            p_ref[n0 // LANES + c] = r[:, c * LANES:(c + 1) * LANES].astype(BF16)


def _proj(h, wt, n_aligned, shift, n_tiles):
    t, d = h.shape
    tm, tn = min(PROJ_TM, t), PROJ_TN
    assert 0 < shift < LANES and shift % BF16_ROWS == 0 and wt.shape[0] == n_tiles * tn + shift
    spb = tn // LANES
    return pl.pallas_call(
        functools.partial(_proj_body, n_aligned=n_aligned, shift=shift),
        grid=(n_tiles, t // tm),
        in_specs=[
            pl.BlockSpec((tm, d), lambda j, i: (i, 0)),
            pl.BlockSpec((tn, d), lambda j, i: (j, 0)),
            pl.BlockSpec((LANES, d), lambda j, i: ((j + 1) * spb, 0)),
        ],
        out_specs=pl.BlockSpec((spb, tm, LANES), lambda j, i: (j, i, 0)),
        out_shape=jax.ShapeDtypeStruct((n_tiles * spb, t, LANES), BF16),
        scratch_shapes=[pltpu.VMEM((tn, d), BF16)],
        compiler_params=pltpu.CompilerParams(
            dimension_semantics=("arbitrary", "arbitrary"), vmem_limit_bytes=VMEM_LIMIT),
        name="proj",
    )(h, wt, wt)


def _merge_body(og_ref, oa_ref, gg0_ref, gg1_ref, ga0_ref, ga1_ref, x_ref, wa_ref, wb_ref, wo_ref, nw_ref,
                o_ref, hn_ref):
    def slabs(*refs):
        return jnp.concatenate([ref[c] for ref in refs for c in range(ref.shape[0])], axis=-1)

    ya = _dot(slabs(og_ref), wa_ref[...])
    yb = _dot(slabs(oa_ref), wb_ref[...])
    m = (jax.nn.sigmoid(slabs(gg0_ref, gg1_ref).astype(F32)) * ya
         + jax.nn.sigmoid(slabs(ga0_ref, ga1_ref).astype(F32)) * yb)
    y = x_ref[...] + _dot(m.astype(BF16), wo_ref[...])
    o_ref[...] = y
    hn_ref[...] = _rms(y, nw_ref[...]).astype(BF16)


def _merge(og3, oa3, p3, gate_blk, x, wa, wb, wo, next_norm_w):
    t, d = x.shape
    tm = min(MERGE_TM, t)
    assert d == 2 * HEADS * LANES
    const = lambda i: (0, 0)
    row = pl.BlockSpec((tm, d), lambda i: (i, 0))
    slab = lambda blk: pl.BlockSpec((HEADS, tm, LANES), lambda i: (blk, i, 0))
    return pl.pallas_call(
        _merge_body,
        grid=(t // tm,),
        in_specs=[
            slab(0), slab(0),
            slab(gate_blk), slab(gate_blk + 1), slab(gate_blk + 2), slab(gate_blk + 3),
            row,
            pl.BlockSpec(wa.shape, const, pipeline_mode=pl.Buffered(1)),
            pl.BlockSpec(wb.shape, const, pipeline_mode=pl.Buffered(1)),
            pl.BlockSpec(wo.shape, const, pipeline_mode=pl.Buffered(1)),
            pl.BlockSpec((1, d), const),
        ],
        out_specs=[row, row],
        out_shape=[jax.ShapeDtypeStruct((t, d), F32), jax.ShapeDtypeStruct((t, d), BF16)],
        compiler_params=pltpu.CompilerParams(
            dimension_semantics=("parallel",), vmem_limit_bytes=VMEM_LIMIT),
        name="merge",
    )(og3, oa3, p3, p3, p3, p3, x, wa, wb, wo, next_norm_w.reshape(1, d))


def _att_body(q_ref, k_ref, v_ref, qw_ref, kw_ref, bvec_ref, w0_ref, w1_ref, w2_ref,
              o_ref, w0b_ref, w1b_ref, w2b_ref, kbuf, vbuf, bias_scr):
    blk = pl.program_id(0)
    rb = q_ref.shape[1]

    w0b_ref[...] = w0_ref[...].astype(BF16)
    w1b_ref[...] = w1_ref[...].astype(BF16)
    w2b_ref[...] = w2_ref[...].astype(BF16)

    @pl.when(blk == 0)
    def _():
        kbuf[:, :rb, :] = jnp.zeros((HEADS, rb, LANES), BF16)
        vbuf[:, :rb, :] = jnp.zeros((HEADS, rb, LANES), BF16)
        qc = lax.broadcasted_iota(jnp.int32, (ATT_QG, ATT_WIN), 0) // CHUNK
        kc = lax.broadcasted_iota(jnp.int32, (ATT_QG, ATT_WIN), 1) // CHUNK
        off = kc - qc
        band = jnp.where(off >= 0, jnp.where(off <= ATT_LEFT_CHUNKS, 0.0, NEG), NEG).astype(F32)
        for h in range(HEADS):
            tbl = jnp.broadcast_to(bvec_ref[h], (ATT_QG, ATT_TBL))
            tbl = pltpu.roll(tbl, 0, 1, stride=1, stride_axis=0)
            bias_scr[h] = tbl[:, :ATT_WIN] * LOG2E + band

    first_neg = jnp.where(blk == 0, NEG, 0.0).astype(F32)
    col = lax.broadcasted_iota(jnp.int32, (1, ATT_WIN), 1)
    qw = qw_ref[...] * (HEAD_DIM ** -0.5 * LOG2E)
    kw = kw_ref[...]
    groups = [(dh, g * ATT_QG) for dh in range(ATT_HEADS_PER_ITER) for g in range(rb // ATT_QG)]

    def heads(hi, carry):
        h0 = hi * ATT_HEADS_PER_ITER
        for dh in range(ATT_HEADS_PER_ITER):
            kbuf[h0 + dh, rb:, :] = _rms(k_ref[h0 + dh].astype(F32), kw).astype(BF16)
            vbuf[h0 + dh, rb:, :] = v_ref[h0 + dh]
        qn = [_rms(q_ref[h0 + dh, r0:r0 + ATT_QG, :].astype(F32), qw).astype(BF16) for dh, r0 in groups]
        s = [_dot_nt(qn[n], kbuf[h0 + dh, r0:r0 + ATT_WIN, :]) for n, (dh, r0) in enumerate(groups)]
        s = [s[n] + bias_scr[h0 + dh] + jnp.where(col < rb - r0, first_neg, 0.0)
             for n, (dh, r0) in enumerate(groups)]
        p = [jnp.exp2(sn - jnp.max(sn, axis=-1, keepdims=True)) for sn in s]
        l = [jnp.sum(pn, axis=-1, keepdims=True) for pn in p]
        o = [_dot(p[n].astype(BF16), vbuf[h0 + dh, r0:r0 + ATT_WIN, :]) for n, (dh, r0) in enumerate(groups)]
        for n, (dh, r0) in enumerate(groups):
            o_ref[h0 + dh, r0:r0 + ATT_QG, :] = (o[n] / l[n]).astype(BF16)
        return carry

    lax.fori_loop(0, HEADS // ATT_HEADS_PER_ITER, heads, 0)
    kbuf[:, :rb, :] = kbuf[:, rb:, :]
    vbuf[:, :rb, :] = vbuf[:, rb:, :]


def _att_bias_table(rel_bias):
    tbl = rel_bias.astype(F32)
    n_tbl = tbl.shape[1]
    far = ATT_LEFT_CHUNKS * CHUNK - REL_MAX
    rep = lambda c, n: jnp.broadcast_to(tbl[:, c:c + 1], (tbl.shape[0], n))
    vec = jnp.concatenate([rep(n_tbl - 1, far), tbl[:, ::-1],
                           rep(0, ATT_TBL - far - n_tbl - (ATT_QG - 1)), rep(n_tbl - 1, ATT_QG - 1)], axis=1)
    return vec[:, None, :]


def _att(p3, q_norm_w, k_norm_w, rel_bias, q_blk, k_blk, v_blk, side_weights):
    t = p3.shape[1]
    rb = SEQ_BLOCK
    assert rb == ATT_LEFT_CHUNKS * CHUNK and t % rb == 0
    nblk = t // rb
    bvec = _att_bias_table(rel_bias)
    assert all(w.shape[0] % (BF16_ROWS * nblk) == 0 for w in side_weights)
    side_specs = [pl.BlockSpec((w.shape[0] // nblk, w.shape[1]), lambda b: (b, 0)) for w in side_weights]
    outs = pl.pallas_call(
        _att_body,
        grid=(nblk,),
        in_specs=[
            pl.BlockSpec((HEADS, rb, LANES), lambda b: (q_blk, b, 0)),
            pl.BlockSpec((HEADS, rb, LANES), lambda b: (k_blk, b, 0)),
            pl.BlockSpec((HEADS, rb, LANES), lambda b: (v_blk, b, 0)),
            pl.BlockSpec((1, LANES), lambda b: (0, 0)),
            pl.BlockSpec((1, LANES), lambda b: (0, 0)),
            pl.BlockSpec(bvec.shape, lambda b: (0, 0, 0)),
        ] + side_specs,
        out_specs=[pl.BlockSpec((HEADS, rb, LANES), lambda b: (0, b, 0))] + side_specs,
        out_shape=[jax.ShapeDtypeStruct((HEADS, t, LANES), BF16)]
                  + [jax.ShapeDtypeStruct(w.shape, BF16) for w in side_weights],
        scratch_shapes=[pltpu.VMEM((HEADS, 2 * rb, LANES), BF16),
                        pltpu.VMEM((HEADS, 2 * rb, LANES), BF16),
                        pltpu.VMEM((HEADS, ATT_QG, ATT_WIN), F32)],
        compiler_params=pltpu.CompilerParams(
            dimension_semantics=("arbitrary",), vmem_limit_bytes=VMEM_LIMIT),
        name="att",
    )(p3, p3, p3, q_norm_w.reshape(1, LANES), k_norm_w.reshape(1, LANES), bvec, *side_weights)
    return outs[0], outs[1:]


def _split2(x):
    hi = x.astype(BF16)
    return hi, (x - hi.astype(F32)).astype(BF16)


def _gdn_body(q_ref, k_ref, v_ref, z_ref, ab_ref, cw_ref, alog_ref, dtb_ref, onw_ref, ls_ref, mx_ref,
              o_ref,
              s_scr, xp_scr, qn_scr, kn_scr, vn_scr, gb_scr, bb_scr, u_scr, wq_scr, aqk_scr, kw_scr,
              bm_scr, cd_scr):
    blk = pl.program_id(0)
    rows = q_ref.shape[1]
    npair = rows // PAIR
    hs = range(HEADS)
    hd = tuple(hs)

    @pl.when(blk == 0)
    def _():
        s_scr[...] = jnp.zeros_like(s_scr)
        xp_scr[:, 0:SUBLANES, :] = jnp.zeros((3 * HEADS, SUBLANES, LANES), F32)

    ls = ls_ref[...]
    mx = mx_ref[...]
    ri = lax.broadcasted_iota(jnp.int32, (PAIR, PAIR), 0)
    ci = lax.broadcasted_iota(jnp.int32, (PAIR, PAIR), 1)
    same = (ri // CHUNK) == (ci // CHUNK)
    lower_f = jnp.where(same, jnp.where(ri >= ci, 1.0, 0.0), 0.0).astype(F32)
    strict_f = jnp.where(same, jnp.where(ri > ci, 1.0, 0.0), 0.0).astype(F32)
    eye = jnp.where(ri == ci, 1.0, 0.0).astype(F32)
    onw = onw_ref[...]
    alog, dtb = alog_ref[...], dtb_ref[...]

    def stage_a(p):
        r0 = p * PAIR
        sl = slice(r0, r0 + PAIR)
        ab = ab_ref[sl, :]
        xa = ab + dtb
        softplus = jnp.maximum(xa, 0.0) + jnp.log(1.0 + jnp.exp(-jnp.abs(xa)))
        g_all = -jnp.exp(alog) * softplus
        beta_all = jax.nn.sigmoid(ab)
        for h in hs:
            gb_scr[hd[h], sl, :] = jnp.broadcast_to(g_all[:, h:h + 1], (PAIR, LANES))
            bb_scr[hd[h], sl, :] = jnp.broadcast_to(beta_all[:, HEADS + h:HEADS + h + 1], (PAIR, LANES))
            outs = []
            for part, ref in enumerate((q_ref, k_ref, v_ref)):
                s = part * HEADS + h
                xp_scr[s, SUBLANES + r0:SUBLANES + r0 + PAIR, :] = ref[h, sl, :].astype(F32)
                w = cw_ref[s]
                y = w[0:1] * xp_scr[s, CONV_LEAD + r0:CONV_LEAD + r0 + PAIR, :]
                for tap in range(1, GDN_CONV):
                    y = y + w[tap:tap + 1] * xp_scr[s, CONV_LEAD + tap + r0:CONV_LEAD + tap + r0 + PAIR, :]
                outs.append(y * jax.nn.sigmoid(y))
            qc, kc, vc = outs
            qn_scr[hd[h], sl, :] = (qc * lax.rsqrt(jnp.sum(qc * qc, axis=-1, keepdims=True) + NORM_EPS)
                                * (HEAD_DIM ** -0.5))
            kn_scr[hd[h], sl, :] = kc * lax.rsqrt(jnp.sum(kc * kc, axis=-1, keepdims=True) + NORM_EPS)
            vn_scr[hd[h], sl, :] = vc

    def stage_b(p, hd):
        hs = range(len(hd))
        sl = slice(p * PAIR, (p + 1) * PAIR)
        k = [kn_scr[hd[h], sl, :] for h in hs]
        q = [qn_scr[hd[h], sl, :] for h in hs]
        v = [vn_scr[hd[h], sl, :] for h in hs]
        gb = [gb_scr[hd[h], sl, :] for h in hs]
        bb = [bb_scr[hd[h], sl, :] for h in hs]
        parts = [_split2(jnp.concatenate([gb[h], gb[h]], axis=1) * mx) for h in hs]
        gm = [_dot(ls, pt[0]) + _dot(ls, pt[1]) for pt in parts]
        yield
        decay = [jnp.exp(g[:, :PAIR]) for g in gm]
        gi = [g[:, PAIR:] for g in gm]
        glast = [jnp.concatenate([jnp.broadcast_to(g[c * CHUNK - 1:c * CHUNK, :], (CHUNK, LANES))
                                  for c in (1, 2)], axis=0) for g in gi]
        eg = [jnp.exp(g) for g in gi]
        er = [jnp.exp(glast[h] - gi[h]) for h in hs]
        kb = [k[h] * bb[h] for h in hs]
        kk = [_dot_nt(jnp.concatenate([kb[h], q[h]], axis=0).astype(BF16), k[h].astype(BF16)) for h in hs]
        yield
        n = [-(kk[h][:PAIR] * decay[h] * strict_f) for h in hs]
        aqk = [kk[h][PAIR:] * decay[h] * lower_f for h in hs]
        inv = [eye + n[h] for h in hs]
        nb = [n[h].astype(BF16) for h in hs]
        nb = [_dot(nb[h], nb[h]).astype(BF16) for h in hs]
        yield
        for it in range(5):
            if it < 4:
                m = [_dot(jnp.concatenate([inv[h].astype(BF16), nb[h]], axis=0), nb[h]) for h in hs]
                inv = [inv[h] + m[h][:PAIR] for h in hs]
                nb = [m[h][PAIR:].astype(BF16) for h in hs]
            else:
                inv = [inv[h] + _dot(inv[h].astype(BF16), nb[h]) for h in hs]
            yield
        uw = [_dot(inv[h].astype(BF16),
                   jnp.concatenate([v[h] * bb[h], kb[h] * eg[h]], axis=1).astype(BF16)) for h in hs]
        yield
        kd = [(k[h] * er[h]).astype(BF16) for h in hs]
        qg = [q[h] * eg[h] for h in hs]
        for cc in range(2):
            c = 2 * p + cc
            rs = slice(cc * CHUNK, (cc + 1) * CHUNK)
            kwb = [_dot_tn(kd[h][rs], jnp.concatenate([uw[h][rs, LANES:], uw[h][rs, :LANES]], axis=1).astype(BF16))
                   for h in hs]
            for h in hs:
                wq_scr[hd[h], c] = jnp.concatenate([uw[h][rs, LANES:], qg[h][rs]], axis=0).astype(BF16)
                kw_scr[hd[h], c] = kwb[h][:, :LANES].astype(BF16)
                bm_scr[hd[h], c] = kwb[h][:, LANES:]
                cd_scr[hd[h], c] = jnp.broadcast_to(eg[h][(cc + 1) * CHUNK - 1:(cc + 1) * CHUNK, :], (SUBLANES, LANES))
            yield
        for h in hs:
            u_scr[hd[h], sl, :] = uw[h][:, :LANES]
            aqk_scr[hd[h], p] = aqk[h].astype(BF16)

    def stage_c(p, hd):
        hs = range(len(hd))
        sl = slice(p * PAIR, (p + 1) * PAIR)
        c0, c1 = 2 * p, 2 * p + 1
        s0 = [s_scr[hd[h]] for h in hs]
        sb0 = [s0[h].astype(BF16) for h in hs]
        s1 = [s0[h] * cd_scr[hd[h], c0][0:1, :] + (bm_scr[hd[h], c0] - _dot(kw_scr[hd[h], c0], sb0[h])) for h in hs]
        yield
        sb1 = [s1[h].astype(BF16) for h in hs]
        s2 = [s1[h] * cd_scr[hd[h], c1][0:1, :] + (bm_scr[hd[h], c1] - _dot(kw_scr[hd[h], c1], sb1[h])) for h in hs]
        for h in hs:
            s_scr[hd[h]] = s2[h]
        yield
        a0 = [_dot(wq_scr[hd[h], c0], sb0[h]) for h in hs]
        a1 = [_dot(wq_scr[hd[h], c1], sb1[h]) for h in hs]
        yield
        vnew = [(u_scr[hd[h], sl, :] - jnp.concatenate([a0[h][:CHUNK], a1[h][:CHUNK]], axis=0)).astype(BF16)
                for h in hs]
        o = [jnp.concatenate([a0[h][CHUNK:], a1[h][CHUNK:]], axis=0) + _dot(aqk_scr[hd[h], p], vnew[h]) for h in hs]
        yield
        for h in hs:
            z = z_ref[hd[h], sl, :].astype(F32)
            o_ref[hd[h], sl, :] = (_rms(o[h], onw) * (z * jax.nn.sigmoid(z))).astype(BF16)

    def interleave(*gens):
        live = [g for g in gens if g is not None]
        while live:
            for g in list(live):
                try:
                    next(g)
                except StopIteration:
                    live.remove(g)

    stage_a(0)
    for step in range(npair + 1):
        @pl.when(blk >= 0)
        def _(step=step):
            if step + 1 < npair:
                stage_a(step + 1)
            for g0 in range(0, HEADS, GDN_HEAD_GROUP):
                grp = tuple(range(g0, g0 + GDN_HEAD_GROUP))
                interleave(stage_c(step - 1, grp) if step >= 1 else None,
                           stage_b(step, grp) if step < npair else None)

    for s in range(3 * HEADS):
        xp_scr[s, 0:SUBLANES, :] = xp_scr[s, rows:rows + SUBLANES, :]


def _gdn(p3, ab, conv_w, a_log, dt_bias, out_norm_w, q_blk):
    t = p3.shape[1]
    rows = min(SEQ_BLOCK, t)
    nchunk, npair = rows // CHUNK, rows // PAIR
    cw = conv_w.astype(F32).reshape(GDN_CONV, 3 * HEADS, LANES).transpose(1, 0, 2)
    pad = lambda v: jnp.zeros((1, LANES), F32).at[0, :HEADS].set(v.astype(F32))
    ti = jnp.arange(PAIR)
    same = (ti[:, None] // CHUNK) == (ti[None, :] // CHUNK)
    le = same & (ti[None, :] <= ti[:, None])
    gt = same & (ti[None, :] > ti[:, None])
    ls = le.astype(BF16)
    mx = jnp.concatenate([gt.T.astype(F32), jnp.ones((PAIR, LANES), F32)], axis=1)
    slab = lambda off: pl.BlockSpec((HEADS, rows, LANES), lambda b: (q_blk + off, b, 0))
    const2 = lambda b: (0, 0)
    hr = (HEADS, rows, LANES)
    return pl.pallas_call(
        _gdn_body,
        grid=(t // rows,),
        in_specs=[
            slab(0), slab(1), slab(2), slab(3),
            pl.BlockSpec((rows, LANES), lambda b: (b, 0)),
            pl.BlockSpec(cw.shape, lambda b: (0, 0, 0)),
            pl.BlockSpec((1, LANES), const2),
            pl.BlockSpec((1, LANES), const2),
            pl.BlockSpec((1, LANES), const2),
            pl.BlockSpec(ls.shape, const2),
            pl.BlockSpec(mx.shape, const2),
        ],
        out_specs=pl.BlockSpec(hr, lambda b: (0, b, 0)),
        out_shape=jax.ShapeDtypeStruct((HEADS, t, LANES), BF16),
        scratch_shapes=[
            pltpu.VMEM((HEADS, HEAD_DIM, HEAD_DIM), F32),
            pltpu.VMEM((3 * HEADS, rows + SUBLANES, LANES), F32),
            pltpu.VMEM(hr, F32), pltpu.VMEM(hr, F32), pltpu.VMEM(hr, F32),
            pltpu.VMEM(hr, F32), pltpu.VMEM(hr, F32),
            pltpu.VMEM(hr, F32),
            pltpu.VMEM((HEADS, nchunk, 2 * CHUNK, LANES), BF16),
            pltpu.VMEM((HEADS, npair, PAIR, PAIR), BF16),
            pltpu.VMEM((HEADS, nchunk, HEAD_DIM, HEAD_DIM), BF16),
            pltpu.VMEM((HEADS, nchunk, HEAD_DIM, HEAD_DIM), F32),
            pltpu.VMEM((HEADS, nchunk, SUBLANES, LANES), F32),
        ],
        compiler_params=pltpu.CompilerParams(
            dimension_semantics=("arbitrary",), vmem_limit_bytes=VMEM_LIMIT),
        name="gdn",
    )(p3, p3, p3, p3, ab, cw, pad(a_log), pad(dt_bias), out_norm_w.astype(F32).reshape(1, LANES), ls, mx)


def _pad_to(w, axis, mult):
    n = w.shape[axis]
    extra = (-n) % mult
    if extra == 0:
        return w
    widths = [(0, 0)] * w.ndim
    widths[axis] = (0, extra)
    return jnp.pad(w, widths)


@jax.jit
def _forward(x, ffn1_norm, ffn1_w_gate, ffn1_w_up, ffn1_w_down, mix_norm, w_in, gdn_conv,
             gdn_A_log, gdn_dt_bias, gdn_out_norm, att_q_norm, att_k_norm, att_rel_bias,
             w_branch_gdn, w_branch_att, w_out, ffn2_norm, ffn2_w_gate, ffn2_w_up, ffn2_w_down):
    b, t, d = x.shape
    gw = HEADS * HEAD_DIM
    outs = []
    for bi in range(b):
        xb = x[bi]
        depth = ffn1_norm.shape[0]
        for l in range(depth):
            wi = w_in[l]
            o_ab = 4 * gw
            o_att = o_ab + 2 * HEADS
            w_ab = _pad_to(wi[:, o_ab:o_att].astype(BF16), 1, LANES)

            act, wd = _ffn_up(_norm(xb, ffn1_norm[l]), ffn1_w_gate[l], ffn1_w_up[l], ffn1_w_down[l])
            xb, hn, ab = _ffn_down(act, wd, xb, mix_norm[l], w_ab)
            p3 = _proj(hn, wi.T, o_ab // PROJ_TN, o_att - o_ab, (wi.shape[1] - (o_att - o_ab)) // PROJ_TN)
            og3 = _gdn(p3, ab, gdn_conv[l], gdn_A_log[l], gdn_dt_bias[l], gdn_out_norm[l], 0)
            oa3, (wa, wb, wo) = _att(p3, att_q_norm[l], att_k_norm[l], att_rel_bias[l], 4, 5, 6,
                                     (w_branch_gdn[l], w_branch_att[l], w_out[l]))
            xb, hn = _merge(og3, oa3, p3, 7, xb, wa, wb, wo, ffn2_norm[l])

            act, wd = _ffn_up(hn, ffn2_w_gate[l], ffn2_w_up[l], ffn2_w_down[l])
            xb = _ffn_down(act, wd, xb)
        outs.append(xb)
    return jnp.stack(outs, axis=0)


def kernel(x, ffn1_norm, ffn1_w_gate, ffn1_w_up, ffn1_w_down, mix_norm, w_in, gdn_conv, gdn_A_log, gdn_dt_bias, gdn_out_norm, att_q_norm, att_k_norm, att_rel_bias, w_branch_gdn, w_branch_att, w_out, ffn2_norm, ffn2_w_gate, ffn2_w_up, ffn2_w_down):
    return _forward(x, ffn1_norm, ffn1_w_gate, ffn1_w_up, ffn1_w_down, mix_norm, w_in, gdn_conv,
                    gdn_A_log, gdn_dt_bias, gdn_out_norm, att_q_norm, att_k_norm, att_rel_bias,
                    w_branch_gdn, w_branch_att, w_out, ffn2_norm, ffn2_w_gate, ffn2_w_up, ffn2_w_down)
```

```python
import functools
import math

import jax
import jax.numpy as jnp
from jax import lax
from jax.experimental import pallas as pl
from jax.experimental.pallas import tpu as pltpu

F32 = jnp.float32
BF16 = jnp.bfloat16

NORM_EPS = 1e-6
CHUNK = 64
HEADS = 8
HEAD_DIM = 128
LANES = 128
SUBLANES = 8
BF16_ROWS = 16
GDN_CONV = 4
CONV_LEAD = SUBLANES - (GDN_CONV - 1)
ATT_LEFT_CHUNKS = 8
REL_MAX = 256
NEG = -1e30

MIB = 1024 * 1024
VMEM_V7X = 64 * MIB
VMEM_LIMIT = VMEM_V7X - 8 * MIB

LOG2E = math.log2(math.e)

NORM_TM = 512
FFN_UP_TM = 2048
FFN_TF = 512
FFN_COL_GROUP = 256
FFN_DOWN_TM = 512
FFN_DOWN_NORM_TM = 512
FFN_DOWN_NORM_VMEM = VMEM_V7X - 2 * MIB
ROW_GROUP = 256
PROJ_TM = 2048
PROJ_TN = 1024
ATT_HEADS_PER_ITER = 2
MERGE_TM = 512
SEQ_BLOCK = 512
ATT_QG = 128
ATT_WIN = ATT_QG + ATT_LEFT_CHUNKS * CHUNK
ATT_TBL = 1024
PAIR = 2 * CHUNK
GDN_HEAD_GROUP = 8


def _rms(x, w):
    return x * lax.rsqrt(jnp.mean(x * x, axis=-1, keepdims=True) + NORM_EPS) * w


def _dot(a, b):
    return jnp.dot(a, b, preferred_element_type=F32)


def _dot_nt(a, b):
    return lax.dot_general(a, b, (((1,), (1,)), ((), ())), preferred_element_type=F32)


def _dot_tn(a, b):
    return lax.dot_general(a, b, (((0,), (0,)), ((), ())), preferred_element_type=F32)


def _norm_body(x_ref, nw_ref, o_ref):
    o_ref[...] = _rms(x_ref[...], nw_ref[...]).astype(BF16)


def _norm(x, norm_w):
    t, d = x.shape
    tm = min(NORM_TM, t)
    return pl.pallas_call(
        _norm_body,
        grid=(t // tm,),
        in_specs=[pl.BlockSpec((tm, d), lambda i: (i, 0)), pl.BlockSpec((1, d), lambda i: (0, 0))],
        out_specs=pl.BlockSpec((tm, d), lambda i: (i, 0)),
        out_shape=jax.ShapeDtypeStruct((t, d), BF16),
        compiler_params=pltpu.CompilerParams(dimension_semantics=("parallel",), vmem_limit_bytes=VMEM_LIMIT),
        name="norm",
    )(x, norm_w.reshape(1, d))


def _ffn_up_body(h_ref, wg_ref, wu_ref, wd_ref, o_ref, wdb_ref, w_scr, *, n_side):
    tf = wg_ref.shape[1]

    @pl.when(pl.program_id(1) == 0)
    def _():
        w_scr[:, :tf] = wg_ref[...].astype(BF16)
        w_scr[:, tf:] = wu_ref[...].astype(BF16)

    @pl.when(pl.program_id(0) * pl.num_programs(1) + pl.program_id(1) < n_side)
    def _():
        wdb_ref[...] = wd_ref[...].astype(BF16)

    for n0 in range(0, tf, FFN_COL_GROUP):
        g = _dot(h_ref[...], w_scr[:, n0:n0 + FFN_COL_GROUP])
        u = _dot(h_ref[...], w_scr[:, tf + n0:tf + n0 + FFN_COL_GROUP])
        o_ref[:, n0:n0 + FFN_COL_GROUP] = (g * jax.nn.sigmoid(g) * u).astype(BF16)


def _ffn_up(h, wg, wu, wd):
    t, d = h.shape
    f = wg.shape[1]
    tm, tf = min(FFN_UP_TM, t), FFN_TF
    nj, ni = pl.cdiv(f, tf), t // tm
    rows = next(r for r in range(BF16_ROWS, f + 1, BF16_ROWS) if f % r == 0 and f // r <= nj * ni)
    n_side = f // rows
    side = pl.BlockSpec((rows, d), lambda j, i: (jnp.minimum(j * ni + i, n_side - 1), 0))
    return pl.pallas_call(
        functools.partial(_ffn_up_body, n_side=n_side),
        grid=(nj, ni),
        in_specs=[
            pl.BlockSpec((tm, d), lambda j, i: (i, 0)),
            pl.BlockSpec((d, tf), lambda j, i: (0, j)),
            pl.BlockSpec((d, tf), lambda j, i: (0, j)),
            side,
        ],
        out_specs=[pl.BlockSpec((tm, tf), lambda j, i: (i, j)), side],
        out_shape=[jax.ShapeDtypeStruct((t, f), BF16), jax.ShapeDtypeStruct((f, d), BF16)],
        scratch_shapes=[pltpu.VMEM((d, 2 * tf), BF16)],
        compiler_params=pltpu.CompilerParams(
            dimension_semantics=("arbitrary", "arbitrary"), vmem_limit_bytes=VMEM_LIMIT),
        name="ffn_up",
    )(h, wg, wu, wd)


def _ffn_down_body(a_ref, wd_ref, x_ref, o_ref):
    o_ref[...] = x_ref[...] + 0.5 * _dot(a_ref[...], wd_ref[...])


def _ffn_down_norm_body(a_ref, wd_ref, x_ref, nw_ref, o_ref, hn_ref):
    rows = a_ref.shape[0]
    group = min(ROW_GROUP, rows)
    for r0 in range(0, rows, group):
        rs = slice(r0, r0 + group)
        y = x_ref[rs, :] + 0.5 * _dot(a_ref[rs, :], wd_ref[...])
        o_ref[rs, :] = y
        hn_ref[rs, :] = _rms(y, nw_ref[...]).astype(BF16)


def _ffn_down(act, wd, x, next_norm_w=None):
    t, d = x.shape
    f = act.shape[1]
    tm = min(FFN_DOWN_TM if next_norm_w is None else FFN_DOWN_NORM_TM, t)
    row = pl.BlockSpec((tm, d), lambda i: (i, 0))
    in_specs = [pl.BlockSpec((tm, f), lambda i: (i, 0)),
                pl.BlockSpec((f, d), lambda i: (0, 0), pipeline_mode=pl.Buffered(1)),
                row]
    params = pltpu.CompilerParams(dimension_semantics=("parallel",), vmem_limit_bytes=VMEM_LIMIT)
    if next_norm_w is None:
        return pl.pallas_call(
            _ffn_down_body, grid=(t // tm,), in_specs=in_specs, out_specs=row,
            out_shape=jax.ShapeDtypeStruct((t, d), F32), compiler_params=params, name="ffn_down",
        )(act, wd, x)
    return pl.pallas_call(
        _ffn_down_norm_body, grid=(t // tm,),
        in_specs=in_specs + [pl.BlockSpec((1, d), lambda i: (0, 0))],
        out_specs=[row, row],
        out_shape=[jax.ShapeDtypeStruct((t, d), F32), jax.ShapeDtypeStruct((t, d), BF16)],
        compiler_params=pltpu.CompilerParams(dimension_semantics=("parallel",),
                                             vmem_limit_bytes=FFN_DOWN_NORM_VMEM),
        name="ffn_down_norm",
    )(act, wd, x, next_norm_w.reshape(1, d))


def _proj_body(h_ref, wf_ref, wx_ref, p_ref, w_scr, *, n_aligned, shift):
    j, i = pl.program_id(0), pl.program_id(1)
    tn = wf_ref.shape[0]

    @pl.when(jnp.logical_and(i == 0, j < n_aligned))
    def _():
        w_scr[...] = wf_ref[...].astype(BF16)

    @pl.when(jnp.logical_and(i == 0, j >= n_aligned))
    def _():
        w_scr[:tn - shift, :] = wf_ref[shift:, :].astype(BF16)
        w_scr[tn - shift:, :] = wx_ref[:shift, :].astype(BF16)

    half = tn // 2
    for n0 in (0, half):
        r = _dot_nt(h_ref[...], w_scr[n0:n0 + half, :])
        for c in range(half // LANES):
            p_ref[n0 // LANES + c] = r[:, c * LANES:(c + 1) * LANES].astype(BF16)


def _proj(h, wt, n_aligned, shift, n_tiles):
    t, d = h.shape
    tm, tn = min(PROJ_TM, t), PROJ_TN
    assert 0 < shift < LANES and shift % BF16_ROWS == 0 and wt.shape[0] == n_tiles * tn + shift
    spb = tn // LANES
    return pl.pallas_call(
        functools.partial(_proj_body, n_aligned=n_aligned, shift=shift),
        grid=(n_tiles, t // tm),
        in_specs=[
            pl.BlockSpec((tm, d), lambda j, i: (i, 0)),
            pl.BlockSpec((tn, d), lambda j, i: (j, 0)),
            pl.BlockSpec((LANES, d), lambda j, i: ((j + 1) * spb, 0)),
        ],
        out_specs=pl.BlockSpec((spb, tm, LANES), lambda j, i: (j, i, 0)),
        out_shape=jax.ShapeDtypeStruct((n_tiles * spb, t, LANES), BF16),
        scratch_shapes=[pltpu.VMEM((tn, d), BF16)],
        compiler_params=pltpu.CompilerParams(
            dimension_semantics=("arbitrary", "arbitrary"), vmem_limit_bytes=VMEM_LIMIT),
        name="proj",
    )(h, wt, wt)


def _merge_body(og_ref, oa_ref, gg0_ref, gg1_ref, ga0_ref, ga1_ref, x_ref, wa_ref, wb_ref, wo_ref, nw_ref,
                o_ref, hn_ref):
    def slabs(*refs):
        return jnp.concatenate([ref[c] for ref in refs for c in range(ref.shape[0])], axis=-1)

    ya = _dot(slabs(og_ref), wa_ref[...])
    yb = _dot(slabs(oa_ref), wb_ref[...])
    m = (jax.nn.sigmoid(slabs(gg0_ref, gg1_ref).astype(F32)) * ya
         + jax.nn.sigmoid(slabs(ga0_ref, ga1_ref).astype(F32)) * yb)
    y = x_ref[...] + _dot(m.astype(BF16), wo_ref[...])
    o_ref[...] = y
    hn_ref[...] = _rms(y, nw_ref[...]).astype(BF16)


def _merge(og3, oa3, p3, gate_blk, x, wa, wb, wo, next_norm_w):
    t, d = x.shape
    tm = min(MERGE_TM, t)
    assert d == 2 * HEADS * LANES
    const = lambda i: (0, 0)
    row = pl.BlockSpec((tm, d), lambda i: (i, 0))
    slab = lambda blk: pl.BlockSpec((HEADS, tm, LANES), lambda i: (blk, i, 0))
    return pl.pallas_call(
        _merge_body,
        grid=(t // tm,),
        in_specs=[
            slab(0), slab(0),
            slab(gate_blk), slab(gate_blk + 1), slab(gate_blk + 2), slab(gate_blk + 3),
            row,
            pl.BlockSpec(wa.shape, const, pipeline_mode=pl.Buffered(1)),
            pl.BlockSpec(wb.shape, const, pipeline_mode=pl.Buffered(1)),
            pl.BlockSpec(wo.shape, const, pipeline_mode=pl.Buffered(1)),
            pl.BlockSpec((1, d), const),
        ],
        out_specs=[row, row],
        out_shape=[jax.ShapeDtypeStruct((t, d), F32), jax.ShapeDtypeStruct((t, d), BF16)],
        compiler_params=pltpu.CompilerParams(
            dimension_semantics=("parallel",), vmem_limit_bytes=VMEM_LIMIT),
        name="merge",
    )(og3, oa3, p3, p3, p3, p3, x, wa, wb, wo, next_norm_w.reshape(1, d))


def _att_body(q_ref, k_ref, v_ref, qw_ref, kw_ref, bvec_ref, w0_ref, w1_ref, w2_ref,
              o_ref, w0b_ref, w1b_ref, w2b_ref, kbuf, vbuf, bias_scr):
    blk = pl.program_id(0)
    rb = q_ref.shape[1]

    w0b_ref[...] = w0_ref[...].astype(BF16)
    w1b_ref[...] = w1_ref[...].astype(BF16)
    w2b_ref[...] = w2_ref[...].astype(BF16)

    @pl.when(blk == 0)
    def _():
        kbuf[:, :rb, :] = jnp.zeros((HEADS, rb, LANES), BF16)
        vbuf[:, :rb, :] = jnp.zeros((HEADS, rb, LANES), BF16)
        qc = lax.broadcasted_iota(jnp.int32, (ATT_QG, ATT_WIN), 0) // CHUNK
        kc = lax.broadcasted_iota(jnp.int32, (ATT_QG, ATT_WIN), 1) // CHUNK
        off = kc - qc
        band = jnp.where(off >= 0, jnp.where(off <= ATT_LEFT_CHUNKS, 0.0, NEG), NEG).astype(F32)
        for h in range(HEADS):
            tbl = jnp.broadcast_to(bvec_ref[h], (ATT_QG, ATT_TBL))
            tbl = pltpu.roll(tbl, 0, 1, stride=1, stride_axis=0)
            bias_scr[h] = tbl[:, :ATT_WIN] * LOG2E + band

    first_neg = jnp.where(blk == 0, NEG, 0.0).astype(F32)
    col = lax.broadcasted_iota(jnp.int32, (1, ATT_WIN), 1)
    qw = qw_ref[...] * (HEAD_DIM ** -0.5 * LOG2E)
    kw = kw_ref[...]
    groups = [(dh, g * ATT_QG) for dh in range(ATT_HEADS_PER_ITER) for g in range(rb // ATT_QG)]

    def heads(hi, carry):
        h0 = hi * ATT_HEADS_PER_ITER
        for dh in range(ATT_HEADS_PER_ITER):
            kbuf[h0 + dh, rb:, :] = _rms(k_ref[h0 + dh].astype(F32), kw).astype(BF16)
            vbuf[h0 + dh, rb:, :] = v_ref[h0 + dh]
        qn = [_rms(q_ref[h0 + dh, r0:r0 + ATT_QG, :].astype(F32), qw).astype(BF16) for dh, r0 in groups]
        s = [_dot_nt(qn[n], kbuf[h0 + dh, r0:r0 + ATT_WIN, :]) for n, (dh, r0) in enumerate(groups)]
        s = [s[n] + bias_scr[h0 + dh] + jnp.where(col < rb - r0, first_neg, 0.0)
             for n, (dh, r0) in enumerate(groups)]
        p = [jnp.exp2(sn - jnp.max(sn, axis=-1, keepdims=True)) for sn in s]
        l = [jnp.sum(pn, axis=-1, keepdims=True) for pn in p]
        o = [_dot(p[n].astype(BF16), vbuf[h0 + dh, r0:r0 + ATT_WIN, :]) for n, (dh, r0) in enumerate(groups)]
        for n, (dh, r0) in enumerate(groups):
            o_ref[h0 + dh, r0:r0 + ATT_QG, :] = (o[n] / l[n]).astype(BF16)
        return carry

    lax.fori_loop(0, HEADS // ATT_HEADS_PER_ITER, heads, 0)
    kbuf[:, :rb, :] = kbuf[:, rb:, :]
    vbuf[:, :rb, :] = vbuf[:, rb:, :]


def _att_bias_table(rel_bias):
    tbl = rel_bias.astype(F32)
    n_tbl = tbl.shape[1]
    far = ATT_LEFT_CHUNKS * CHUNK - REL_MAX
    rep = lambda c, n: jnp.broadcast_to(tbl[:, c:c + 1], (tbl.shape[0], n))
    vec = jnp.concatenate([rep(n_tbl - 1, far), tbl[:, ::-1],
                           rep(0, ATT_TBL - far - n_tbl - (ATT_QG - 1)), rep(n_tbl - 1, ATT_QG - 1)], axis=1)
    return vec[:, None, :]


def _att(p3, q_norm_w, k_norm_w, rel_bias, q_blk, k_blk, v_blk, side_weights):
    t = p3.shape[1]
    rb = SEQ_BLOCK
    assert rb == ATT_LEFT_CHUNKS * CHUNK and t % rb == 0
    nblk = t // rb
    bvec = _att_bias_table(rel_bias)
    assert all(w.shape[0] % (BF16_ROWS * nblk) == 0 for w in side_weights)
    side_specs = [pl.BlockSpec((w.shape[0] // nblk, w.shape[1]), lambda b: (b, 0)) for w in side_weights]
    outs = pl.pallas_call(
        _att_body,
        grid=(nblk,),
        in_specs=[
            pl.BlockSpec((HEADS, rb, LANES), lambda b: (q_blk, b, 0)),
            pl.BlockSpec((HEADS, rb, LANES), lambda b: (k_blk, b, 0)),
            pl.BlockSpec((HEADS, rb, LANES), lambda b: (v_blk, b, 0)),
            pl.BlockSpec((1, LANES), lambda b: (0, 0)),
            pl.BlockSpec((1, LANES), lambda b: (0, 0)),
            pl.BlockSpec(bvec.shape, lambda b: (0, 0, 0)),
        ] + side_specs,
        out_specs=[pl.BlockSpec((HEADS, rb, LANES), lambda b: (0, b, 0))] + side_specs,
        out_shape=[jax.ShapeDtypeStruct((HEADS, t, LANES), BF16)]
                  + [jax.ShapeDtypeStruct(w.shape, BF16) for w in side_weights],
        scratch_shapes=[pltpu.VMEM((HEADS, 2 * rb, LANES), BF16),
                        pltpu.VMEM((HEADS, 2 * rb, LANES), BF16),
                        pltpu.VMEM((HEADS, ATT_QG, ATT_WIN), F32)],
        compiler_params=pltpu.CompilerParams(
            dimension_semantics=("arbitrary",), vmem_limit_bytes=VMEM_LIMIT),
        name="att",
    )(p3, p3, p3, q_norm_w.reshape(1, LANES), k_norm_w.reshape(1, LANES), bvec, *side_weights)
    return outs[0], outs[1:]


def _split2(x):
    hi = x.astype(BF16)
    return hi, (x - hi.astype(F32)).astype(BF16)


def _gdn_body(q_ref, k_ref, v_ref, z_ref, hn_ref, wab_ref, cw_ref, alog_ref, dtb_ref, onw_ref, ls_ref, mx_ref,
              o_ref,
              s_scr, xp_scr, qn_scr, kn_scr, vn_scr, gb_scr, bb_scr, u_scr, wq_scr, aqk_scr, kw_scr,
              bm_scr, cd_scr):
    blk = pl.program_id(0)
    rows = q_ref.shape[1]
    npair = rows // PAIR
    hs = range(HEADS)
    hd = tuple(hs)

    @pl.when(blk == 0)
    def _():
        s_scr[...] = jnp.zeros_like(s_scr)
        xp_scr[:, 0:SUBLANES, :] = jnp.zeros((3 * HEADS, SUBLANES, LANES), F32)

    ls = ls_ref[...]
    mx = mx_ref[...]
    ri = lax.broadcasted_iota(jnp.int32, (PAIR, PAIR), 0)
    ci = lax.broadcasted_iota(jnp.int32, (PAIR, PAIR), 1)
    same = (ri // CHUNK) == (ci // CHUNK)
    lower_f = jnp.where(same, jnp.where(ri >= ci, 1.0, 0.0), 0.0).astype(F32)
    strict_f = jnp.where(same, jnp.where(ri > ci, 1.0, 0.0), 0.0).astype(F32)
    eye = jnp.where(ri == ci, 1.0, 0.0).astype(F32)
    onw = onw_ref[...]
    alog, dtb = alog_ref[...], dtb_ref[...]

    def stage_a(p):
        r0 = p * PAIR
        sl = slice(r0, r0 + PAIR)
        ab = _dot(hn_ref[sl, :], wab_ref[...])
        xa = ab + dtb
        softplus = jnp.maximum(xa, 0.0) + jnp.log(1.0 + jnp.exp(-jnp.abs(xa)))
        g_all = -jnp.exp(alog) * softplus
        beta_all = jax.nn.sigmoid(ab)
        for h in hs:
            gb_scr[hd[h], sl, :] = jnp.broadcast_to(g_all[:, h:h + 1], (PAIR, LANES))
            bb_scr[hd[h], sl, :] = jnp.broadcast_to(beta_all[:, HEADS + h:HEADS + h + 1], (PAIR, LANES))
            outs = []
            for part, ref in enumerate((q_ref, k_ref, v_ref)):
                s = part * HEADS + h
                xp_scr[s, SUBLANES + r0:SUBLANES + r0 + PAIR, :] = ref[h, sl, :].astype(F32)
                w = cw_ref[s]
                y = w[0:1] * xp_scr[s, CONV_LEAD + r0:CONV_LEAD + r0 + PAIR, :]
                for tap in range(1, GDN_CONV):
                    y = y + w[tap:tap + 1] * xp_scr[s, CONV_LEAD + tap + r0:CONV_LEAD + tap + r0 + PAIR, :]
                outs.append(y * jax.nn.sigmoid(y))
            qc, kc, vc = outs
            qn_scr[hd[h], sl, :] = (qc * lax.rsqrt(jnp.sum(qc * qc, axis=-1, keepdims=True) + NORM_EPS)
                                * (HEAD_DIM ** -0.5))
            kn_scr[hd[h], sl, :] = kc * lax.rsqrt(jnp.sum(kc * kc, axis=-1, keepdims=True) + NORM_EPS)
            vn_scr[hd[h], sl, :] = vc

    def stage_b(p, hd):
        hs = range(len(hd))
        sl = slice(p * PAIR, (p + 1) * PAIR)
        k = [kn_scr[hd[h], sl, :] for h in hs]
        q = [qn_scr[hd[h], sl, :] for h in hs]
        v = [vn_scr[hd[h], sl, :] for h in hs]
        gb = [gb_scr[hd[h], sl, :] for h in hs]
        bb = [bb_scr[hd[h], sl, :] for h in hs]
        parts = [_split2(jnp.concatenate([gb[h], gb[h]], axis=1) * mx) for h in hs]
        gm = [_dot(ls, pt[0]) + _dot(ls, pt[1]) for pt in parts]
        yield
        decay = [jnp.exp(g[:, :PAIR]) for g in gm]
        gi = [g[:, PAIR:] for g in gm]
        glast = [jnp.concatenate([jnp.broadcast_to(g[c * CHUNK - 1:c * CHUNK, :], (CHUNK, LANES))
                                  for c in (1, 2)], axis=0) for g in gi]
        eg = [jnp.exp(g) for g in gi]
        er = [jnp.exp(glast[h] - gi[h]) for h in hs]
        kb = [k[h] * bb[h] for h in hs]
        kk = [_dot_nt(jnp.concatenate([kb[h], q[h]], axis=0).astype(BF16), k[h].astype(BF16)) for h in hs]
        yield
        n = [-(kk[h][:PAIR] * decay[h] * strict_f) for h in hs]
        aqk = [kk[h][PAIR:] * decay[h] * lower_f for h in hs]
        inv = [eye + n[h] for h in hs]
        nb = [n[h].astype(BF16) for h in hs]
        nb = [_dot(nb[h], nb[h]).astype(BF16) for h in hs]
        yield
        for it in range(5):
            if it < 4:
                m = [_dot(jnp.concatenate([inv[h].astype(BF16), nb[h]], axis=0), nb[h]) for h in hs]
                inv = [inv[h] + m[h][:PAIR] for h in hs]
                nb = [m[h][PAIR:].astype(BF16) for h in hs]
            else:
                inv = [inv[h] + _dot(inv[h].astype(BF16), nb[h]) for h in hs]
            yield
        uw = [_dot(inv[h].astype(BF16),
                   jnp.concatenate([v[h] * bb[h], kb[h] * eg[h]], axis=1).astype(BF16)) for h in hs]
        yield
        kd = [(k[h] * er[h]).astype(BF16) for h in hs]
        qg = [q[h] * eg[h] for h in hs]
        for cc in range(2):
            c = 2 * p + cc
            rs = slice(cc * CHUNK, (cc + 1) * CHUNK)
            kwb = [_dot_tn(kd[h][rs], jnp.concatenate([uw[h][rs, LANES:], uw[h][rs, :LANES]], axis=1).astype(BF16))
                   for h in hs]
            for h in hs:
                wq_scr[hd[h], c] = jnp.concatenate([uw[h][rs, LANES:], qg[h][rs]], axis=0).astype(BF16)
                kw_scr[hd[h], c] = kwb[h][:, :LANES].astype(BF16)
                bm_scr[hd[h], c] = kwb[h][:, LANES:]
                cd_scr[hd[h], c] = jnp.broadcast_to(eg[h][(cc + 1) * CHUNK - 1:(cc + 1) * CHUNK, :], (SUBLANES, LANES))
            yield
        for h in hs:
            u_scr[hd[h], sl, :] = uw[h][:, :LANES]
            aqk_scr[hd[h], p] = aqk[h].astype(BF16)

    def stage_c(p, hd):
        hs = range(len(hd))
        sl = slice(p * PAIR, (p + 1) * PAIR)
        c0, c1 = 2 * p, 2 * p + 1
        s0 = [s_scr[hd[h]] for h in hs]
        sb0 = [s0[h].astype(BF16) for h in hs]
        s1 = [s0[h] * cd_scr[hd[h], c0][0:1, :] + (bm_scr[hd[h], c0] - _dot(kw_scr[hd[h], c0], sb0[h])) for h in hs]
        yield
        sb1 = [s1[h].astype(BF16) for h in hs]
        s2 = [s1[h] * cd_scr[hd[h], c1][0:1, :] + (bm_scr[hd[h], c1] - _dot(kw_scr[hd[h], c1], sb1[h])) for h in hs]
        for h in hs:
            s_scr[hd[h]] = s2[h]
        yield
        a0 = [_dot(wq_scr[hd[h], c0], sb0[h]) for h in hs]
        a1 = [_dot(wq_scr[hd[h], c1], sb1[h]) for h in hs]
        yield
        vnew = [(u_scr[hd[h], sl, :] - jnp.concatenate([a0[h][:CHUNK], a1[h][:CHUNK]], axis=0)).astype(BF16)
                for h in hs]
        o = [jnp.concatenate([a0[h][CHUNK:], a1[h][CHUNK:]], axis=0) + _dot(aqk_scr[hd[h], p], vnew[h]) for h in hs]
        yield
        for h in hs:
            z = z_ref[hd[h], sl, :].astype(F32)
            o_ref[hd[h], sl, :] = (_rms(o[h], onw) * (z * jax.nn.sigmoid(z))).astype(BF16)

    def interleave(*gens):
        live = [g for g in gens if g is not None]
        while live:
            for g in list(live):
                try:
                    next(g)
                except StopIteration:
                    live.remove(g)

    stage_a(0)
    for step in range(npair + 1):
        if step + 1 < npair:
            stage_a(step + 1)
        for g0 in range(0, HEADS, GDN_HEAD_GROUP):
            grp = tuple(range(g0, g0 + GDN_HEAD_GROUP))
            interleave(stage_c(step - 1, grp) if step >= 1 else None, stage_b(step, grp) if step < npair else None)

    for s in range(3 * HEADS):
        xp_scr[s, 0:SUBLANES, :] = xp_scr[s, rows:rows + SUBLANES, :]


def _gdn(p3, hn, w_ab, conv_w, a_log, dt_bias, out_norm_w, q_blk):
    t = p3.shape[1]
    rows = min(SEQ_BLOCK, t)
    nchunk, npair = rows // CHUNK, rows // PAIR
    cw = conv_w.astype(F32).reshape(GDN_CONV, 3 * HEADS, LANES).transpose(1, 0, 2)
    pad = lambda v: jnp.zeros((1, LANES), F32).at[0, :HEADS].set(v.astype(F32))
    ti = jnp.arange(PAIR)
    same = (ti[:, None] // CHUNK) == (ti[None, :] // CHUNK)
    le = same & (ti[None, :] <= ti[:, None])
    gt = same & (ti[None, :] > ti[:, None])
    ls = le.astype(BF16)
    mx = jnp.concatenate([gt.T.astype(F32), jnp.ones((PAIR, LANES), F32)], axis=1)
    slab = lambda off: pl.BlockSpec((HEADS, rows, LANES), lambda b: (q_blk + off, b, 0))
    const2 = lambda b: (0, 0)
    hr = (HEADS, rows, LANES)
    return pl.pallas_call(
        _gdn_body,
        grid=(t // rows,),
        in_specs=[
            slab(0), slab(1), slab(2), slab(3),
            pl.BlockSpec((rows, hn.shape[1]), lambda b: (b, 0)),
            pl.BlockSpec(w_ab.shape, const2),
            pl.BlockSpec(cw.shape, lambda b: (0, 0, 0)),
            pl.BlockSpec((1, LANES), const2),
            pl.BlockSpec((1, LANES), const2),
            pl.BlockSpec((1, LANES), const2),
            pl.BlockSpec(ls.shape, const2),
            pl.BlockSpec(mx.shape, const2),
        ],
        out_specs=pl.BlockSpec(hr, lambda b: (0, b, 0)),
        out_shape=jax.ShapeDtypeStruct((HEADS, t, LANES), BF16),
        scratch_shapes=[
            pltpu.VMEM((HEADS, HEAD_DIM, HEAD_DIM), F32),
            pltpu.VMEM((3 * HEADS, rows + SUBLANES, LANES), F32),
            pltpu.VMEM(hr, F32), pltpu.VMEM(hr, F32), pltpu.VMEM(hr, F32),
            pltpu.VMEM(hr, F32), pltpu.VMEM(hr, F32),
            pltpu.VMEM(hr, F32),
            pltpu.VMEM((HEADS, nchunk, 2 * CHUNK, LANES), BF16),
            pltpu.VMEM((HEADS, npair, PAIR, PAIR), BF16),
            pltpu.VMEM((HEADS, nchunk, HEAD_DIM, HEAD_DIM), BF16),
            pltpu.VMEM((HEADS, nchunk, HEAD_DIM, HEAD_DIM), F32),
            pltpu.VMEM((HEADS, nchunk, SUBLANES, LANES), F32),
        ],
        compiler_params=pltpu.CompilerParams(
            dimension_semantics=("arbitrary",), vmem_limit_bytes=VMEM_LIMIT),
        name="gdn",
    )(p3, p3, p3, p3, hn, w_ab, cw, pad(a_log), pad(dt_bias), out_norm_w.astype(F32).reshape(1, LANES), ls, mx)


def _pad_to(w, axis, mult):
    n = w.shape[axis]
    extra = (-n) % mult
    if extra == 0:
        return w
    widths = [(0, 0)] * w.ndim
    widths[axis] = (0, extra)
    return jnp.pad(w, widths)


@jax.jit
def _forward(x, ffn1_norm, ffn1_w_gate, ffn1_w_up, ffn1_w_down, mix_norm, w_in, gdn_conv,
             gdn_A_log, gdn_dt_bias, gdn_out_norm, att_q_norm, att_k_norm, att_rel_bias,
             w_branch_gdn, w_branch_att, w_out, ffn2_norm, ffn2_w_gate, ffn2_w_up, ffn2_w_down):
    b, t, d = x.shape
    gw = HEADS * HEAD_DIM
    outs = []
    for bi in range(b):
        xb = x[bi]
        depth = ffn1_norm.shape[0]
        for l in range(depth):
            wi = w_in[l]
            o_ab = 4 * gw
            o_att = o_ab + 2 * HEADS
            w_ab = _pad_to(wi[:, o_ab:o_att].astype(BF16), 1, LANES)

            act, wd = _ffn_up(_norm(xb, ffn1_norm[l]), ffn1_w_gate[l], ffn1_w_up[l], ffn1_w_down[l])
            xb, hn = _ffn_down(act, wd, xb, mix_norm[l])
            p3 = _proj(hn, wi.T, o_ab // PROJ_TN, o_att - o_ab, (wi.shape[1] - (o_att - o_ab)) // PROJ_TN)
            og3 = _gdn(p3, hn, w_ab, gdn_conv[l], gdn_A_log[l], gdn_dt_bias[l], gdn_out_norm[l], 0)
            oa3, (wa, wb, wo) = _att(p3, att_q_norm[l], att_k_norm[l], att_rel_bias[l], 4, 5, 6,
                                     (w_branch_gdn[l], w_branch_att[l], w_out[l]))
            xb, hn = _merge(og3, oa3, p3, 7, xb, wa, wb, wo, ffn2_norm[l])

            act, wd = _ffn_up(hn, ffn2_w_gate[l], ffn2_w_up[l], ffn2_w_down[l])
            xb = _ffn_down(act, wd, xb)
        outs.append(xb)
    return jnp.stack(outs, axis=0)


def kernel(x, ffn1_norm, ffn1_w_gate, ffn1_w_up, ffn1_w_down, mix_norm, w_in, gdn_conv, gdn_A_log, gdn_dt_bias, gdn_out_norm, att_q_norm, att_k_norm, att_rel_bias, w_branch_gdn, w_branch_att, w_out, ffn2_norm, ffn2_w_gate, ffn2_w_up, ffn2_w_down):
    return _forward(x, ffn1_norm, ffn1_w_gate, ffn1_w_up, ffn1_w_down, mix_norm, w_in, gdn_conv,
                    gdn_A_log, gdn_dt_bias, gdn_out_norm, att_q_norm, att_k_norm, att_rel_bias,
                    w_branch_gdn, w_branch_att, w_out, ffn2_norm, ffn2_w_gate, ffn2_w_up, ffn2_w_down)
```

```python
import functools
import math

import jax
import jax.numpy as jnp
from jax import lax
from jax.experimental import pallas as pl
from jax.experimental.pallas import tpu as pltpu

F32 = jnp.float32
BF16 = jnp.bfloat16

NORM_EPS = 1e-6
CHUNK = 64
HEADS = 8
HEAD_DIM = 128
LANES = 128
SUBLANES = 8
BF16_ROWS = 16
GDN_CONV = 4
CONV_LEAD = SUBLANES - (GDN_CONV - 1)
ATT_LEFT_CHUNKS = 8
REL_MAX = 256
NEG = -1e30

MIB = 1024 * 1024
VMEM_V7X = 64 * MIB
VMEM_LIMIT = VMEM_V7X - 8 * MIB

LOG2E = math.log2(math.e)

NORM_TM = 512
FFN_UP_TM = 2048
FFN_UP_NORM_TM = 1024
FFN_TF = 512
FFN_COL_GROUP = 256
FFN_DOWN_TM = 512
FFN_DOWN_NORM_TM = 512
FFN_DOWN_NORM_VMEM = VMEM_V7X - 2 * MIB
ROW_GROUP = 256
PROJ_TM = 2048
PROJ_TN = 1024
ATT_HEADS_PER_ITER = 2
MERGE_TM = 512
SEQ_BLOCK = 512
ATT_QG = 128
ATT_WIN = ATT_QG + ATT_LEFT_CHUNKS * CHUNK
ATT_TBL = 1024
PAIR = 2 * CHUNK
GDN_HEAD_GROUP = 8


def _rms(x, w):
    return x * lax.rsqrt(jnp.mean(x * x, axis=-1, keepdims=True) + NORM_EPS) * w


def _dot(a, b):
    return jnp.dot(a, b, preferred_element_type=F32)


def _dot_nt(a, b):
    return lax.dot_general(a, b, (((1,), (1,)), ((), ())), preferred_element_type=F32)


def _dot_tn(a, b):
    return lax.dot_general(a, b, (((0,), (0,)), ((), ())), preferred_element_type=F32)


def _norm_body(x_ref, nw_ref, o_ref):
    o_ref[...] = _rms(x_ref[...], nw_ref[...]).astype(BF16)


def _norm(x, norm_w):
    t, d = x.shape
    tm = min(NORM_TM, t)
    return pl.pallas_call(
        _norm_body,
        grid=(t // tm,),
        in_specs=[pl.BlockSpec((tm, d), lambda i: (i, 0)), pl.BlockSpec((1, d), lambda i: (0, 0))],
        out_specs=pl.BlockSpec((tm, d), lambda i: (i, 0)),
        out_shape=jax.ShapeDtypeStruct((t, d), BF16),
        compiler_params=pltpu.CompilerParams(dimension_semantics=("parallel",), vmem_limit_bytes=VMEM_LIMIT),
        name="norm",
    )(x, norm_w.reshape(1, d))


def _ffn_up_body(h_ref, wg_ref, wu_ref, wd_ref, o_ref, wdb_ref, w_scr, *, n_side):
    tf = wg_ref.shape[1]

    @pl.when(pl.program_id(1) == 0)
    def _():
        w_scr[:, :tf] = wg_ref[...].astype(BF16)
        w_scr[:, tf:] = wu_ref[...].astype(BF16)

    @pl.when(pl.program_id(0) * pl.num_programs(1) + pl.program_id(1) < n_side)
    def _():
        wdb_ref[...] = wd_ref[...].astype(BF16)

    for n0 in range(0, tf, FFN_COL_GROUP):
        g = _dot(h_ref[...], w_scr[:, n0:n0 + FFN_COL_GROUP])
        u = _dot(h_ref[...], w_scr[:, tf + n0:tf + n0 + FFN_COL_GROUP])
        o_ref[:, n0:n0 + FFN_COL_GROUP] = (g * jax.nn.sigmoid(g) * u).astype(BF16)


def _ffn_up_norm_body(x_ref, nw_ref, wg_ref, wu_ref, wd_ref, o_ref, wdb_ref, w_scr, h_scr, *, n_side):
    h_scr[...] = _rms(x_ref[...], nw_ref[...]).astype(BF16)
    _ffn_up_body(h_scr, wg_ref, wu_ref, wd_ref, o_ref, wdb_ref, w_scr, n_side=n_side)


def _ffn_up(h, wg, wu, wd, norm_w=None):
    t, d = h.shape
    f = wg.shape[1]
    fused = norm_w is not None
    tm, tf = min(FFN_UP_NORM_TM if fused else FFN_UP_TM, t), FFN_TF
    nj, ni = pl.cdiv(f, tf), t // tm
    rows = next(r for r in range(BF16_ROWS, f + 1, BF16_ROWS) if f % r == 0 and f // r <= nj * ni)
    n_side = f // rows
    side = pl.BlockSpec((rows, d), lambda j, i: (jnp.minimum(j * ni + i, n_side - 1), 0))
    weights = [pl.BlockSpec((d, tf), lambda j, i: (0, j)), pl.BlockSpec((d, tf), lambda j, i: (0, j)), side]
    rows_spec = pl.BlockSpec((tm, d), lambda j, i: (i, 0))
    common = dict(
        grid=(nj, ni),
        out_specs=[pl.BlockSpec((tm, tf), lambda j, i: (i, j)), side],
        out_shape=[jax.ShapeDtypeStruct((t, f), BF16), jax.ShapeDtypeStruct((f, d), BF16)],
        compiler_params=pltpu.CompilerParams(
            dimension_semantics=("arbitrary", "arbitrary"), vmem_limit_bytes=VMEM_LIMIT),
    )
    if fused:
        return pl.pallas_call(
            functools.partial(_ffn_up_norm_body, n_side=n_side),
            in_specs=[rows_spec, pl.BlockSpec((1, d), lambda j, i: (0, 0))] + weights,
            scratch_shapes=[pltpu.VMEM((d, 2 * tf), BF16), pltpu.VMEM((tm, d), BF16)],
            name="ffn_up_norm", **common,
        )(h, norm_w.reshape(1, d), wg, wu, wd)
    return pl.pallas_call(
        functools.partial(_ffn_up_body, n_side=n_side),
        in_specs=[rows_spec] + weights,
        scratch_shapes=[pltpu.VMEM((d, 2 * tf), BF16)],
        name="ffn_up", **common,
    )(h, wg, wu, wd)


def _ffn_down_body(a_ref, wd_ref, x_ref, o_ref):
    o_ref[...] = x_ref[...] + 0.5 * _dot(a_ref[...], wd_ref[...])


def _ffn_down_norm_body(a_ref, wd_ref, x_ref, nw_ref, ws_ref, o_ref, hn_ref, side_ref):
    rows = a_ref.shape[0]
    group = min(ROW_GROUP, rows)
    for r0 in range(0, rows, group):
        rs = slice(r0, r0 + group)
        y = x_ref[rs, :] + 0.5 * _dot(a_ref[rs, :], wd_ref[...])
        o_ref[rs, :] = y
        hn = _rms(y, nw_ref[...]).astype(BF16)
        hn_ref[rs, :] = hn
        side_ref[rs, :] = _dot(hn, ws_ref[...])


def _ffn_down(act, wd, x, next_norm_w=None, w_side=None):
    t, d = x.shape
    f = act.shape[1]
    tm = min(FFN_DOWN_TM if next_norm_w is None else FFN_DOWN_NORM_TM, t)
    row = pl.BlockSpec((tm, d), lambda i: (i, 0))
    in_specs = [pl.BlockSpec((tm, f), lambda i: (i, 0)),
                pl.BlockSpec((f, d), lambda i: (0, 0), pipeline_mode=pl.Buffered(1)),
                row]
    params = pltpu.CompilerParams(dimension_semantics=("parallel",), vmem_limit_bytes=VMEM_LIMIT)
    if next_norm_w is None:
        assert w_side is None
        return pl.pallas_call(
            _ffn_down_body, grid=(t // tm,), in_specs=in_specs, out_specs=row,
            out_shape=jax.ShapeDtypeStruct((t, d), F32), compiler_params=params, name="ffn_down",
        )(act, wd, x)
    return pl.pallas_call(
        _ffn_down_norm_body, grid=(t // tm,),
        in_specs=in_specs + [pl.BlockSpec((1, d), lambda i: (0, 0)), pl.BlockSpec((d, LANES), lambda i: (0, 0))],
        out_specs=[row, row, pl.BlockSpec((tm, LANES), lambda i: (i, 0))],
        out_shape=[jax.ShapeDtypeStruct((t, d), F32), jax.ShapeDtypeStruct((t, d), BF16),
                   jax.ShapeDtypeStruct((t, LANES), F32)],
        compiler_params=pltpu.CompilerParams(dimension_semantics=("parallel",),
                                             vmem_limit_bytes=FFN_DOWN_NORM_VMEM),
        name="ffn_down_norm",
    )(act, wd, x, next_norm_w.reshape(1, d), w_side)


def _proj_body(h_ref, wf_ref, wx_ref, p_ref, w_scr, *, n_aligned, shift):
    j, i = pl.program_id(0), pl.program_id(1)
    tn = wf_ref.shape[0]

    @pl.when(jnp.logical_and(i == 0, j < n_aligned))
    def _():
        w_scr[...] = wf_ref[...].astype(BF16)

    @pl.when(jnp.logical_and(i == 0, j >= n_aligned))
    def _():
        w_scr[:tn - shift, :] = wf_ref[shift:, :].astype(BF16)
        w_scr[tn - shift:, :] = wx_ref[:shift, :].astype(BF16)

    half = tn // 2
    for n0 in (0, half):
        r = _dot_nt(h_ref[...], w_scr[n0:n0 + half, :])
        for c in range(half // LANES):
            p_ref[n0 // LANES + c] = r[:, c * LANES:(c + 1) * LANES].astype(BF16)


def _proj(h, wt, n_aligned, shift, n_tiles):
    t, d = h.shape
    tm, tn = min(PROJ_TM, t), PROJ_TN
    assert 0 < shift < LANES and shift % BF16_ROWS == 0 and wt.shape[0] == n_tiles * tn + shift
    spb = tn // LANES
    return pl.pallas_call(
        functools.partial(_proj_body, n_aligned=n_aligned, shift=shift),
        grid=(n_tiles, t // tm),
        in_specs=[
            pl.BlockSpec((tm, d), lambda j, i: (i, 0)),
            pl.BlockSpec((tn, d), lambda j, i: (j, 0)),
            pl.BlockSpec((LANES, d), lambda j, i: ((j + 1) * spb, 0)),
        ],
        out_specs=pl.BlockSpec((spb, tm, LANES), lambda j, i: (j, i, 0)),
        out_shape=jax.ShapeDtypeStruct((n_tiles * spb, t, LANES), BF16),
        scratch_shapes=[pltpu.VMEM((tn, d), BF16)],
        compiler_params=pltpu.CompilerParams(
            dimension_semantics=("arbitrary", "arbitrary"), vmem_limit_bytes=VMEM_LIMIT),
        name="proj",
    )(h, wt, wt)


def _merge_body(og_ref, oa_ref, gg0_ref, gg1_ref, ga0_ref, ga1_ref, x_ref, wa_ref, wb_ref, wo_ref, nw_ref,
                o_ref, hn_ref):
    def slabs(*refs):
        return jnp.concatenate([ref[c] for ref in refs for c in range(ref.shape[0])], axis=-1)

    ya = _dot(slabs(og_ref), wa_ref[...])
    yb = _dot(slabs(oa_ref), wb_ref[...])
    m = (jax.nn.sigmoid(slabs(gg0_ref, gg1_ref).astype(F32)) * ya
         + jax.nn.sigmoid(slabs(ga0_ref, ga1_ref).astype(F32)) * yb)
    y = x_ref[...] + _dot(m.astype(BF16), wo_ref[...])
    o_ref[...] = y
    hn_ref[...] = _rms(y, nw_ref[...]).astype(BF16)


def _merge(og3, oa3, p3, gate_blk, x, wa, wb, wo, next_norm_w):
    t, d = x.shape
    tm = min(MERGE_TM, t)
    assert d == 2 * HEADS * LANES
    const = lambda i: (0, 0)
    row = pl.BlockSpec((tm, d), lambda i: (i, 0))
    slab = lambda blk: pl.BlockSpec((HEADS, tm, LANES), lambda i: (blk, i, 0))
    return pl.pallas_call(
        _merge_body,
        grid=(t // tm,),
        in_specs=[
            slab(0), slab(0),
            slab(gate_blk), slab(gate_blk + 1), slab(gate_blk + 2), slab(gate_blk + 3),
            row,
            pl.BlockSpec(wa.shape, const, pipeline_mode=pl.Buffered(1)),
            pl.BlockSpec(wb.shape, const, pipeline_mode=pl.Buffered(1)),
            pl.BlockSpec(wo.shape, const, pipeline_mode=pl.Buffered(1)),
            pl.BlockSpec((1, d), const),
        ],
        out_specs=[row, row],
        out_shape=[jax.ShapeDtypeStruct((t, d), F32), jax.ShapeDtypeStruct((t, d), BF16)],
        compiler_params=pltpu.CompilerParams(
            dimension_semantics=("parallel",), vmem_limit_bytes=VMEM_LIMIT),
        name="merge",
    )(og3, oa3, p3, p3, p3, p3, x, wa, wb, wo, next_norm_w.reshape(1, d))


def _att_body(q_ref, k_ref, v_ref, qw_ref, kw_ref, bvec_ref, w0_ref, w1_ref, w2_ref,
              o_ref, w0b_ref, w1b_ref, w2b_ref, kbuf, vbuf, bias_scr):
    blk = pl.program_id(0)
    rb = q_ref.shape[1]

    w0b_ref[...] = w0_ref[...].astype(BF16)
    w1b_ref[...] = w1_ref[...].astype(BF16)
    w2b_ref[...] = w2_ref[...].astype(BF16)

    @pl.when(blk == 0)
    def _():
        kbuf[:, :rb, :] = jnp.zeros((HEADS, rb, LANES), BF16)
        vbuf[:, :rb, :] = jnp.zeros((HEADS, rb, LANES), BF16)
        qc = lax.broadcasted_iota(jnp.int32, (ATT_QG, ATT_WIN), 0) // CHUNK
        kc = lax.broadcasted_iota(jnp.int32, (ATT_QG, ATT_WIN), 1) // CHUNK
        off = kc - qc
        band = jnp.where(off >= 0, jnp.where(off <= ATT_LEFT_CHUNKS, 0.0, NEG), NEG).astype(F32)
        for h in range(HEADS):
            tbl = jnp.broadcast_to(bvec_ref[h], (ATT_QG, ATT_TBL))
            tbl = pltpu.roll(tbl, 0, 1, stride=1, stride_axis=0)
            bias_scr[h] = tbl[:, :ATT_WIN] * LOG2E + band

    first_neg = jnp.where(blk == 0, NEG, 0.0).astype(F32)
    col = lax.broadcasted_iota(jnp.int32, (1, ATT_WIN), 1)
    qw = qw_ref[...] * (HEAD_DIM ** -0.5 * LOG2E)
    kw = kw_ref[...]
    groups = [(dh, g * ATT_QG) for dh in range(ATT_HEADS_PER_ITER) for g in range(rb // ATT_QG)]

    def heads(hi, carry):
        h0 = hi * ATT_HEADS_PER_ITER
        for dh in range(ATT_HEADS_PER_ITER):
            kbuf[h0 + dh, rb:, :] = _rms(k_ref[h0 + dh].astype(F32), kw).astype(BF16)
            vbuf[h0 + dh, rb:, :] = v_ref[h0 + dh]
        qn = [_rms(q_ref[h0 + dh, r0:r0 + ATT_QG, :].astype(F32), qw).astype(BF16) for dh, r0 in groups]
        s = [_dot_nt(qn[n], kbuf[h0 + dh, r0:r0 + ATT_WIN, :]) for n, (dh, r0) in enumerate(groups)]
        s = [s[n] + bias_scr[h0 + dh] + jnp.where(col < rb - r0, first_neg, 0.0)
             for n, (dh, r0) in enumerate(groups)]
        p = [jnp.exp2(sn - jnp.max(sn, axis=-1, keepdims=True)) for sn in s]
        l = [jnp.sum(pn, axis=-1, keepdims=True) for pn in p]
        o = [_dot(p[n].astype(BF16), vbuf[h0 + dh, r0:r0 + ATT_WIN, :]) for n, (dh, r0) in enumerate(groups)]
        for n, (dh, r0) in enumerate(groups):
            o_ref[h0 + dh, r0:r0 + ATT_QG, :] = (o[n] / l[n]).astype(BF16)
        return carry

    lax.fori_loop(0, HEADS // ATT_HEADS_PER_ITER, heads, 0)
    kbuf[:, :rb, :] = kbuf[:, rb:, :]
    vbuf[:, :rb, :] = vbuf[:, rb:, :]


def _att_bias_table(rel_bias):
    tbl = rel_bias.astype(F32)
    n_tbl = tbl.shape[1]
    far = ATT_LEFT_CHUNKS * CHUNK - REL_MAX
    rep = lambda c, n: jnp.broadcast_to(tbl[:, c:c + 1], (tbl.shape[0], n))
    vec = jnp.concatenate([rep(n_tbl - 1, far), tbl[:, ::-1],
                           rep(0, ATT_TBL - far - n_tbl - (ATT_QG - 1)), rep(n_tbl - 1, ATT_QG - 1)], axis=1)
    return vec[:, None, :]


def _att(p3, q_norm_w, k_norm_w, rel_bias, q_blk, k_blk, v_blk, side_weights):
    t = p3.shape[1]
    rb = SEQ_BLOCK
    assert rb == ATT_LEFT_CHUNKS * CHUNK and t % rb == 0
    nblk = t // rb
    bvec = _att_bias_table(rel_bias)
    assert all(w.shape[0] % (BF16_ROWS * nblk) == 0 for w in side_weights)
    side_specs = [pl.BlockSpec((w.shape[0] // nblk, w.shape[1]), lambda b: (b, 0)) for w in side_weights]
    outs = pl.pallas_call(
        _att_body,
        grid=(nblk,),
        in_specs=[
            pl.BlockSpec((HEADS, rb, LANES), lambda b: (q_blk, b, 0)),
            pl.BlockSpec((HEADS, rb, LANES), lambda b: (k_blk, b, 0)),
            pl.BlockSpec((HEADS, rb, LANES), lambda b: (v_blk, b, 0)),
            pl.BlockSpec((1, LANES), lambda b: (0, 0)),
            pl.BlockSpec((1, LANES), lambda b: (0, 0)),
            pl.BlockSpec(bvec.shape, lambda b: (0, 0, 0)),
        ] + side_specs,
        out_specs=[pl.BlockSpec((HEADS, rb, LANES), lambda b: (0, b, 0))] + side_specs,
        out_shape=[jax.ShapeDtypeStruct((HEADS, t, LANES), BF16)]
                  + [jax.ShapeDtypeStruct(w.shape, BF16) for w in side_weights],
        scratch_shapes=[pltpu.VMEM((HEADS, 2 * rb, LANES), BF16),
                        pltpu.VMEM((HEADS, 2 * rb, LANES), BF16),
                        pltpu.VMEM((HEADS, ATT_QG, ATT_WIN), F32)],
        compiler_params=pltpu.CompilerParams(
            dimension_semantics=("arbitrary",), vmem_limit_bytes=VMEM_LIMIT),
        name="att",
    )(p3, p3, p3, q_norm_w.reshape(1, LANES), k_norm_w.reshape(1, LANES), bvec, *side_weights)
    return outs[0], outs[1:]


def _split2(x):
    hi = x.astype(BF16)
    return hi, (x - hi.astype(F32)).astype(BF16)


def _gdn_body(q_ref, k_ref, v_ref, z_ref, ab_ref, cw_ref, alog_ref, dtb_ref, onw_ref, ls_ref, mx_ref,
              o_ref,
              s_scr, xp_scr, qn_scr, kn_scr, vn_scr, gb_scr, bb_scr, u_scr, wq_scr, aqk_scr, kw_scr,
              bm_scr, cd_scr):
    blk = pl.program_id(0)
    rows = q_ref.shape[1]
    npair = rows // PAIR
    hs = range(HEADS)
    hd = tuple(hs)

    @pl.when(blk == 0)
    def _():
        s_scr[...] = jnp.zeros_like(s_scr)
        xp_scr[:, 0:SUBLANES, :] = jnp.zeros((3 * HEADS, SUBLANES, LANES), F32)

    ls = ls_ref[...]
    mx = mx_ref[...]
    ri = lax.broadcasted_iota(jnp.int32, (PAIR, PAIR), 0)
    ci = lax.broadcasted_iota(jnp.int32, (PAIR, PAIR), 1)
    same = (ri // CHUNK) == (ci // CHUNK)
    lower_f = jnp.where(same, jnp.where(ri >= ci, 1.0, 0.0), 0.0).astype(F32)
    strict_f = jnp.where(same, jnp.where(ri > ci, 1.0, 0.0), 0.0).astype(F32)
    eye = jnp.where(ri == ci, 1.0, 0.0).astype(F32)
    onw = onw_ref[...]
    alog, dtb = alog_ref[...], dtb_ref[...]

    def stage_a(p):
        r0 = p * PAIR
        sl = slice(r0, r0 + PAIR)
        ab = ab_ref[sl, :]
        xa = ab + dtb
        softplus = jnp.maximum(xa, 0.0) + jnp.log(1.0 + jnp.exp(-jnp.abs(xa)))
        g_all = -jnp.exp(alog) * softplus
        beta_all = jax.nn.sigmoid(ab)
        for h in hs:
            gb_scr[hd[h], sl, :] = jnp.broadcast_to(g_all[:, h:h + 1], (PAIR, LANES))
            bb_scr[hd[h], sl, :] = jnp.broadcast_to(beta_all[:, HEADS + h:HEADS + h + 1], (PAIR, LANES))
            outs = []
            for part, ref in enumerate((q_ref, k_ref, v_ref)):
                s = part * HEADS + h
                xp_scr[s, SUBLANES + r0:SUBLANES + r0 + PAIR, :] = ref[h, sl, :].astype(F32)
                w = cw_ref[s]
                y = w[0:1] * xp_scr[s, CONV_LEAD + r0:CONV_LEAD + r0 + PAIR, :]
                for tap in range(1, GDN_CONV):
                    y = y + w[tap:tap + 1] * xp_scr[s, CONV_LEAD + tap + r0:CONV_LEAD + tap + r0 + PAIR, :]
                outs.append(y * jax.nn.sigmoid(y))
            qc, kc, vc = outs
            qn_scr[hd[h], sl, :] = (qc * lax.rsqrt(jnp.sum(qc * qc, axis=-1, keepdims=True) + NORM_EPS)
                                * (HEAD_DIM ** -0.5))
            kn_scr[hd[h], sl, :] = kc * lax.rsqrt(jnp.sum(kc * kc, axis=-1, keepdims=True) + NORM_EPS)
            vn_scr[hd[h], sl, :] = vc

    def stage_b(p, hd):
        hs = range(len(hd))
        sl = slice(p * PAIR, (p + 1) * PAIR)
        k = [kn_scr[hd[h], sl, :] for h in hs]
        q = [qn_scr[hd[h], sl, :] for h in hs]
        v = [vn_scr[hd[h], sl, :] for h in hs]
        gb = [gb_scr[hd[h], sl, :] for h in hs]
        bb = [bb_scr[hd[h], sl, :] for h in hs]
        parts = [_split2(jnp.concatenate([gb[h], gb[h]], axis=1) * mx) for h in hs]
        gm = [_dot(ls, pt[0]) + _dot(ls, pt[1]) for pt in parts]
        yield
        decay = [jnp.exp(g[:, :PAIR]) for g in gm]
        gi = [g[:, PAIR:] for g in gm]
        glast = [jnp.concatenate([jnp.broadcast_to(g[c * CHUNK - 1:c * CHUNK, :], (CHUNK, LANES))
                                  for c in (1, 2)], axis=0) for g in gi]
        eg = [jnp.exp(g) for g in gi]
        er = [jnp.exp(glast[h] - gi[h]) for h in hs]
        kb = [k[h] * bb[h] for h in hs]
        kk = [_dot_nt(jnp.concatenate([kb[h], q[h]], axis=0).astype(BF16), k[h].astype(BF16)) for h in hs]
        yield
        n = [-(kk[h][:PAIR] * decay[h] * strict_f) for h in hs]
        aqk = [kk[h][PAIR:] * decay[h] * lower_f for h in hs]
        inv = [eye + n[h] for h in hs]
        nb = [n[h].astype(BF16) for h in hs]
        nb = [_dot(nb[h], nb[h]).astype(BF16) for h in hs]
        yield
        for it in range(5):
            if it < 4:
                m = [_dot(jnp.concatenate([inv[h].astype(BF16), nb[h]], axis=0), nb[h]) for h in hs]
                inv = [inv[h] + m[h][:PAIR] for h in hs]
                nb = [m[h][PAIR:].astype(BF16) for h in hs]
            else:
                inv = [inv[h] + _dot(inv[h].astype(BF16), nb[h]) for h in hs]
            yield
        uw = [_dot(inv[h].astype(BF16),
                   jnp.concatenate([v[h] * bb[h], kb[h] * eg[h]], axis=1).astype(BF16)) for h in hs]
        yield
        kd = [(k[h] * er[h]).astype(BF16) for h in hs]
        qg = [q[h] * eg[h] for h in hs]
        for cc in range(2):
            c = 2 * p + cc
            rs = slice(cc * CHUNK, (cc + 1) * CHUNK)
            kwb = [_dot_tn(kd[h][rs], jnp.concatenate([uw[h][rs, LANES:], uw[h][rs, :LANES]], axis=1).astype(BF16))
                   for h in hs]
            for h in hs:
                wq_scr[hd[h], c] = jnp.concatenate([uw[h][rs, LANES:], qg[h][rs]], axis=0).astype(BF16)
                kw_scr[hd[h], c] = kwb[h][:, :LANES].astype(BF16)
                bm_scr[hd[h], c] = kwb[h][:, LANES:]
                cd_scr[hd[h], c] = jnp.broadcast_to(eg[h][(cc + 1) * CHUNK - 1:(cc + 1) * CHUNK, :], (SUBLANES, LANES))
            yield
        for h in hs:
            u_scr[hd[h], sl, :] = uw[h][:, :LANES]
            aqk_scr[hd[h], p] = aqk[h].astype(BF16)

    def stage_c(p, hd):
        hs = range(len(hd))
        sl = slice(p * PAIR, (p + 1) * PAIR)
        c0, c1 = 2 * p, 2 * p + 1
        s0 = [s_scr[hd[h]] for h in hs]
        sb0 = [s0[h].astype(BF16) for h in hs]
        s1 = [s0[h] * cd_scr[hd[h], c0][0:1, :] + (bm_scr[hd[h], c0] - _dot(kw_scr[hd[h], c0], sb0[h])) for h in hs]
        yield
        sb1 = [s1[h].astype(BF16) for h in hs]
        s2 = [s1[h] * cd_scr[hd[h], c1][0:1, :] + (bm_scr[hd[h], c1] - _dot(kw_scr[hd[h], c1], sb1[h])) for h in hs]
        for h in hs:
            s_scr[hd[h]] = s2[h]
        yield
        a0 = [_dot(wq_scr[hd[h], c0], sb0[h]) for h in hs]
        a1 = [_dot(wq_scr[hd[h], c1], sb1[h]) for h in hs]
        yield
        vnew = [(u_scr[hd[h], sl, :] - jnp.concatenate([a0[h][:CHUNK], a1[h][:CHUNK]], axis=0)).astype(BF16)
                for h in hs]
        o = [jnp.concatenate([a0[h][CHUNK:], a1[h][CHUNK:]], axis=0) + _dot(aqk_scr[hd[h], p], vnew[h]) for h in hs]
        yield
        for h in hs:
            z = z_ref[hd[h], sl, :].astype(F32)
            o_ref[hd[h], sl, :] = (_rms(o[h], onw) * (z * jax.nn.sigmoid(z))).astype(BF16)

    def interleave(*gens):
        live = [g for g in gens if g is not None]
        while live:
            for g in list(live):
                try:
                    next(g)
                except StopIteration:
                    live.remove(g)

    stage_a(0)
    for step in range(npair + 1):
        if step + 1 < npair:
            stage_a(step + 1)
        for g0 in range(0, HEADS, GDN_HEAD_GROUP):
            grp = tuple(range(g0, g0 + GDN_HEAD_GROUP))
            interleave(stage_c(step - 1, grp) if step >= 1 else None, stage_b(step, grp) if step < npair else None)

    for s in range(3 * HEADS):
        xp_scr[s, 0:SUBLANES, :] = xp_scr[s, rows:rows + SUBLANES, :]


def _gdn(p3, ab, conv_w, a_log, dt_bias, out_norm_w, q_blk):
    t = p3.shape[1]
    rows = min(SEQ_BLOCK, t)
    nchunk, npair = rows // CHUNK, rows // PAIR
    cw = conv_w.astype(F32).reshape(GDN_CONV, 3 * HEADS, LANES).transpose(1, 0, 2)
    pad = lambda v: jnp.zeros((1, LANES), F32).at[0, :HEADS].set(v.astype(F32))
    ti = jnp.arange(PAIR)
    same = (ti[:, None] // CHUNK) == (ti[None, :] // CHUNK)
    le = same & (ti[None, :] <= ti[:, None])
    gt = same & (ti[None, :] > ti[:, None])
    ls = le.astype(BF16)
    mx = jnp.concatenate([gt.T.astype(F32), jnp.ones((PAIR, LANES), F32)], axis=1)
    slab = lambda off: pl.BlockSpec((HEADS, rows, LANES), lambda b: (q_blk + off, b, 0))
    const2 = lambda b: (0, 0)
    hr = (HEADS, rows, LANES)
    return pl.pallas_call(
        _gdn_body,
        grid=(t // rows,),
        in_specs=[
            slab(0), slab(1), slab(2), slab(3),
            pl.BlockSpec((rows, LANES), lambda b: (b, 0)),
            pl.BlockSpec(cw.shape, lambda b: (0, 0, 0)),
            pl.BlockSpec((1, LANES), const2),
            pl.BlockSpec((1, LANES), const2),
            pl.BlockSpec((1, LANES), const2),
            pl.BlockSpec(ls.shape, const2),
            pl.BlockSpec(mx.shape, const2),
        ],
        out_specs=pl.BlockSpec(hr, lambda b: (0, b, 0)),
        out_shape=jax.ShapeDtypeStruct((HEADS, t, LANES), BF16),
        scratch_shapes=[
            pltpu.VMEM((HEADS, HEAD_DIM, HEAD_DIM), F32),
            pltpu.VMEM((3 * HEADS, rows + SUBLANES, LANES), F32),
            pltpu.VMEM(hr, F32), pltpu.VMEM(hr, F32), pltpu.VMEM(hr, F32),
            pltpu.VMEM(hr, F32), pltpu.VMEM(hr, F32),
            pltpu.VMEM(hr, F32),
            pltpu.VMEM((HEADS, nchunk, 2 * CHUNK, LANES), BF16),
            pltpu.VMEM((HEADS, npair, PAIR, PAIR), BF16),
            pltpu.VMEM((HEADS, nchunk, HEAD_DIM, HEAD_DIM), BF16),
            pltpu.VMEM((HEADS, nchunk, HEAD_DIM, HEAD_DIM), F32),
            pltpu.VMEM((HEADS, nchunk, SUBLANES, LANES), F32),
        ],
        compiler_params=pltpu.CompilerParams(
            dimension_semantics=("arbitrary",), vmem_limit_bytes=VMEM_LIMIT),
        name="gdn",
    )(p3, p3, p3, p3, ab, cw, pad(a_log), pad(dt_bias), out_norm_w.astype(F32).reshape(1, LANES), ls, mx)


def _pad_to(w, axis, mult):
    n = w.shape[axis]
    extra = (-n) % mult
    if extra == 0:
        return w
    widths = [(0, 0)] * w.ndim
    widths[axis] = (0, extra)
    return jnp.pad(w, widths)


@jax.jit
def _forward(x, ffn1_norm, ffn1_w_gate, ffn1_w_up, ffn1_w_down, mix_norm, w_in, gdn_conv,
             gdn_A_log, gdn_dt_bias, gdn_out_norm, att_q_norm, att_k_norm, att_rel_bias,
             w_branch_gdn, w_branch_att, w_out, ffn2_norm, ffn2_w_gate, ffn2_w_up, ffn2_w_down):
    b, t, d = x.shape
    gw = HEADS * HEAD_DIM
    outs = []
    for bi in range(b):
        xb = x[bi]
        depth = ffn1_norm.shape[0]
        for l in range(depth):
            wi = w_in[l]
            o_ab = 4 * gw
            o_att = o_ab + 2 * HEADS
            w_ab = _pad_to(wi[:, o_ab:o_att].astype(BF16), 1, LANES)

            act, wd = _ffn_up(xb, ffn1_w_gate[l], ffn1_w_up[l], ffn1_w_down[l], ffn1_norm[l])
            xb, hn, ab = _ffn_down(act, wd, xb, mix_norm[l], w_ab)
            p3 = _proj(hn, wi.T, o_ab // PROJ_TN, o_att - o_ab, (wi.shape[1] - (o_att - o_ab)) // PROJ_TN)
            og3 = _gdn(p3, ab, gdn_conv[l], gdn_A_log[l], gdn_dt_bias[l], gdn_out_norm[l], 0)
            oa3, (wa, wb, wo) = _att(p3, att_q_norm[l], att_k_norm[l], att_rel_bias[l], 4, 5, 6,
                                     (w_branch_gdn[l], w_branch_att[l], w_out[l]))
            xb, hn = _merge(og3, oa3, p3, 7, xb, wa, wb, wo, ffn2_norm[l])

            act, wd = _ffn_up(hn, ffn2_w_gate[l], ffn2_w_up[l], ffn2_w_down[l])
            xb = _ffn_down(act, wd, xb)
        outs.append(xb)
    return jnp.stack(outs, axis=0)


def kernel(x, ffn1_norm, ffn1_w_gate, ffn1_w_up, ffn1_w_down, mix_norm, w_in, gdn_conv, gdn_A_log, gdn_dt_bias, gdn_out_norm, att_q_norm, att_k_norm, att_rel_bias, w_branch_gdn, w_branch_att, w_out, ffn2_norm, ffn2_w_gate, ffn2_w_up, ffn2_w_down):
    return _forward(x, ffn1_norm, ffn1_w_gate, ffn1_w_up, ffn1_w_down, mix_norm, w_in, gdn_conv,
                    gdn_A_log, gdn_dt_bias, gdn_out_norm, att_q_norm, att_k_norm, att_rel_bias,
                    w_branch_gdn, w_branch_att, w_out, ffn2_norm, ffn2_w_gate, ffn2_w_up, ffn2_w_down)
```

```python
import functools
import math

import jax
import jax.numpy as jnp
from jax import lax
from jax.experimental import pallas as pl
from jax.experimental.pallas import tpu as pltpu

F32 = jnp.float32
BF16 = jnp.bfloat16

NORM_EPS = 1e-6
CHUNK = 64
HEADS = 8
HEAD_DIM = 128
LANES = 128
SUBLANES = 8
BF16_ROWS = 16
GDN_CONV = 4
CONV_LEAD = SUBLANES - (GDN_CONV - 1)
ATT_LEFT_CHUNKS = 8
REL_MAX = 256
NEG = -1e30

MIB = 1024 * 1024
VMEM_V7X = 64 * MIB
VMEM_LIMIT = VMEM_V7X - 8 * MIB

LOG2E = math.log2(math.e)

NORM_TM = 512
FFN_UP_TM = 2048
FFN_TF = 512
FFN_COL_GROUP = 256
FFN_DOWN_TM = 512
FFN_DOWN_NORM_TM = 512
FFN_DOWN_COL_GROUP = 512
FFN_DOWN_NORM_VMEM = VMEM_V7X - 2 * MIB
ROW_GROUP = 256
PROJ_TM = 2048
PROJ_TN = 1024
ATT_HEADS_PER_ITER = 2
MERGE_TM = 512
SEQ_BLOCK = 512
ATT_QG = 128
ATT_WIN = ATT_QG + ATT_LEFT_CHUNKS * CHUNK
ATT_TBL = 1024
PAIR = 2 * CHUNK
GDN_HEAD_GROUP = 8


def _rms(x, w):
    return x * lax.rsqrt(jnp.mean(x * x, axis=-1, keepdims=True) + NORM_EPS) * w


def _dot(a, b):
    return jnp.dot(a, b, preferred_element_type=F32)


def _dot_nt(a, b):
    return lax.dot_general(a, b, (((1,), (1,)), ((), ())), preferred_element_type=F32)


def _dot_tn(a, b):
    return lax.dot_general(a, b, (((0,), (0,)), ((), ())), preferred_element_type=F32)


def _norm_body(x_ref, nw_ref, o_ref):
    o_ref[...] = _rms(x_ref[...], nw_ref[...]).astype(BF16)


def _norm(x, norm_w):
    t, d = x.shape
    tm = min(NORM_TM, t)
    return pl.pallas_call(
        _norm_body,
        grid=(t // tm,),
        in_specs=[pl.BlockSpec((tm, d), lambda i: (i, 0)), pl.BlockSpec((1, d), lambda i: (0, 0))],
        out_specs=pl.BlockSpec((tm, d), lambda i: (i, 0)),
        out_shape=jax.ShapeDtypeStruct((t, d), BF16),
        compiler_params=pltpu.CompilerParams(dimension_semantics=("parallel",), vmem_limit_bytes=VMEM_LIMIT),
        name="norm",
    )(x, norm_w.reshape(1, d))


def _ffn_up_body(h_ref, wg_ref, wu_ref, wd_ref, o_ref, wdb_ref, w_scr, *, n_side):
    tf = wg_ref.shape[1]

    @pl.when(pl.program_id(1) == 0)
    def _():
        w_scr[:, :tf] = wg_ref[...].astype(BF16)
        w_scr[:, tf:] = wu_ref[...].astype(BF16)

    @pl.when(pl.program_id(0) * pl.num_programs(1) + pl.program_id(1) < n_side)
    def _():
        wdb_ref[...] = wd_ref[...].astype(BF16)

    for n0 in range(0, tf, FFN_COL_GROUP):
        g = _dot(h_ref[...], w_scr[:, n0:n0 + FFN_COL_GROUP])
        u = _dot(h_ref[...], w_scr[:, tf + n0:tf + n0 + FFN_COL_GROUP])
        o_ref[:, n0:n0 + FFN_COL_GROUP] = (g * jax.nn.sigmoid(g) * u).astype(BF16)


def _ffn_up(h, wg, wu, wd):
    t, d = h.shape
    f = wg.shape[1]
    tm, tf = min(FFN_UP_TM, t), FFN_TF
    nj, ni = pl.cdiv(f, tf), t // tm
    rows = next(r for r in range(BF16_ROWS, f + 1, BF16_ROWS) if f % r == 0 and f // r <= nj * ni)
    n_side = f // rows
    side = pl.BlockSpec((rows, d), lambda j, i: (jnp.minimum(j * ni + i, n_side - 1), 0))
    return pl.pallas_call(
        functools.partial(_ffn_up_body, n_side=n_side),
        grid=(nj, ni),
        in_specs=[
            pl.BlockSpec((tm, d), lambda j, i: (i, 0)),
            pl.BlockSpec((d, tf), lambda j, i: (0, j)),
            pl.BlockSpec((d, tf), lambda j, i: (0, j)),
            side,
        ],
        out_specs=[pl.BlockSpec((tm, tf), lambda j, i: (i, j)), side],
        out_shape=[jax.ShapeDtypeStruct((t, f), BF16), jax.ShapeDtypeStruct((f, d), BF16)],
        scratch_shapes=[pltpu.VMEM((d, 2 * tf), BF16)],
        compiler_params=pltpu.CompilerParams(
            dimension_semantics=("arbitrary", "arbitrary"), vmem_limit_bytes=VMEM_LIMIT),
        name="ffn_up",
    )(h, wg, wu, wd)


def _ffn_down_body(a_ref, wd_ref, x_ref, o_ref):
    d = o_ref.shape[1]
    group = min(FFN_DOWN_COL_GROUP, d)
    for n0 in range(0, d, group):
        cs = slice(n0, n0 + group)
        o_ref[:, cs] = x_ref[:, cs] + 0.5 * _dot(a_ref[...], wd_ref[:, cs])


def _ffn_down_norm_body(a_ref, wd_ref, x_ref, nw_ref, ws_ref, o_ref, hn_ref, side_ref):
    rows = a_ref.shape[0]
    group = min(ROW_GROUP, rows)
    for r0 in range(0, rows, group):
        rs = slice(r0, r0 + group)
        d = o_ref.shape[1]
        cgroup = min(FFN_DOWN_COL_GROUP, d)
        for n0 in range(0, d, cgroup):
            cs = slice(n0, n0 + cgroup)
            o_ref[rs, cs] = x_ref[rs, cs] + 0.5 * _dot(a_ref[rs, :], wd_ref[:, cs])
        y = o_ref[rs, :]
        hn = _rms(y, nw_ref[...]).astype(BF16)
        hn_ref[rs, :] = hn
        side_ref[rs, :] = _dot(hn, ws_ref[...])


def _ffn_down(act, wd, x, next_norm_w=None, w_side=None):
    t, d = x.shape
    f = act.shape[1]
    tm = min(FFN_DOWN_TM if next_norm_w is None else FFN_DOWN_NORM_TM, t)
    row = pl.BlockSpec((tm, d), lambda i: (i, 0))
    in_specs = [pl.BlockSpec((tm, f), lambda i: (i, 0)),
                pl.BlockSpec((f, d), lambda i: (0, 0), pipeline_mode=pl.Buffered(1)),
                row]
    params = pltpu.CompilerParams(dimension_semantics=("parallel",), vmem_limit_bytes=VMEM_LIMIT)
    if next_norm_w is None:
        assert w_side is None
        return pl.pallas_call(
            _ffn_down_body, grid=(t // tm,), in_specs=in_specs, out_specs=row,
            out_shape=jax.ShapeDtypeStruct((t, d), F32), compiler_params=params, name="ffn_down",
        )(act, wd, x)
    return pl.pallas_call(
        _ffn_down_norm_body, grid=(t // tm,),
        in_specs=in_specs + [pl.BlockSpec((1, d), lambda i: (0, 0)), pl.BlockSpec((d, LANES), lambda i: (0, 0))],
        out_specs=[row, row, pl.BlockSpec((tm, LANES), lambda i: (i, 0))],
        out_shape=[jax.ShapeDtypeStruct((t, d), F32), jax.ShapeDtypeStruct((t, d), BF16),
                   jax.ShapeDtypeStruct((t, LANES), F32)],
        compiler_params=pltpu.CompilerParams(dimension_semantics=("parallel",),
                                             vmem_limit_bytes=FFN_DOWN_NORM_VMEM),
        name="ffn_down_norm",
    )(act, wd, x, next_norm_w.reshape(1, d), w_side)


def _proj_body(h_ref, wf_ref, wx_ref, p_ref, w_scr, *, n_aligned, shift):
    j, i = pl.program_id(0), pl.program_id(1)
    tn = wf_ref.shape[0]

    @pl.when(jnp.logical_and(i == 0, j < n_aligned))
    def _():
        w_scr[...] = wf_ref[...].astype(BF16)

    @pl.when(jnp.logical_and(i == 0, j >= n_aligned))
    def _():
        w_scr[:tn - shift, :] = wf_ref[shift:, :].astype(BF16)
        w_scr[tn - shift:, :] = wx_ref[:shift, :].astype(BF16)

    half = tn // 2
    for n0 in (0, half):
        r = _dot_nt(h_ref[...], w_scr[n0:n0 + half, :])
        for c in range(half // LANES):
            p_ref[n0 // LANES + c] = r[:, c * LANES:(c + 1) * LANES].astype(BF16)


def _proj(h, wt, n_aligned, shift, n_tiles):
    t, d = h.shape
    tm, tn = min(PROJ_TM, t), PROJ_TN
    assert 0 < shift < LANES and shift % BF16_ROWS == 0 and wt.shape[0] == n_tiles * tn + shift
    spb = tn // LANES
    return pl.pallas_call(
        functools.partial(_proj_body, n_aligned=n_aligned, shift=shift),
        grid=(n_tiles, t // tm),
        in_specs=[
            pl.BlockSpec((tm, d), lambda j, i: (i, 0)),
            pl.BlockSpec((tn, d), lambda j, i: (j, 0)),
            pl.BlockSpec((LANES, d), lambda j, i: ((j + 1) * spb, 0)),
        ],
        out_specs=pl.BlockSpec((spb, tm, LANES), lambda j, i: (j, i, 0)),
        out_shape=jax.ShapeDtypeStruct((n_tiles * spb, t, LANES), BF16),
        scratch_shapes=[pltpu.VMEM((tn, d), BF16)],
        compiler_params=pltpu.CompilerParams(
            dimension_semantics=("arbitrary", "arbitrary"), vmem_limit_bytes=VMEM_LIMIT),
        name="proj",
    )(h, wt, wt)


def _merge_body(og_ref, oa_ref, gg0_ref, gg1_ref, ga0_ref, ga1_ref, x_ref, wa_ref, wb_ref, wo_ref, nw_ref,
                o_ref, hn_ref):
    def slabs(*refs):
        return jnp.concatenate([ref[c] for ref in refs for c in range(ref.shape[0])], axis=-1)

    ya = _dot(slabs(og_ref), wa_ref[...])
    yb = _dot(slabs(oa_ref), wb_ref[...])
    m = (jax.nn.sigmoid(slabs(gg0_ref, gg1_ref).astype(F32)) * ya
         + jax.nn.sigmoid(slabs(ga0_ref, ga1_ref).astype(F32)) * yb)
    y = x_ref[...] + _dot(m.astype(BF16), wo_ref[...])
    o_ref[...] = y
    hn_ref[...] = _rms(y, nw_ref[...]).astype(BF16)


def _merge(og3, oa3, p3, gate_blk, x, wa, wb, wo, next_norm_w):
    t, d = x.shape
    tm = min(MERGE_TM, t)
    assert d == 2 * HEADS * LANES
    const = lambda i: (0, 0)
    row = pl.BlockSpec((tm, d), lambda i: (i, 0))
    slab = lambda blk: pl.BlockSpec((HEADS, tm, LANES), lambda i: (blk, i, 0))
    return pl.pallas_call(
        _merge_body,
        grid=(t // tm,),
        in_specs=[
            slab(0), slab(0),
            slab(gate_blk), slab(gate_blk + 1), slab(gate_blk + 2), slab(gate_blk + 3),
            row,
            pl.BlockSpec(wa.shape, const, pipeline_mode=pl.Buffered(1)),
            pl.BlockSpec(wb.shape, const, pipeline_mode=pl.Buffered(1)),
            pl.BlockSpec(wo.shape, const, pipeline_mode=pl.Buffered(1)),
            pl.BlockSpec((1, d), const),
        ],
        out_specs=[row, row],
        out_shape=[jax.ShapeDtypeStruct((t, d), F32), jax.ShapeDtypeStruct((t, d), BF16)],
        compiler_params=pltpu.CompilerParams(
            dimension_semantics=("parallel",), vmem_limit_bytes=VMEM_LIMIT),
        name="merge",
    )(og3, oa3, p3, p3, p3, p3, x, wa, wb, wo, next_norm_w.reshape(1, d))


def _att_body(q_ref, k_ref, v_ref, qw_ref, kw_ref, bvec_ref, w0_ref, w1_ref, w2_ref,
              o_ref, w0b_ref, w1b_ref, w2b_ref, kbuf, vbuf, bias_scr):
    blk = pl.program_id(0)
    rb = q_ref.shape[1]

    w0b_ref[...] = w0_ref[...].astype(BF16)
    w1b_ref[...] = w1_ref[...].astype(BF16)
    w2b_ref[...] = w2_ref[...].astype(BF16)

    @pl.when(blk == 0)
    def _():
        kbuf[:, :rb, :] = jnp.zeros((HEADS, rb, LANES), BF16)
        vbuf[:, :rb, :] = jnp.zeros((HEADS, rb, LANES), BF16)
        qc = lax.broadcasted_iota(jnp.int32, (ATT_QG, ATT_WIN), 0) // CHUNK
        kc = lax.broadcasted_iota(jnp.int32, (ATT_QG, ATT_WIN), 1) // CHUNK
        off = kc - qc
        band = jnp.where(off >= 0, jnp.where(off <= ATT_LEFT_CHUNKS, 0.0, NEG), NEG).astype(F32)
        for h in range(HEADS):
            tbl = jnp.broadcast_to(bvec_ref[h], (ATT_QG, ATT_TBL))
            tbl = pltpu.roll(tbl, 0, 1, stride=1, stride_axis=0)
            bias_scr[h] = tbl[:, :ATT_WIN] * LOG2E + band

    first_neg = jnp.where(blk == 0, NEG, 0.0).astype(F32)
    col = lax.broadcasted_iota(jnp.int32, (1, ATT_WIN), 1)
    qw = qw_ref[...] * (HEAD_DIM ** -0.5 * LOG2E)
    kw = kw_ref[...]
    groups = [(dh, g * ATT_QG) for dh in range(ATT_HEADS_PER_ITER) for g in range(rb // ATT_QG)]

    def heads(hi, carry):
        h0 = hi * ATT_HEADS_PER_ITER
        for dh in range(ATT_HEADS_PER_ITER):
            kbuf[h0 + dh, rb:, :] = _rms(k_ref[h0 + dh].astype(F32), kw).astype(BF16)
            vbuf[h0 + dh, rb:, :] = v_ref[h0 + dh]
        qn = [_rms(q_ref[h0 + dh, r0:r0 + ATT_QG, :].astype(F32), qw).astype(BF16) for dh, r0 in groups]
        s = [_dot_nt(qn[n], kbuf[h0 + dh, r0:r0 + ATT_WIN, :]) for n, (dh, r0) in enumerate(groups)]
        s = [s[n] + bias_scr[h0 + dh] + jnp.where(col < rb - r0, first_neg, 0.0)
             for n, (dh, r0) in enumerate(groups)]
        p = [jnp.exp2(sn - jnp.max(sn, axis=-1, keepdims=True)) for sn in s]
        l = [jnp.sum(pn, axis=-1, keepdims=True) for pn in p]
        o = [_dot(p[n].astype(BF16), vbuf[h0 + dh, r0:r0 + ATT_WIN, :]) for n, (dh, r0) in enumerate(groups)]
        for n, (dh, r0) in enumerate(groups):
            o_ref[h0 + dh, r0:r0 + ATT_QG, :] = (o[n] / l[n]).astype(BF16)
        return carry

    lax.fori_loop(0, HEADS // ATT_HEADS_PER_ITER, heads, 0)
    kbuf[:, :rb, :] = kbuf[:, rb:, :]
    vbuf[:, :rb, :] = vbuf[:, rb:, :]


def _att_bias_table(rel_bias):
    tbl = rel_bias.astype(F32)
    n_tbl = tbl.shape[1]
    far = ATT_LEFT_CHUNKS * CHUNK - REL_MAX
    rep = lambda c, n: jnp.broadcast_to(tbl[:, c:c + 1], (tbl.shape[0], n))
    vec = jnp.concatenate([rep(n_tbl - 1, far), tbl[:, ::-1],
                           rep(0, ATT_TBL - far - n_tbl - (ATT_QG - 1)), rep(n_tbl - 1, ATT_QG - 1)], axis=1)
    return vec[:, None, :]


def _att(p3, q_norm_w, k_norm_w, rel_bias, q_blk, k_blk, v_blk, side_weights):
    t = p3.shape[1]
    rb = SEQ_BLOCK
    assert rb == ATT_LEFT_CHUNKS * CHUNK and t % rb == 0
    nblk = t // rb
    bvec = _att_bias_table(rel_bias)
    assert all(w.shape[0] % (BF16_ROWS * nblk) == 0 for w in side_weights)
    side_specs = [pl.BlockSpec((w.shape[0] // nblk, w.shape[1]), lambda b: (b, 0)) for w in side_weights]
    outs = pl.pallas_call(
        _att_body,
        grid=(nblk,),
        in_specs=[
            pl.BlockSpec((HEADS, rb, LANES), lambda b: (q_blk, b, 0)),
            pl.BlockSpec((HEADS, rb, LANES), lambda b: (k_blk, b, 0)),
            pl.BlockSpec((HEADS, rb, LANES), lambda b: (v_blk, b, 0)),
            pl.BlockSpec((1, LANES), lambda b: (0, 0)),
            pl.BlockSpec((1, LANES), lambda b: (0, 0)),
            pl.BlockSpec(bvec.shape, lambda b: (0, 0, 0)),
        ] + side_specs,
        out_specs=[pl.BlockSpec((HEADS, rb, LANES), lambda b: (0, b, 0))] + side_specs,
        out_shape=[jax.ShapeDtypeStruct((HEADS, t, LANES), BF16)]
                  + [jax.ShapeDtypeStruct(w.shape, BF16) for w in side_weights],
        scratch_shapes=[pltpu.VMEM((HEADS, 2 * rb, LANES), BF16),
                        pltpu.VMEM((HEADS, 2 * rb, LANES), BF16),
                        pltpu.VMEM((HEADS, ATT_QG, ATT_WIN), F32)],
        compiler_params=pltpu.CompilerParams(
            dimension_semantics=("arbitrary",), vmem_limit_bytes=VMEM_LIMIT),
        name="att",
    )(p3, p3, p3, q_norm_w.reshape(1, LANES), k_norm_w.reshape(1, LANES), bvec, *side_weights)
    return outs[0], outs[1:]


def _split2(x):
    hi = x.astype(BF16)
    return hi, (x - hi.astype(F32)).astype(BF16)


def _gdn_body(q_ref, k_ref, v_ref, z_ref, ab_ref, cw_ref, alog_ref, dtb_ref, onw_ref, ls_ref, mx_ref,
              o_ref,
              s_scr, xp_scr, qn_scr, kn_scr, vn_scr, gb_scr, bb_scr, u_scr, wq_scr, aqk_scr, kw_scr,
              bm_scr, cd_scr):
    blk = pl.program_id(0)
    rows = q_ref.shape[1]
    npair = rows // PAIR
    hs = range(HEADS)
    hd = tuple(hs)

    @pl.when(blk == 0)
    def _():
        s_scr[...] = jnp.zeros_like(s_scr)
        xp_scr[:, 0:SUBLANES, :] = jnp.zeros((3 * HEADS, SUBLANES, LANES), F32)

    ls = ls_ref[...]
    mx = mx_ref[...]
    ri = lax.broadcasted_iota(jnp.int32, (PAIR, PAIR), 0)
    ci = lax.broadcasted_iota(jnp.int32, (PAIR, PAIR), 1)
    same = (ri // CHUNK) == (ci // CHUNK)
    lower_f = jnp.where(same, jnp.where(ri >= ci, 1.0, 0.0), 0.0).astype(F32)
    strict_f = jnp.where(same, jnp.where(ri > ci, 1.0, 0.0), 0.0).astype(F32)
    eye = jnp.where(ri == ci, 1.0, 0.0).astype(F32)
    onw = onw_ref[...]
    alog, dtb = alog_ref[...], dtb_ref[...]

    def stage_a(p):
        r0 = p * PAIR
        sl = slice(r0, r0 + PAIR)
        ab = ab_ref[sl, :]
        xa = ab + dtb
        softplus = jnp.maximum(xa, 0.0) + jnp.log(1.0 + jnp.exp(-jnp.abs(xa)))
        g_all = -jnp.exp(alog) * softplus
        beta_all = jax.nn.sigmoid(ab)
        for h in hs:
            gb_scr[hd[h], sl, :] = jnp.broadcast_to(g_all[:, h:h + 1], (PAIR, LANES))
            bb_scr[hd[h], sl, :] = jnp.broadcast_to(beta_all[:, HEADS + h:HEADS + h + 1], (PAIR, LANES))
            outs = []
            for part, ref in enumerate((q_ref, k_ref, v_ref)):
                s = part * HEADS + h
                xp_scr[s, SUBLANES + r0:SUBLANES + r0 + PAIR, :] = ref[h, sl, :].astype(F32)
                w = cw_ref[s]
                y = w[0:1] * xp_scr[s, CONV_LEAD + r0:CONV_LEAD + r0 + PAIR, :]
                for tap in range(1, GDN_CONV):
                    y = y + w[tap:tap + 1] * xp_scr[s, CONV_LEAD + tap + r0:CONV_LEAD + tap + r0 + PAIR, :]
                outs.append(y * jax.nn.sigmoid(y))
            qc, kc, vc = outs
            qn_scr[hd[h], sl, :] = (qc * lax.rsqrt(jnp.sum(qc * qc, axis=-1, keepdims=True) + NORM_EPS)
                                * (HEAD_DIM ** -0.5))
            kn_scr[hd[h], sl, :] = kc * lax.rsqrt(jnp.sum(kc * kc, axis=-1, keepdims=True) + NORM_EPS)
            vn_scr[hd[h], sl, :] = vc

    def stage_b(p, hd):
        hs = range(len(hd))
        sl = slice(p * PAIR, (p + 1) * PAIR)
        k = [kn_scr[hd[h], sl, :] for h in hs]
        q = [qn_scr[hd[h], sl, :] for h in hs]
        v = [vn_scr[hd[h], sl, :] for h in hs]
        gb = [gb_scr[hd[h], sl, :] for h in hs]
        bb = [bb_scr[hd[h], sl, :] for h in hs]
        parts = [_split2(jnp.concatenate([gb[h], gb[h]], axis=1) * mx) for h in hs]
        gm = [_dot(ls, pt[0]) + _dot(ls, pt[1]) for pt in parts]
        yield
        decay = [jnp.exp(g[:, :PAIR]) for g in gm]
        gi = [g[:, PAIR:] for g in gm]
        glast = [jnp.concatenate([jnp.broadcast_to(g[c * CHUNK - 1:c * CHUNK, :], (CHUNK, LANES))
                                  for c in (1, 2)], axis=0) for g in gi]
        eg = [jnp.exp(g) for g in gi]
        er = [jnp.exp(glast[h] - gi[h]) for h in hs]
        kb = [k[h] * bb[h] for h in hs]
        kk = [_dot_nt(jnp.concatenate([kb[h], q[h]], axis=0).astype(BF16), k[h].astype(BF16)) for h in hs]
        yield
        n = [-(kk[h][:PAIR] * decay[h] * strict_f) for h in hs]
        aqk = [kk[h][PAIR:] * decay[h] * lower_f for h in hs]
        inv = [eye + n[h] for h in hs]
        nb = [n[h].astype(BF16) for h in hs]
        nb = [_dot(nb[h], nb[h]).astype(BF16) for h in hs]
        yield
        for it in range(5):
            if it < 4:
                m = [_dot(jnp.concatenate([inv[h].astype(BF16), nb[h]], axis=0), nb[h]) for h in hs]
                inv = [inv[h] + m[h][:PAIR] for h in hs]
                nb = [m[h][PAIR:].astype(BF16) for h in hs]
            else:
                inv = [inv[h] + _dot(inv[h].astype(BF16), nb[h]) for h in hs]
            yield
        uw = [_dot(inv[h].astype(BF16),
                   jnp.concatenate([v[h] * bb[h], kb[h] * eg[h]], axis=1).astype(BF16)) for h in hs]
        yield
        kd = [(k[h] * er[h]).astype(BF16) for h in hs]
        qg = [q[h] * eg[h] for h in hs]
        for cc in range(2):
            c = 2 * p + cc
            rs = slice(cc * CHUNK, (cc + 1) * CHUNK)
            kwb = [_dot_tn(kd[h][rs], jnp.concatenate([uw[h][rs, LANES:], uw[h][rs, :LANES]], axis=1).astype(BF16))
                   for h in hs]
            for h in hs:
                wq_scr[hd[h], c] = jnp.concatenate([uw[h][rs, LANES:], qg[h][rs]], axis=0).astype(BF16)
                kw_scr[hd[h], c] = kwb[h][:, :LANES].astype(BF16)
                bm_scr[hd[h], c] = kwb[h][:, LANES:]
                cd_scr[hd[h], c] = jnp.broadcast_to(eg[h][(cc + 1) * CHUNK - 1:(cc + 1) * CHUNK, :], (SUBLANES, LANES))
            yield
        for h in hs:
            u_scr[hd[h], sl, :] = uw[h][:, :LANES]
            aqk_scr[hd[h], p] = aqk[h].astype(BF16)

    def stage_c(p, hd):
        hs = range(len(hd))
        sl = slice(p * PAIR, (p + 1) * PAIR)
        c0, c1 = 2 * p, 2 * p + 1
        s0 = [s_scr[hd[h]] for h in hs]
        sb0 = [s0[h].astype(BF16) for h in hs]
        s1 = [s0[h] * cd_scr[hd[h], c0][0:1, :] + (bm_scr[hd[h], c0] - _dot(kw_scr[hd[h], c0], sb0[h])) for h in hs]
        yield
        sb1 = [s1[h].astype(BF16) for h in hs]
        s2 = [s1[h] * cd_scr[hd[h], c1][0:1, :] + (bm_scr[hd[h], c1] - _dot(kw_scr[hd[h], c1], sb1[h])) for h in hs]
        for h in hs:
            s_scr[hd[h]] = s2[h]
        yield
        a0 = [_dot(wq_scr[hd[h], c0], sb0[h]) for h in hs]
        a1 = [_dot(wq_scr[hd[h], c1], sb1[h]) for h in hs]
        yield
        vnew = [(u_scr[hd[h], sl, :] - jnp.concatenate([a0[h][:CHUNK], a1[h][:CHUNK]], axis=0)).astype(BF16)
                for h in hs]
        o = [jnp.concatenate([a0[h][CHUNK:], a1[h][CHUNK:]], axis=0) + _dot(aqk_scr[hd[h], p], vnew[h]) for h in hs]
        yield
        for h in hs:
            z = z_ref[hd[h], sl, :].astype(F32)
            o_ref[hd[h], sl, :] = (_rms(o[h], onw) * (z * jax.nn.sigmoid(z))).astype(BF16)

    def interleave(*gens):
        live = [g for g in gens if g is not None]
        while live:
            for g in list(live):
                try:
                    next(g)
                except StopIteration:
                    live.remove(g)

    stage_a(0)
    for step in range(npair + 1):
        if step + 1 < npair:
            stage_a(step + 1)
        for g0 in range(0, HEADS, GDN_HEAD_GROUP):
            grp = tuple(range(g0, g0 + GDN_HEAD_GROUP))
            interleave(stage_c(step - 1, grp) if step >= 1 else None, stage_b(step, grp) if step < npair else None)

    for s in range(3 * HEADS):
        xp_scr[s, 0:SUBLANES, :] = xp_scr[s, rows:rows + SUBLANES, :]


def _gdn(p3, ab, conv_w, a_log, dt_bias, out_norm_w, q_blk):
    t = p3.shape[1]
    rows = min(SEQ_BLOCK, t)
    nchunk, npair = rows // CHUNK, rows // PAIR
    cw = conv_w.astype(F32).reshape(GDN_CONV, 3 * HEADS, LANES).transpose(1, 0, 2)
    pad = lambda v: jnp.zeros((1, LANES), F32).at[0, :HEADS].set(v.astype(F32))
    ti = jnp.arange(PAIR)
    same = (ti[:, None] // CHUNK) == (ti[None, :] // CHUNK)
    le = same & (ti[None, :] <= ti[:, None])
    gt = same & (ti[None, :] > ti[:, None])
    ls = le.astype(BF16)
    mx = jnp.concatenate([gt.T.astype(F32), jnp.ones((PAIR, LANES), F32)], axis=1)
    slab = lambda off: pl.BlockSpec((HEADS, rows, LANES), lambda b: (q_blk + off, b, 0))
    const2 = lambda b: (0, 0)
    hr = (HEADS, rows, LANES)
    return pl.pallas_call(
        _gdn_body,
        grid=(t // rows,),
        in_specs=[
            slab(0), slab(1), slab(2), slab(3),
            pl.BlockSpec((rows, LANES), lambda b: (b, 0)),
            pl.BlockSpec(cw.shape, lambda b: (0, 0, 0)),
            pl.BlockSpec((1, LANES), const2),
            pl.BlockSpec((1, LANES), const2),
            pl.BlockSpec((1, LANES), const2),
            pl.BlockSpec(ls.shape, const2),
            pl.BlockSpec(mx.shape, const2),
        ],
        out_specs=pl.BlockSpec(hr, lambda b: (0, b, 0)),
        out_shape=jax.ShapeDtypeStruct((HEADS, t, LANES), BF16),
        scratch_shapes=[
            pltpu.VMEM((HEADS, HEAD_DIM, HEAD_DIM), F32),
            pltpu.VMEM((3 * HEADS, rows + SUBLANES, LANES), F32),
            pltpu.VMEM(hr, F32), pltpu.VMEM(hr, F32), pltpu.VMEM(hr, F32),
            pltpu.VMEM(hr, F32), pltpu.VMEM(hr, F32),
            pltpu.VMEM(hr, F32),
            pltpu.VMEM((HEADS, nchunk, 2 * CHUNK, LANES), BF16),
            pltpu.VMEM((HEADS, npair, PAIR, PAIR), BF16),
            pltpu.VMEM((HEADS, nchunk, HEAD_DIM, HEAD_DIM), BF16),
            pltpu.VMEM((HEADS, nchunk, HEAD_DIM, HEAD_DIM), F32),
            pltpu.VMEM((HEADS, nchunk, SUBLANES, LANES), F32),
        ],
        compiler_params=pltpu.CompilerParams(
            dimension_semantics=("arbitrary",), vmem_limit_bytes=VMEM_LIMIT),
        name="gdn",
    )(p3, p3, p3, p3, ab, cw, pad(a_log), pad(dt_bias), out_norm_w.astype(F32).reshape(1, LANES), ls, mx)


def _pad_to(w, axis, mult):
    n = w.shape[axis]
    extra = (-n) % mult
    if extra == 0:
        return w
    widths = [(0, 0)] * w.ndim
    widths[axis] = (0, extra)
    return jnp.pad(w, widths)


@jax.jit
def _forward(x, ffn1_norm, ffn1_w_gate, ffn1_w_up, ffn1_w_down, mix_norm, w_in, gdn_conv,
             gdn_A_log, gdn_dt_bias, gdn_out_norm, att_q_norm, att_k_norm, att_rel_bias,
             w_branch_gdn, w_branch_att, w_out, ffn2_norm, ffn2_w_gate, ffn2_w_up, ffn2_w_down):
    b, t, d = x.shape
    gw = HEADS * HEAD_DIM
    outs = []
    for bi in range(b):
        xb = x[bi]
        depth = ffn1_norm.shape[0]
        for l in range(depth):
            wi = w_in[l]
            o_ab = 4 * gw
            o_att = o_ab + 2 * HEADS
            w_ab = _pad_to(wi[:, o_ab:o_att].astype(BF16), 1, LANES)

            act, wd = _ffn_up(_norm(xb, ffn1_norm[l]), ffn1_w_gate[l], ffn1_w_up[l], ffn1_w_down[l])
            xb, hn, ab = _ffn_down(act, wd, xb, mix_norm[l], w_ab)
            p3 = _proj(hn, wi.T, o_ab // PROJ_TN, o_att - o_ab, (wi.shape[1] - (o_att - o_ab)) // PROJ_TN)
            og3 = _gdn(p3, ab, gdn_conv[l], gdn_A_log[l], gdn_dt_bias[l], gdn_out_norm[l], 0)
            oa3, (wa, wb, wo) = _att(p3, att_q_norm[l], att_k_norm[l], att_rel_bias[l], 4, 5, 6,
                                     (w_branch_gdn[l], w_branch_att[l], w_out[l]))
            xb, hn = _merge(og3, oa3, p3, 7, xb, wa, wb, wo, ffn2_norm[l])

            act, wd = _ffn_up(hn, ffn2_w_gate[l], ffn2_w_up[l], ffn2_w_down[l])
            xb = _ffn_down(act, wd, xb)
        outs.append(xb)
    return jnp.stack(outs, axis=0)


def kernel(x, ffn1_norm, ffn1_w_gate, ffn1_w_up, ffn1_w_down, mix_norm, w_in, gdn_conv, gdn_A_log, gdn_dt_bias, gdn_out_norm, att_q_norm, att_k_norm, att_rel_bias, w_branch_gdn, w_branch_att, w_out, ffn2_norm, ffn2_w_gate, ffn2_w_up, ffn2_w_down):
    return _forward(x, ffn1_norm, ffn1_w_gate, ffn1_w_up, ffn1_w_down, mix_norm, w_in, gdn_conv,
                    gdn_A_log, gdn_dt_bias, gdn_out_norm, att_q_norm, att_k_norm, att_rel_bias,
                    w_branch_gdn, w_branch_att, w_out, ffn2_norm, ffn2_w_gate, ffn2_w_up, ffn2_w_down)
```
